```python
import math
import jax, jax.numpy as jnp
from jax import lax
import numpy as np

D_MODEL = 1024
BATCH = 8
SEQ = 2048
DEPTH = 2
DEC_BATCH = 128
DEC_SEQ = 8
PAST_LEN = 16384
PAGE_SIZE = 128

N_HEADS = 4
DQK = D_MODEL // (2 * N_HEADS)
DV = D_MODEL // N_HEADS
QK_WIDTH = N_HEADS * DQK
V_WIDTH = N_HEADS * DV
GATE_RANK = 16
GATE_TAU = 16.0
D_FF = -(-8 * D_MODEL // (3 * 256)) * 256
CHUNK = 64
EPS = 1e-6
FORGET_BIAS = 3.0
N_MLSTM = (DEPTH + 1) // 2
N_GLA = DEPTH // 2
MLSTM_IN = 2 * QK_WIDTH + 2 * V_WIDTH + 2 * N_HEADS
GLA_IN = 2 * QK_WIDTH + 2 * V_WIDTH + GATE_RANK
SPLIT_MAIN = [QK_WIDTH, 2 * QK_WIDTH, 2 * QK_WIDTH + V_WIDTH, 2 * QK_WIDTH + 2 * V_WIDTH]

kernel_name = "hybrid_mlstm_gla_decoder_step"


def rms_norm(x, g):
    xf = x.astype(jnp.float32)
    y = xf * lax.rsqrt(jnp.mean(xf * xf, axis=-1, keepdims=True) + EPS)
    return (y * g.astype(jnp.float32)).astype(x.dtype)


def head_rms_norm(h, g):
    y = h * lax.rsqrt(jnp.mean(h * h, axis=-1, keepdims=True) + EPS)
    return y * g.astype(jnp.float32).reshape(N_HEADS, DV)


def _to_chunks(a, L):
    B, S = a.shape[:2]
    return jnp.moveaxis(a.reshape((B, S // L, L) + a.shape[2:]), 1, 0)


def _from_chunks(a):
    nc, B, L = a.shape[:3]
    return jnp.moveaxis(a, 0, 1).reshape((B, nc * L) + a.shape[3:])


def mlstm_chunked(q, k, v, i_pre, lf, C0, n0, m0):
    L = math.gcd(CHUNK, q.shape[1])
    causal = jnp.tril(jnp.ones((L, L), dtype=bool))

    def step(carry, inp):
        C, n, m = carry
        qc, kc, vc, ic, fc = inp
        b = jnp.cumsum(jnp.swapaxes(fc, 1, 2), axis=-1)
        ic = jnp.swapaxes(ic, 1, 2)
        D = jnp.where(causal, b[..., :, None] - b[..., None, :] + ic[..., None, :], -jnp.inf)
        inter = b + m[..., None]
        m_t = jnp.maximum(inter, jnp.max(D, axis=-1))
        w_inter = jnp.exp(inter - m_t)
        A = jnp.einsum('blhd,bshd->bhls', qc, kc) * jnp.exp(D - m_t[..., None])
        num = jnp.einsum('bhls,bshv->bhlv', A, vc) + w_inter[..., None] * jnp.einsum('blhd,bhdv->bhlv', qc, C)
        den = jnp.sum(A, axis=-1) + w_inter * jnp.einsum('blhd,bhd->bhl', qc, n)
        h = num / jnp.maximum(jnp.abs(den), jnp.exp(-m_t))[..., None]
        m_new = m_t[..., -1]
        w_s = jnp.exp(b[..., -1:] - b + ic - m_new[..., None])
        decay = jnp.exp(b[..., -1] + m - m_new)
        kw = kc * jnp.swapaxes(w_s, 1, 2)[..., None]
        C_new = decay[..., None, None] * C + jnp.einsum('bshd,bshv->bhdv', kw, vc)
        n_new = decay[..., None] * n + jnp.sum(kw, axis=1)
        return (C_new, n_new, m_new), jnp.swapaxes(h, 1, 2)

    xs = tuple(_to_chunks(a, L) for a in (q, k, v, i_pre, lf))
    carry0 = (C0.astype(jnp.float32), n0.astype(jnp.float32), m0.astype(jnp.float32))
    (C, n, m), h = lax.scan(step, carry0, xs)
    return _from_chunks(h), C, n, m


def gla_chunked(q, k, v, g, S0):
    L = math.gcd(CHUNK, q.shape[1])
    causal = jnp.tril(jnp.ones((L, L), dtype=bool))[None, :, :, None, None]

    def step(S, inp):
        qc, kc, vc, gc = inp
        Bc = jnp.cumsum(gc, axis=1)
        o_inter = jnp.einsum('blhd,bhdv->blhv', qc * jnp.exp(Bc), S)
        decay = jnp.exp(jnp.where(causal, Bc[:, :, None] - Bc[:, None, :], -jnp.inf))
        A = jnp.einsum('blhd,bshd,blshd->bhls', qc, kc, decay)
        o_intra = jnp.einsum('bhls,bshv->blhv', A, vc)
        BL = Bc[:, -1]
        S_new = jnp.exp(BL)[..., None] * S + jnp.einsum('bshd,bshv->bhdv', kc * jnp.exp(BL[:, None] - Bc), vc)
        return S_new, o_inter + o_intra

    xs = tuple(_to_chunks(a, L) for a in (q, k, v, g))
    S, o = lax.scan(step, S0.astype(jnp.float32), xs)
    return _from_chunks(o), S


def mlstm_mixer(x, w_in, b_i, b_f, g_head, w_out, C0, n0, m0):
    B, S, _ = x.shape
    proj = jnp.einsum('bsd,de->bse', x, w_in).astype(jnp.float32)
    q, k, v, o, ig, fg = jnp.split(proj, SPLIT_MAIN + [SPLIT_MAIN[-1] + N_HEADS], axis=-1)
    q = q.reshape(B, S, N_HEADS, DQK)
    k = k.reshape(B, S, N_HEADS, DQK) * (DQK ** -0.5)
    v = v.reshape(B, S, N_HEADS, DV)
    i_pre = ig + b_i.astype(jnp.float32)
    lf = jax.nn.log_sigmoid(fg + b_f.astype(jnp.float32))
    h, C, n, m = mlstm_chunked(q, k, v, i_pre, lf, C0, n0, m0)
    y = jax.nn.sigmoid(o) * head_rms_norm(h, g_head).reshape(B, S, V_WIDTH)
    return jnp.einsum('bse,ed->bsd', y.astype(x.dtype), w_out), C, n, m


def gla_mixer(x, w_in, w_gate2, b_gate, g_head, w_out, S0):
    B, S, _ = x.shape
    proj = jnp.einsum('bsd,de->bse', x, w_in).astype(jnp.float32)
    q, k, v, r, glr = jnp.split(proj, SPLIT_MAIN, axis=-1)
    q = q.reshape(B, S, N_HEADS, DQK) * (DQK ** -0.5)
    k = k.reshape(B, S, N_HEADS, DQK)
    v = v.reshape(B, S, N_HEADS, DV)
    g = jax.nn.log_sigmoid(jnp.einsum('bsr,re->bse', glr, w_gate2.astype(jnp.float32))
                           + b_gate.astype(jnp.float32)) / GATE_TAU
    g = g.reshape(B, S, N_HEADS, DQK)
    o, S_new = gla_chunked(q, k, v, g, S0)
    y = jax.nn.silu(r) * head_rms_norm(o, g_head).reshape(B, S, V_WIDTH)
    return jnp.einsum('bse,ed->bsd', y.astype(x.dtype), w_out), S_new


def swiglu(x, w_gate, w_up, w_down):
    h = jax.nn.silu(jnp.einsum('bsd,df->bsf', x, w_gate)) * jnp.einsum('bsd,df->bsf', x, w_up)
    return jnp.einsum('bsf,fd->bsd', h, w_down)


def trunk(x, C0, n0, m0, S0, w):
    (g_pre_mix, g_post_mix, g_pre_ffn, g_post_ffn,
     w_in_mlstm, b_i_mlstm, b_f_mlstm, g_head_mlstm, w_out_mlstm,
     w_in_gla, w_gate2_gla, b_gate_gla, g_head_gla, w_out_gla,
     w_ffn_gate, w_ffn_up, w_ffn_down) = w
    Cs, ns, ms, Ss = [], [], [], []
    for i in range(DEPTH):
        h = rms_norm(x, g_pre_mix[i])
        j = i // 2
        if i % 2 == 0:
            mix, C, n, m = mlstm_mixer(h, w_in_mlstm[j], b_i_mlstm[j], b_f_mlstm[j], g_head_mlstm[j],
                                       w_out_mlstm[j], C0[j], n0[j], m0[j])
            Cs.append(C); ns.append(n); ms.append(m)
        else:
            mix, S = gla_mixer(h, w_in_gla[j], w_gate2_gla[j], b_gate_gla[j], g_head_gla[j],
                               w_out_gla[j], S0[j])
            Ss.append(S)
        x = x + rms_norm(mix, g_post_mix[i])
        h = rms_norm(x, g_pre_ffn[i])
        x = x + rms_norm(swiglu(h, w_ffn_gate[i], w_ffn_up[i], w_ffn_down[i]), g_post_ffn[i])
    return x, jnp.stack(Cs), jnp.stack(ns), jnp.stack(ms), jnp.stack(Ss)


def setup_inputs(seed: int = 0) -> dict:
    key = jax.random.key(seed)
    ks = jax.random.split(key, 24)
    f32 = jnp.float32
    nrm = lambda k, shape, s: jax.random.normal(k, shape, f32) * s
    gain = lambda k, shape: 1.0 + nrm(k, shape, 0.05)
    return {
        "x_prompt": nrm(ks[0], (BATCH, SEQ, D_MODEL), 1.0),
        "x_sample": nrm(ks[1], (DEC_BATCH, DEC_SEQ, D_MODEL), 1.0),
        "state_mlstm_C": nrm(ks[2], (N_MLSTM, DEC_BATCH, N_HEADS, DQK, DV), 0.1),
        "state_mlstm_n": nrm(ks[3], (N_MLSTM, DEC_BATCH, N_HEADS, DQK), 0.5),
        "state_mlstm_m": nrm(ks[4], (N_MLSTM, DEC_BATCH, N_HEADS), 1.0),
        "state_gla_S": nrm(ks[5], (N_GLA, DEC_BATCH, N_HEADS, DQK, DV), 0.1),
        "g_pre_mix": gain(ks[6], (DEPTH, D_MODEL)),
        "g_post_mix": gain(ks[7], (DEPTH, D_MODEL)),
        "g_pre_ffn": gain(ks[8], (DEPTH, D_MODEL)),
        "g_post_ffn": gain(ks[9], (DEPTH, D_MODEL)),
        "w_in_mlstm": nrm(ks[10], (N_MLSTM, D_MODEL, MLSTM_IN), D_MODEL ** -0.5),
        "b_i_mlstm": nrm(ks[11], (N_MLSTM, N_HEADS), 0.1),
        "b_f_mlstm": FORGET_BIAS + nrm(ks[12], (N_MLSTM, N_HEADS), 0.5),
        "g_head_mlstm": gain(ks[13], (N_MLSTM, V_WIDTH)),
        "w_out_mlstm": nrm(ks[14], (N_MLSTM, V_WIDTH, D_MODEL), V_WIDTH ** -0.5),
        "w_in_gla": nrm(ks[15], (N_GLA, D_MODEL, GLA_IN), D_MODEL ** -0.5),
        "w_gate2_gla": nrm(ks[16], (N_GLA, GATE_RANK, QK_WIDTH), GATE_RANK ** -0.5),
        "b_gate_gla": nrm(ks[17], (N_GLA, QK_WIDTH), 0.1),
        "g_head_gla": gain(ks[18], (N_GLA, V_WIDTH)),
        "w_out_gla": nrm(ks[19], (N_GLA, V_WIDTH, D_MODEL), V_WIDTH ** -0.5),
        "w_ffn_gate": nrm(ks[20], (DEPTH, D_MODEL, D_FF), D_MODEL ** -0.5),
        "w_ffn_up": nrm(ks[21], (DEPTH, D_MODEL, D_FF), D_MODEL ** -0.5),
        "w_ffn_down": nrm(ks[22], (DEPTH, D_FF, D_MODEL), D_FF ** -0.5),
    }


def reference(x_prompt, x_sample, state_mlstm_C, state_mlstm_n, state_mlstm_m, state_gla_S,
              g_pre_mix, g_post_mix, g_pre_ffn, g_post_ffn,
              w_in_mlstm, b_i_mlstm, b_f_mlstm, g_head_mlstm, w_out_mlstm,
              w_in_gla, w_gate2_gla, b_gate_gla, g_head_gla, w_out_gla,
              w_ffn_gate, w_ffn_up, w_ffn_down):
    w = (g_pre_mix, g_post_mix, g_pre_ffn, g_post_ffn,
         w_in_mlstm, b_i_mlstm, b_f_mlstm, g_head_mlstm, w_out_mlstm,
         w_in_gla, w_gate2_gla, b_gate_gla, g_head_gla, w_out_gla,
         w_ffn_gate, w_ffn_up, w_ffn_down)
    Bp = x_prompt.shape[0]
    C0p = jnp.zeros((N_MLSTM, Bp, N_HEADS, DQK, DV), jnp.float32)
    n0p = jnp.zeros((N_MLSTM, Bp, N_HEADS, DQK), jnp.float32)
    m0p = jnp.zeros((N_MLSTM, Bp, N_HEADS), jnp.float32)
    S0p = jnp.zeros((N_GLA, Bp, N_HEADS, DQK, DV), jnp.float32)
    y_prompt, C_p, n_p, m_p, S_p = trunk(x_prompt, C0p, n0p, m0p, S0p, w)
    y_sample, C_s, n_s, m_s, S_s = trunk(x_sample, state_mlstm_C, state_mlstm_n, state_mlstm_m, state_gla_S, w)
    return (y_prompt, y_sample, C_p, n_p, m_p, S_p, C_s, n_s, m_s, S_s)
```

```python
import functools

import numpy as np
import jax
import jax.numpy as jnp
from jax import lax
from jax.experimental import pallas as pl
from jax.experimental.pallas import tpu as pltpu

F32 = jnp.float32
BF16 = jnp.bfloat16

D_MODEL = 1024
N_HEADS = 4
DQK = 128
DV = 256
QK_WIDTH = N_HEADS * DQK
V_WIDTH = N_HEADS * DV
MAIN_WIDTH = 2 * QK_WIDTH + 2 * V_WIDTH
GATE_RANK = 16
GATE_TAU = 16.0
EPS = 1e-6
LANES = 128
SUBLANES = 8
VMEM_LIMIT = 56 * 1024 * 1024

CHUNK = 64
SAMPLE_SEQS_PER_STEP = CHUNK // 8
TM = 1024
TN_IN = 512
TF = 256


def _dot(a, b):
    return jnp.dot(a.astype(BF16), b.astype(BF16), preferred_element_type=F32)


def _dot_nt(a, b):
    return lax.dot_general(a.astype(BF16), b.astype(BF16), (((1,), (1,)), ((), ())),
                           preferred_element_type=F32)


def _dot_tn(a, b):
    return lax.dot_general(a.astype(BF16), b.astype(BF16), (((0,), (0,)), ((), ())),
                           preferred_element_type=F32)


def _split_hi_lo(x):
    hi = x.astype(BF16)
    lo = (x - hi.astype(F32)).astype(BF16)
    return hi, lo


def _log_sigmoid(x):
    return jnp.minimum(x, 0.0) - jnp.log1p(jnp.exp(-jnp.abs(x)))


def _sigmoid(x):
    return 1.0 / (1.0 + jnp.exp(-x))


def _rms(x, gain):
    return x * lax.rsqrt(jnp.mean(x * x, axis=-1, keepdims=True) + EPS) * gain


def _rows_per_segment(values, seg_len):
    parts = [jnp.broadcast_to(v, (seg_len, v.shape[1])) for v in values]
    return parts[0] if len(parts) == 1 else jnp.concatenate(parts, axis=0)


def _inproj_body(x_ref, gain_ref, w_ref, wg_ref, o_ref, og_ref, h_scr):
    @pl.when(pl.program_id(1) == 0)
    def _():
        h = _rms(x_ref[...], gain_ref[...]).astype(BF16)
        h_scr[...] = h
        og_ref[...] = jnp.dot(h, wg_ref[...], preferred_element_type=F32)

    o_ref[...] = jnp.dot(h_scr[...], w_ref[...], preferred_element_type=F32)


def _inproj(x, gain, w_main, w_gate):
    m = x.shape[0]
    grid = (m // TM, MAIN_WIDTH // TN_IN)
    return pl.pallas_call(
        _inproj_body,
        grid=grid,
        in_specs=[
            pl.BlockSpec((TM, D_MODEL), lambda i, j: (i, 0)),
            pl.BlockSpec((1, D_MODEL), lambda i, j: (0, 0)),
            pl.BlockSpec((D_MODEL, TN_IN), lambda i, j: (0, j)),
            pl.BlockSpec((D_MODEL, LANES), lambda i, j: (0, 0)),
        ],
        out_specs=[
            pl.BlockSpec((TM, TN_IN), lambda i, j: (i, j)),
            pl.BlockSpec((TM, LANES), lambda i, j: (i, 0)),
        ],
        out_shape=[
            jax.ShapeDtypeStruct((m, MAIN_WIDTH), F32),
            jax.ShapeDtypeStruct((m, LANES), F32),
        ],
        scratch_shapes=[pltpu.VMEM((TM, D_MODEL), BF16)],
        compiler_params=pltpu.CompilerParams(
            dimension_semantics=("arbitrary", "arbitrary"), vmem_limit_bytes=VMEM_LIMIT),
        name="inproj",
    )(x, gain, w_main, w_gate)


def _tail_body(y_ref, wo_ref, x_ref, gpm_ref, gpf_ref, gqf_ref, wg_ref, wu_ref, wd_ref,
               o_ref, x1_scr, h_scr, acc_scr):
    j = pl.program_id(1)

    @pl.when(j == 0)
    def _():
        mix = jnp.dot(y_ref[...], wo_ref[...], preferred_element_type=F32)
        x1 = x_ref[...] + _rms(mix, gpm_ref[...])
        x1_scr[...] = x1
        h_scr[...] = _rms(x1, gpf_ref[...]).astype(BF16)
        acc_scr[...] = jnp.zeros_like(acc_scr)

    h = h_scr[...]
    gate = jnp.dot(h, wg_ref[...], preferred_element_type=F32)
    up = jnp.dot(h, wu_ref[...], preferred_element_type=F32)
    act = (gate * _sigmoid(gate) * up).astype(BF16)
    acc_scr[...] += jnp.dot(act, wd_ref[...], preferred_element_type=F32)

    @pl.when(j == pl.num_programs(1) - 1)
    def _():
        o_ref[...] = x1_scr[...] + _rms(acc_scr[...], gqf_ref[...])


def _block_tail(y, w_out, x, g_post_mix, g_pre_ffn, g_post_ffn, w_gate, w_up, w_down):
    m = x.shape[0]
    d_ff = w_gate.shape[1]
    grid = (m // TM, d_ff // TF)
    vec = pl.BlockSpec((1, D_MODEL), lambda i, j: (0, 0))
    return pl.pallas_call(
        _tail_body,
        grid=grid,
        in_specs=[
            pl.BlockSpec((TM, V_WIDTH), lambda i, j: (i, 0)),
            pl.BlockSpec((V_WIDTH, D_MODEL), lambda i, j: (0, 0)),
            pl.BlockSpec((TM, D_MODEL), lambda i, j: (i, 0)),
            vec, vec, vec,
            pl.BlockSpec((D_MODEL, TF), lambda i, j: (0, j)),
            pl.BlockSpec((D_MODEL, TF), lambda i, j: (0, j)),
            pl.BlockSpec((TF, D_MODEL), lambda i, j: (j, 0)),
        ],
        out_specs=pl.BlockSpec((TM, D_MODEL), lambda i, j: (i, 0)),
        out_shape=jax.ShapeDtypeStruct((m, D_MODEL), F32),
        scratch_shapes=[
            pltpu.VMEM((TM, D_MODEL), F32),
            pltpu.VMEM((TM, D_MODEL), BF16),
            pltpu.VMEM((TM, D_MODEL), F32),
        ],
        compiler_params=pltpu.CompilerParams(
            dimension_semantics=("arbitrary", "arbitrary"), vmem_limit_bytes=VMEM_LIMIT),
        name="block_tail",
    )(y, w_out, x, g_post_mix, g_pre_ffn, g_post_ffn, w_gate, w_up, w_down)


def _segment_masks(rows, seg_len):
    r = lax.broadcasted_iota(jnp.int32, (rows, rows), 0)
    c = lax.broadcasted_iota(jnp.int32, (rows, rows), 1)
    same = (r // seg_len) == (c // seg_len)
    return r, c, same


def _mlstm_rows(qk_ref, v_ref, o_ref, gcol_ref, grow_ref, bias_row_ref, bias_col_ref, ghead_ref,
                c_in, n_in, m_in, c_out, n_out, m_out, y_ref, rows, seg_len):
    nseg = rows // seg_len
    r, c, same = _segment_masks(rows, seg_len)
    lower = same & (c <= r)
    upper = same & (r <= c)
    row_seg = lax.broadcasted_iota(jnp.int32, (rows, DQK), 0) // seg_len

    gates_col = gcol_ref[...] + bias_row_ref[...]
    gates_row = grow_ref[0] + bias_col_ref[...]
    qk = qk_ref[...]
    v_all = v_ref[...]
    o_all = o_ref[...]
    ghead = ghead_ref[...]

    for h in range(N_HEADS):
        q = qk[:, h * DQK:(h + 1) * DQK]
        k = qk[:, QK_WIDTH + h * DQK:QK_WIDTH + (h + 1) * DQK] * (DQK ** -0.5)
        v = v_all[:, h * DV:(h + 1) * DV]
        i_col = gates_col[:, h:h + 1]
        lf_col = _log_sigmoid(gates_col[:, N_HEADS + h:N_HEADS + h + 1])
        i_row = gates_row[h:h + 1, :]
        lf_row = _log_sigmoid(gates_row[N_HEADS + h:N_HEADS + h + 1, :])

        b_col = jnp.sum(jnp.where(lower, lf_row, 0.0), axis=1, keepdims=True)
        b_row = jnp.sum(jnp.where(upper, lf_col, 0.0), axis=0, keepdims=True)
        b_last = jnp.sum(jnp.where(same, lf_row, 0.0), axis=1, keepdims=True)
        m_prev = _rows_per_segment([m_in[s, h:h + 1, 0:1] for s in range(nseg)], seg_len)
        n_prev = _rows_per_segment([n_in[s, h:h + 1, :] for s in range(nseg)], seg_len)

        src = i_row - b_row
        dmat = jnp.where(lower, b_col + src, -jnp.inf)
        inter = b_col + m_prev
        m_t = jnp.maximum(inter, jnp.max(dmat, axis=1, keepdims=True))
        w_inter = jnp.exp(inter - m_t)
        a = _dot_nt(q, k) * jnp.exp(dmat - m_t)

        q_bf = q.astype(BF16)
        qc_parts = []
        for s in range(nseg):
            qc = jnp.dot(q_bf, c_in[s, h].astype(BF16), preferred_element_type=F32)
            qc_parts.append(qc[s * seg_len:(s + 1) * seg_len] if nseg > 1 else qc)
        qc = qc_parts[0] if nseg == 1 else jnp.concatenate(qc_parts, axis=0)

        num = _dot(a, v) + w_inter * qc
        den = jnp.sum(a, axis=1, keepdims=True) + w_inter * jnp.sum(q * n_prev, axis=1, keepdims=True)
        hval = num * (1.0 / jnp.maximum(jnp.abs(den), jnp.exp(-m_t)))

        dlast = jnp.where(same, b_last + src, -jnp.inf)
        m_new = jnp.maximum(b_last + m_prev, jnp.max(dlast, axis=1, keepdims=True))
        w_s = jnp.exp(b_last - b_col + i_col - m_new)
        decay = jnp.exp(b_last + m_prev - m_new)
        kw = k * w_s
        v_bf = v.astype(BF16)
        for s in range(nseg):
            kw_s = kw if nseg == 1 else jnp.where(row_seg == s, kw, 0.0)
            dec = decay[s * seg_len:s * seg_len + 1, :]
            c_out[s, h] = dec * c_in[s, h] + _dot_tn(kw_s, v_bf)
            n_out[s, h:h + 1, :] = dec * n_in[s, h:h + 1, :] + jnp.sum(kw_s, axis=0, keepdims=True)
            m_out[s, h:h + 1, :] = jnp.broadcast_to(m_new[s * seg_len:s * seg_len + 1, :], (1, LANES))

        hn = _rms(hval, ghead[:, h * DV:(h + 1) * DV])
        y_ref[:, h * DV:(h + 1) * DV] = (_sigmoid(o_all[:, h * DV:(h + 1) * DV]) * hn).astype(y_ref.dtype)


def _mlstm_prompt_body(qk_ref, v_ref, o_ref, gcol_ref, grow_ref, brow_ref, bcol_ref, ghead_ref,
                       y_ref, c_ref, n_ref, m_ref):
    @pl.when(pl.program_id(1) == 0)
    def _():
        c_ref[...] = jnp.zeros_like(c_ref)
        n_ref[...] = jnp.zeros_like(n_ref)
        m_ref[...] = jnp.zeros_like(m_ref)

    _mlstm_rows(qk_ref, v_ref, o_ref, gcol_ref, grow_ref, brow_ref, bcol_ref, ghead_ref,
                c_ref, n_ref, m_ref, c_ref, n_ref, m_ref, y_ref, CHUNK, CHUNK)


def _mlstm_sample_body(qk_ref, v_ref, o_ref, gcol_ref, grow_ref, brow_ref, bcol_ref, ghead_ref,
                       c_in, n_in, m_in, y_ref, c_out, n_out, m_out, *, seg_len):
    _mlstm_rows(qk_ref, v_ref, o_ref, gcol_ref, grow_ref, brow_ref, bcol_ref, ghead_ref,
                c_in, n_in, m_in, c_out, n_out, m_out, y_ref, CHUNK, seg_len)


def _gla_rows(qk_ref, v_ref, r_ref, glr_ref, w2_ref, bg_ref, emat_ref, bd_ref, segind_ref, ghead_ref,
              s_in, s_out, y_ref, rows, seg_len):
    nseg = rows // seg_len
    levels = [hb for hb in (32, 16, 8) if 2 * hb <= seg_len]
    r, c, _ = _segment_masks(rows, seg_len)
    row_seg = lax.broadcasted_iota(jnp.int32, (rows, DQK), 0) // seg_len
    row_id = lax.broadcasted_iota(jnp.int32, (rows, DQK), 0)
    row_in_tile = row_id % SUBLANES

    pre = _dot(glr_ref[...], w2_ref[...]) + bg_ref[...]
    g = _log_sigmoid(pre) * (1.0 / GATE_TAU)
    g_hi, g_lo = _split_hi_lo(g)
    emat = emat_ref[...]
    e = jnp.dot(emat, g_hi, preferred_element_type=F32) + jnp.dot(emat, g_lo, preferred_element_type=F32)
    segind = segind_ref[...]
    total_col = (lax.dot_general(g_hi, segind, (((0,), (0,)), ((), ())), preferred_element_type=F32)
                 + lax.dot_general(g_lo, segind, (((0,), (0,)), ((), ())), preferred_element_type=F32))

    qk = qk_ref[...]
    v_all = v_ref[...]
    r_all = r_ref[...]
    ghead = ghead_ref[...]
    bd = bd_ref[...]

    for h in range(N_HEADS):
        sl = slice(h * DQK, (h + 1) * DQK)
        q = qk[:, sl] * (DQK ** -0.5)
        k = qk[:, QK_WIDTH + h * DQK:QK_WIDTH + (h + 1) * DQK]
        v_bf = v_all[:, h * DV:(h + 1) * DV].astype(BF16)
        bc = e[0:rows, sl]
        brem = e[rows:2 * rows, sl]

        a = jnp.zeros((rows, rows), F32)
        for lvl, hb in enumerate(levels):
            e_l = e[(2 + lvl) * rows:(3 + lvl) * rows, sl]
            u = (jnp.where((row_id % (2 * hb)) >= hb, q, k) * jnp.exp(e_l)).astype(BF16)
            z = _dot_nt(u, u)
            mask = ((r // (2 * hb)) == (c // (2 * hb))) & ((r % (2 * hb)) >= hb) & ((c % (2 * hb)) < hb)
            a = a + jnp.where(mask, z, 0.0)

        prods = []
        for s_loc in range(SUBLANES):
            kb = jnp.concatenate(
                [jnp.broadcast_to(k[t * SUBLANES + s_loc:t * SUBLANES + s_loc + 1, :], (SUBLANES, DQK))
                 for t in range(rows // SUBLANES)], axis=0)
            bcb = jnp.concatenate(
                [jnp.broadcast_to(bc[t * SUBLANES + s_loc:t * SUBLANES + s_loc + 1, :], (SUBLANES, DQK))
                 for t in range(rows // SUBLANES)], axis=0)
            expo = jnp.where(row_in_tile >= s_loc, bc - bcb, -jnp.inf)
            prods.append((q * kb * jnp.exp(expo)).astype(BF16))
        tile_sums = jnp.dot(jnp.concatenate(prods, axis=1), bd, preferred_element_type=F32)
        a = a + jnp.where((r // SUBLANES) == (c // SUBLANES), tile_sums, 0.0)

        o = jnp.dot(a.astype(BF16), v_bf, preferred_element_type=F32)
        qq = (q * jnp.exp(bc)).astype(BF16)
        kk = k * jnp.exp(brem)
        inter_parts = []
        for s in range(nseg):
            st = s_in[s, h]
            oi = jnp.dot(qq, st.astype(BF16), preferred_element_type=F32)
            inter_parts.append(oi[s * seg_len:(s + 1) * seg_len] if nseg > 1 else oi)
            kk_s = kk if nseg == 1 else jnp.where(row_seg == s, kk, 0.0)
            s_out[s, h] = st * jnp.exp(total_col[sl, s:s + 1]) + _dot_tn(kk_s, v_bf)
        o = o + (inter_parts[0] if nseg == 1 else jnp.concatenate(inter_parts, axis=0))

        hn = _rms(o, ghead[:, h * DV:(h + 1) * DV])
        gate = r_all[:, h * DV:(h + 1) * DV]
        y_ref[:, h * DV:(h + 1) * DV] = (gate * _sigmoid(gate) * hn).astype(y_ref.dtype)


def _gla_prompt_body(qk_ref, v_ref, r_ref, glr_ref, w2_ref, bg_ref, emat_ref, bd_ref, segind_ref, ghead_ref,
                     y_ref, s_ref):
    @pl.when(pl.program_id(1) == 0)
    def _():
        s_ref[...] = jnp.zeros_like(s_ref)

    _gla_rows(qk_ref, v_ref, r_ref, glr_ref, w2_ref, bg_ref, emat_ref, bd_ref, segind_ref, ghead_ref,
              s_ref, s_ref, y_ref, CHUNK, CHUNK)


def _gla_sample_body(qk_ref, v_ref, r_ref, glr_ref, w2_ref, bg_ref, emat_ref, bd_ref, segind_ref, ghead_ref,
                     s_in, y_ref, s_out, *, seg_len):
    _gla_rows(qk_ref, v_ref, r_ref, glr_ref, w2_ref, bg_ref, emat_ref, bd_ref, segind_ref, ghead_ref,
              s_in, s_out, y_ref, CHUNK, seg_len)


def _gla_constants(rows, seg_len):
    t = np.arange(rows)
    seg = t // seg_len
    same = seg[:, None] == seg[None, :]
    mats = [same & (t[None, :] <= t[:, None]), same & (t[None, :] > t[:, None])]
    for hb in (32, 16, 8):
        if 2 * hb > seg_len:
            continue
        ref = (t // (2 * hb)) * (2 * hb) + hb - 1
        upper_half = (t % (2 * hb)) >= hb
        q_side = (t[None, :] > ref[:, None]) & (t[None, :] <= t[:, None])
        k_side = (t[None, :] > t[:, None]) & (t[None, :] <= ref[:, None])
        mats.append(np.where(upper_half[:, None], q_side, k_side))
    emat = np.concatenate(mats, axis=0).astype(np.float32)
    lane = np.arange(SUBLANES * DQK) // DQK
    bd = (lane[:, None] == (t[None, :] % SUBLANES)).astype(np.float32)
    segind = (seg[:, None] == np.arange(LANES)[None, :]).astype(np.float32)
    return jnp.asarray(emat, BF16), jnp.asarray(bd, BF16), jnp.asarray(segind, BF16)


def _const_spec(shape):
    nd = len(shape)
    return pl.BlockSpec(shape, lambda *_: (0,) * nd)


def _mixer_row_specs(row_block):
    return [
        pl.BlockSpec((CHUNK, 2 * QK_WIDTH), lambda *ids: (row_block(*ids), 0)),
        pl.BlockSpec((CHUNK, V_WIDTH), lambda *ids: (row_block(*ids), 1)),
        pl.BlockSpec((CHUNK, V_WIDTH), lambda *ids: (row_block(*ids), 2)),
        pl.BlockSpec((CHUNK, LANES), lambda *ids: (row_block(*ids), 0)),
    ]


def _mlstm_mixer(proj, gcol, bias_i, bias_f, ghead, c0, n0, m0, n_prompt_seq, prompt_len):
    m_rows = proj.shape[0]
    n_prompt = n_prompt_seq * prompt_len
    n_sample_seq = c0.shape[0]
    seg_len = (m_rows - n_prompt) // n_sample_seq
    grow = jnp.transpose(gcol[:, :2 * N_HEADS].reshape(m_rows // CHUNK, CHUNK, 2 * N_HEADS), (0, 2, 1))
    bias = jnp.concatenate([bias_i, bias_f]).astype(F32)
    bias_row = jnp.zeros((1, LANES), F32).at[0, :2 * N_HEADS].set(bias)
    bias_col = bias.reshape(2 * N_HEADS, 1)
    ghead = ghead.reshape(1, V_WIDTH)
    consts = [_const_spec((1, LANES)), _const_spec((2 * N_HEADS, 1)), _const_spec((1, V_WIDTH))]
    params = pltpu.CompilerParams(dimension_semantics=("arbitrary",) * 2, vmem_limit_bytes=VMEM_LIMIT)

    chunks = prompt_len // CHUNK
    blk = lambda b, ci: b * chunks + ci
    y_p, c_p, n_p, m_p = pl.pallas_call(
        _mlstm_prompt_body,
        grid=(n_prompt_seq, chunks),
        in_specs=_mixer_row_specs(blk) + [pl.BlockSpec((1, 2 * N_HEADS, CHUNK), lambda b, ci: (blk(b, ci), 0, 0))]
        + consts,
        out_specs=[
            pl.BlockSpec((CHUNK, V_WIDTH), lambda b, ci: (blk(b, ci), 0)),
            pl.BlockSpec((1, N_HEADS, DQK, DV), lambda b, ci: (b, 0, 0, 0)),
            pl.BlockSpec((1, N_HEADS, DQK), lambda b, ci: (b, 0, 0)),
            pl.BlockSpec((1, N_HEADS, LANES), lambda b, ci: (b, 0, 0)),
        ],
        out_shape=[
            jax.ShapeDtypeStruct((n_prompt, V_WIDTH), BF16),
            jax.ShapeDtypeStruct((n_prompt_seq, N_HEADS, DQK, DV), F32),
            jax.ShapeDtypeStruct((n_prompt_seq, N_HEADS, DQK), F32),
            jax.ShapeDtypeStruct((n_prompt_seq, N_HEADS, LANES), F32),
        ],
        compiler_params=params,
        name="mlstm_prompt",
    )(proj, proj, proj, gcol, grow, bias_row, bias_col, ghead)

    spb = CHUNK // seg_len
    first = n_prompt // CHUNK
    sblk = lambda i: first + i
    m0b = jnp.broadcast_to(m0[:, :, None], m0.shape + (LANES,))
    y_s, c_s, n_s, m_s = pl.pallas_call(
        functools.partial(_mlstm_sample_body, seg_len=seg_len),
        grid=(n_sample_seq // spb,),
        in_specs=_mixer_row_specs(sblk) + [pl.BlockSpec((1, 2 * N_HEADS, CHUNK), lambda i: (sblk(i), 0, 0))]
        + consts + [
            pl.BlockSpec((spb, N_HEADS, DQK, DV), lambda i: (i, 0, 0, 0)),
            pl.BlockSpec((spb, N_HEADS, DQK), lambda i: (i, 0, 0)),
            pl.BlockSpec((spb, N_HEADS, LANES), lambda i: (i, 0, 0)),
        ],
        out_specs=[
            pl.BlockSpec((CHUNK, V_WIDTH), lambda i: (i, 0)),
            pl.BlockSpec((spb, N_HEADS, DQK, DV), lambda i: (i, 0, 0, 0)),
            pl.BlockSpec((spb, N_HEADS, DQK), lambda i: (i, 0, 0)),
            pl.BlockSpec((spb, N_HEADS, LANES), lambda i: (i, 0, 0)),
        ],
        out_shape=[
            jax.ShapeDtypeStruct((m_rows - n_prompt, V_WIDTH), BF16),
            jax.ShapeDtypeStruct(c0.shape, F32),
            jax.ShapeDtypeStruct(n0.shape, F32),
            jax.ShapeDtypeStruct(m0b.shape, F32),
        ],
        compiler_params=pltpu.CompilerParams(dimension_semantics=("arbitrary",), vmem_limit_bytes=VMEM_LIMIT),
        name="mlstm_sample",
    )(proj, proj, proj, gcol, grow, bias_row, bias_col, ghead, c0, n0, m0b)

    y = jnp.concatenate([y_p, y_s], axis=0)
    return y, (c_p, n_p, m_p[:, :, 0]), (c_s, n_s, m_s[:, :, 0])


def _gla_mixer(proj, glr, w_gate2, b_gate, ghead, s0, n_prompt_seq, prompt_len):
    m_rows = proj.shape[0]
    n_prompt = n_prompt_seq * prompt_len
    n_sample_seq = s0.shape[0]
    seg_len = (m_rows - n_prompt) // n_sample_seq
    w2 = jnp.zeros((LANES, QK_WIDTH), BF16).at[:GATE_RANK].set(w_gate2.astype(BF16))
    bg = b_gate.reshape(1, QK_WIDTH).astype(F32)
    ghead = ghead.reshape(1, V_WIDTH)

    def const_inputs(seg):
        emat, bd, segind = _gla_constants(CHUNK, seg)
        arrays = [w2, bg, emat, bd, segind, ghead]
        return arrays, [_const_spec(a.shape) for a in arrays]

    chunks = prompt_len // CHUNK
    blk = lambda b, ci: b * chunks + ci
    arrays, specs = const_inputs(CHUNK)
    y_p, s_p = pl.pallas_call(
        _gla_prompt_body,
        grid=(n_prompt_seq, chunks),
        in_specs=_mixer_row_specs(blk) + specs,
        out_specs=[
            pl.BlockSpec((CHUNK, V_WIDTH), lambda b, ci: (blk(b, ci), 0)),
            pl.BlockSpec((1, N_HEADS, DQK, DV), lambda b, ci: (b, 0, 0, 0)),
        ],
        out_shape=[
            jax.ShapeDtypeStruct((n_prompt, V_WIDTH), BF16),
            jax.ShapeDtypeStruct((n_prompt_seq, N_HEADS, DQK, DV), F32),
        ],
        compiler_params=pltpu.CompilerParams(dimension_semantics=("arbitrary",) * 2, vmem_limit_bytes=VMEM_LIMIT),
        name="gla_prompt",
    )(proj, proj, proj, glr, *arrays)

    spb = CHUNK // seg_len
    first = n_prompt // CHUNK
    sblk = lambda i: first + i
    arrays, specs = const_inputs(seg_len)
    y_s, s_s = pl.pallas_call(
        functools.partial(_gla_sample_body, seg_len=seg_len),
        grid=(n_sample_seq // spb,),
        in_specs=_mixer_row_specs(sblk) + specs + [pl.BlockSpec((spb, N_HEADS, DQK, DV), lambda i: (i, 0, 0, 0))],
        out_specs=[
            pl.BlockSpec((CHUNK, V_WIDTH), lambda i: (i, 0)),
            pl.BlockSpec((spb, N_HEADS, DQK, DV), lambda i: (i, 0, 0, 0)),
        ],
        out_shape=[
            jax.ShapeDtypeStruct((m_rows - n_prompt, V_WIDTH), BF16),
            jax.ShapeDtypeStruct(s0.shape, F32),
        ],
        compiler_params=pltpu.CompilerParams(dimension_semantics=("arbitrary",), vmem_limit_bytes=VMEM_LIMIT),
        name="gla_sample",
    )(proj, proj, proj, glr, *arrays, s0)

    return jnp.concatenate([y_p, y_s], axis=0), s_p, s_s


def _pad_gate_columns(w):
    return jnp.zeros((D_MODEL, LANES), BF16).at[:, :w.shape[1]].set(w.astype(BF16))


def kernel(x_prompt, x_sample, state_mlstm_C, state_mlstm_n, state_mlstm_m, state_gla_S, g_pre_mix, g_post_mix, g_pre_ffn, g_post_ffn, w_in_mlstm, b_i_mlstm, b_f_mlstm, g_head_mlstm, w_out_mlstm, w_in_gla, w_gate2_gla, b_gate_gla, g_head_gla, w_out_gla, w_ffn_gate, w_ffn_up, w_ffn_down):
    bp, sp, d = x_prompt.shape
    bs, ss, _ = x_sample.shape
    depth = g_pre_mix.shape[0]
    assert d == D_MODEL and sp % CHUNK == 0 and CHUNK % ss == 0 and (bs * ss) % CHUNK == 0
    assert (bp * sp + bs * ss) % TM == 0

    x = jnp.concatenate([x_prompt.reshape(bp * sp, d), x_sample.reshape(bs * ss, d)], axis=0)
    vec = lambda g: g.reshape(1, D_MODEL).astype(F32)

    prompt_states = {"C": [], "n": [], "m": [], "S": []}
    sample_states = {"C": [], "n": [], "m": [], "S": []}
    for layer in range(depth):
        j = layer // 2
        if layer % 2 == 0:
            w_in = w_in_mlstm[j]
            proj, gcol = _inproj(x, vec(g_pre_mix[layer]), w_in[:, :MAIN_WIDTH].astype(BF16),
                                 _pad_gate_columns(w_in[:, MAIN_WIDTH:]))
            y, st_p, st_s = _mlstm_mixer(proj, gcol, b_i_mlstm[j], b_f_mlstm[j], g_head_mlstm[j],
                                         state_mlstm_C[j], state_mlstm_n[j], state_mlstm_m[j], bp, sp)
            for dst, st in ((prompt_states, st_p), (sample_states, st_s)):
                dst["C"].append(st[0]); dst["n"].append(st[1]); dst["m"].append(st[2])
            w_out = w_out_mlstm[j]
        else:
            w_in = w_in_gla[j]
            proj, glr = _inproj(x, vec(g_pre_mix[layer]), w_in[:, :MAIN_WIDTH].astype(BF16),
                                _pad_gate_columns(w_in[:, MAIN_WIDTH:]))
            y, s_p, s_s = _gla_mixer(proj, glr, w_gate2_gla[j], b_gate_gla[j], g_head_gla[j],
                                     state_gla_S[j], bp, sp)
            prompt_states["S"].append(s_p)
            sample_states["S"].append(s_s)
            w_out = w_out_gla[j]
        x = _block_tail(y, w_out.astype(BF16), x, vec(g_post_mix[layer]), vec(g_pre_ffn[layer]),
                        vec(g_post_ffn[layer]), w_ffn_gate[layer].astype(BF16), w_ffn_up[layer].astype(BF16),
                        w_ffn_down[layer].astype(BF16))

    y_prompt = x[:bp * sp].reshape(bp, sp, d)
    y_sample = x[bp * sp:].reshape(bs, ss, d)
    stack = lambda xs: jnp.stack(xs)
    return (y_prompt, y_sample,
            stack(prompt_states["C"]), stack(prompt_states["n"]), stack(prompt_states["m"]), stack(prompt_states["S"]),
            stack(sample_states["C"]), stack(sample_states["n"]), stack(sample_states["m"]), stack(sample_states["S"]))
```

```python
import functools

import numpy as np
import jax
import jax.numpy as jnp
from jax import lax
from jax.experimental import pallas as pl
from jax.experimental.pallas import tpu as pltpu

F32 = jnp.float32
BF16 = jnp.bfloat16

D_MODEL = 1024
N_HEADS = 4
DQK = 128
DV = 256
QK_WIDTH = N_HEADS * DQK
V_WIDTH = N_HEADS * DV
MAIN_WIDTH = 2 * QK_WIDTH + 2 * V_WIDTH
GATE_RANK = 16
GATE_TAU = 16.0
EPS = 1e-6
LANES = 128
SUBLANES = 8
VMEM_LIMIT = 56 * 1024 * 1024

PROMPT_ROWS = 128
SAMPLE_ROWS = 64
LEVELS = (64, 32, 16, 8)
TM = 1024
TN_IN = 512
TF = 256


def _dot(a, b):
    return jnp.dot(a.astype(BF16), b.astype(BF16), preferred_element_type=F32)


def _dot_nt(a, b):
    return lax.dot_general(a.astype(BF16), b.astype(BF16), (((1,), (1,)), ((), ())),
                           preferred_element_type=F32)


def _dot_tn(a, b):
    return lax.dot_general(a.astype(BF16), b.astype(BF16), (((0,), (0,)), ((), ())),
                           preferred_element_type=F32)


def _split_hi_lo(x):
    hi = x.astype(BF16)
    lo = (x - hi.astype(F32)).astype(BF16)
    return hi, lo


def _log_sigmoid(x):
    return jnp.minimum(x, 0.0) - jnp.log1p(jnp.exp(-jnp.abs(x)))


def _sigmoid(x):
    return 1.0 / (1.0 + jnp.exp(-x))


def _rms(x, gain):
    return x * lax.rsqrt(jnp.mean(x * x, axis=-1, keepdims=True) + EPS) * gain


def _rows_per_segment(values, seg_len):
    parts = [jnp.broadcast_to(v, (seg_len, v.shape[1])) for v in values]
    return parts[0] if len(parts) == 1 else jnp.concatenate(parts, axis=0)


def _const_spec(shape):
    nd = len(shape)
    return pl.BlockSpec(shape, lambda *_: (0,) * nd)


def _two_group_specs(n_prompt_blocks, width):
    prompt = pl.BlockSpec((TM, width), lambda i, j: (jnp.minimum(i, n_prompt_blocks - 1), 0))
    sample = pl.BlockSpec((TM, width), lambda i, j: (jnp.maximum(i - n_prompt_blocks, 0), 0))
    return prompt, sample


def _inproj_body(xp_ref, xs_ref, gain_ref, w_ref, wg_ref, o_ref, og_ref, h_scr, *, n_prompt_blocks):
    i = pl.program_id(0)
    first = pl.program_id(1) == 0

    def prologue(x_ref):
        h = _rms(x_ref[...], gain_ref[...]).astype(BF16)
        h_scr[...] = h
        og_ref[...] = jnp.dot(h, wg_ref[...], preferred_element_type=F32)

    pl.when(first & (i < n_prompt_blocks))(lambda: prologue(xp_ref))
    pl.when(first & (i >= n_prompt_blocks))(lambda: prologue(xs_ref))
    o_ref[...] = jnp.dot(h_scr[...], w_ref[...], preferred_element_type=F32)


def _inproj(xp, xs, gain, w_main, w_gate):
    npb = xp.shape[0] // TM
    m = xp.shape[0] + xs.shape[0]
    spec_p, spec_s = _two_group_specs(npb, D_MODEL)
    return pl.pallas_call(
        functools.partial(_inproj_body, n_prompt_blocks=npb),
        grid=(m // TM, MAIN_WIDTH // TN_IN),
        in_specs=[
            spec_p, spec_s,
            pl.BlockSpec((1, D_MODEL), lambda i, j: (0, 0)),
            pl.BlockSpec((D_MODEL, TN_IN), lambda i, j: (0, j)),
            pl.BlockSpec((D_MODEL, LANES), lambda i, j: (0, 0)),
        ],
        out_specs=[
            pl.BlockSpec((TM, TN_IN), lambda i, j: (i, j)),
            pl.BlockSpec((TM, LANES), lambda i, j: (i, 0)),
        ],
        out_shape=[
            jax.ShapeDtypeStruct((m, MAIN_WIDTH), F32),
            jax.ShapeDtypeStruct((m, LANES), F32),
        ],
        scratch_shapes=[pltpu.VMEM((TM, D_MODEL), BF16)],
        compiler_params=pltpu.CompilerParams(
            dimension_semantics=("arbitrary", "arbitrary"), vmem_limit_bytes=VMEM_LIMIT),
        name="inproj",
    )(xp, xs, gain, w_main, w_gate)


def _tail_body(y_ref, wo_ref, xp_ref, xs_ref, gpm_ref, gpf_ref, gqf_ref, wg_ref, wu_ref, wd_ref,
               op_ref, os_ref, x1_scr, h_scr, acc_scr, *, n_prompt_blocks):
    i = pl.program_id(0)
    j = pl.program_id(1)
    last = j == pl.num_programs(1) - 1

    def prologue(x_ref):
        mix = jnp.dot(y_ref[...], wo_ref[...], preferred_element_type=F32)
        x1 = x_ref[...] + _rms(mix, gpm_ref[...])
        x1_scr[...] = x1
        h_scr[...] = _rms(x1, gpf_ref[...]).astype(BF16)
        acc_scr[...] = jnp.zeros_like(acc_scr)

    pl.when((j == 0) & (i < n_prompt_blocks))(lambda: prologue(xp_ref))
    pl.when((j == 0) & (i >= n_prompt_blocks))(lambda: prologue(xs_ref))

    h = h_scr[...]
    gate = jnp.dot(h, wg_ref[...], preferred_element_type=F32)
    up = jnp.dot(h, wu_ref[...], preferred_element_type=F32)
    act = (gate * _sigmoid(gate) * up).astype(BF16)
    acc_scr[...] += jnp.dot(act, wd_ref[...], preferred_element_type=F32)

    def epilogue(o_ref):
        o_ref[...] = x1_scr[...] + _rms(acc_scr[...], gqf_ref[...])

    pl.when(last & (i < n_prompt_blocks))(lambda: epilogue(op_ref))
    pl.when(last & (i >= n_prompt_blocks))(lambda: epilogue(os_ref))


def _block_tail(y, w_out, xp, xs, g_post_mix, g_pre_ffn, g_post_ffn, w_gate, w_up, w_down):
    npb = xp.shape[0] // TM
    m = xp.shape[0] + xs.shape[0]
    d_ff = w_gate.shape[1]
    spec_p, spec_s = _two_group_specs(npb, D_MODEL)
    vec = pl.BlockSpec((1, D_MODEL), lambda i, j: (0, 0))
    return pl.pallas_call(
        functools.partial(_tail_body, n_prompt_blocks=npb),
        grid=(m // TM, d_ff // TF),
        in_specs=[
            pl.BlockSpec((TM, V_WIDTH), lambda i, j: (i, 0)),
            pl.BlockSpec((V_WIDTH, D_MODEL), lambda i, j: (0, 0)),
            spec_p, spec_s,
            vec, vec, vec,
            pl.BlockSpec((D_MODEL, TF), lambda i, j: (0, j)),
            pl.BlockSpec((D_MODEL, TF), lambda i, j: (0, j)),
            pl.BlockSpec((TF, D_MODEL), lambda i, j: (j, 0)),
        ],
        out_specs=[spec_p, spec_s],
        out_shape=[jax.ShapeDtypeStruct(xp.shape, F32), jax.ShapeDtypeStruct(xs.shape, F32)],
        scratch_shapes=[
            pltpu.VMEM((TM, D_MODEL), F32),
            pltpu.VMEM((TM, D_MODEL), BF16),
            pltpu.VMEM((TM, D_MODEL), F32),
        ],
        compiler_params=pltpu.CompilerParams(
            dimension_semantics=("arbitrary", "arbitrary"), vmem_limit_bytes=VMEM_LIMIT),
        name="block_tail",
    )(y, w_out, xp, xs, g_post_mix, g_pre_ffn, g_post_ffn, w_gate, w_up, w_down)


def _segment_structure(rows, seg_len):
    t = np.arange(rows)
    seg = t // seg_len
    same = seg[:, None] == seg[None, :]
    cum = same & (t[None, :] <= t[:, None])
    return t, seg, same, cum


def _mlstm_constants(rows, seg_len):
    _, _, same, cum = _segment_structure(rows, seg_len)
    col = np.concatenate([cum, same], axis=0).astype(np.float32)
    row = cum.T.astype(np.float32)
    return jnp.asarray(col, BF16), jnp.asarray(row, BF16)


def _mlstm_rows(qk_ref, v_ref, o_ref, gcol_ref, grow_ref, bias_row_ref, bias_col_ref, ccol_ref, crow_ref,
                ghead_ref, c_in, n_in, m_in, c_out, n_out, m_out, y_ref, rows, seg_len):
    nseg = rows // seg_len
    r = lax.broadcasted_iota(jnp.int32, (rows, rows), 0)
    c = lax.broadcasted_iota(jnp.int32, (rows, rows), 1)
    same = None if nseg == 1 else (r // seg_len) == (c // seg_len)
    lower = (c <= r) if nseg == 1 else same & (c <= r)
    row_seg = lax.broadcasted_iota(jnp.int32, (rows, DQK), 0) // seg_len

    pre_col = gcol_ref[...] + bias_row_ref[...]
    pre_row = grow_ref[0] + bias_col_ref[...]
    ccol = ccol_ref[...]
    crow = crow_ref[...]
    hi, lo = _split_hi_lo(_log_sigmoid(pre_col))
    sums_col = jnp.dot(ccol, hi, preferred_element_type=F32) + jnp.dot(ccol, lo, preferred_element_type=F32)
    hi, lo = _split_hi_lo(_log_sigmoid(pre_row))
    sums_row = jnp.dot(hi, crow, preferred_element_type=F32) + jnp.dot(lo, crow, preferred_element_type=F32)

    qk = qk_ref[...]
    v_all = v_ref[...]
    o_all = o_ref[...]
    ghead = ghead_ref[...]

    for h in range(N_HEADS):
        f = N_HEADS + h
        q = qk[:, h * DQK:(h + 1) * DQK]
        k = qk[:, QK_WIDTH + h * DQK:QK_WIDTH + (h + 1) * DQK] * (DQK ** -0.5)
        v_bf = v_all[:, h * DV:(h + 1) * DV].astype(BF16)
        i_col = pre_col[:, h:h + 1]
        b_col = sums_col[0:rows, f:f + 1]
        b_last = sums_col[rows:2 * rows, f:f + 1]
        src = pre_row[h:h + 1, :] - sums_row[f:f + 1, :]
        m_prev = _rows_per_segment([m_in[s, h:h + 1, 0:1] for s in range(nseg)], seg_len)
        n_prev = _rows_per_segment([n_in[s, h:h + 1, :] for s in range(nseg)], seg_len)

        dmat = jnp.where(lower, b_col + src, -jnp.inf)
        inter = b_col + m_prev
        m_t = jnp.maximum(inter, jnp.max(dmat, axis=1, keepdims=True))
        w_inter = jnp.exp(inter - m_t)
        a = _dot_nt(q, k) * jnp.exp(dmat - m_t)

        q_bf = q.astype(BF16)
        qc_parts = []
        for s in range(nseg):
            qc = jnp.dot(q_bf, c_in[s, h].astype(BF16), preferred_element_type=F32)
            qc_parts.append(qc[s * seg_len:(s + 1) * seg_len] if nseg > 1 else qc)
        qc = qc_parts[0] if nseg == 1 else jnp.concatenate(qc_parts, axis=0)

        num = jnp.dot(a.astype(BF16), v_bf, preferred_element_type=F32) + w_inter * qc
        den = jnp.sum(a, axis=1, keepdims=True) + w_inter * jnp.sum(q * n_prev, axis=1, keepdims=True)
        hval = num * (1.0 / jnp.maximum(jnp.abs(den), jnp.exp(-m_t)))

        dlast = b_last + src
        if nseg > 1:
            dlast = jnp.where(same, dlast, -jnp.inf)
        m_new = jnp.maximum(b_last + m_prev, jnp.max(dlast, axis=1, keepdims=True))
        w_s = jnp.exp(b_last - b_col + i_col - m_new)
        decay = jnp.exp(b_last + m_prev - m_new)
        kw = k * w_s
        for s in range(nseg):
            kw_s = kw if nseg == 1 else jnp.where(row_seg == s, kw, 0.0)
            dec = decay[s * seg_len:s * seg_len + 1, :]
            c_out[s, h] = dec * c_in[s, h] + _dot_tn(kw_s, v_bf)
            n_out[s, h:h + 1, :] = dec * n_in[s, h:h + 1, :] + jnp.sum(kw_s, axis=0, keepdims=True)
            m_out[s, h:h + 1, :] = jnp.broadcast_to(m_new[s * seg_len:s * seg_len + 1, :], (1, LANES))

        hn = _rms(hval, ghead[:, h * DV:(h + 1) * DV])
        y_ref[:, h * DV:(h + 1) * DV] = (_sigmoid(o_all[:, h * DV:(h + 1) * DV]) * hn).astype(y_ref.dtype)


def _mlstm_prompt_body(qk_ref, v_ref, o_ref, gcol_ref, grow_ref, brow_ref, bcol_ref, ccol_ref, crow_ref,
                       ghead_ref, y_ref, c_ref, n_ref, m_ref):
    @pl.when(pl.program_id(1) == 0)
    def _():
        c_ref[...] = jnp.zeros_like(c_ref)
        n_ref[...] = jnp.zeros_like(n_ref)
        m_ref[...] = jnp.zeros_like(m_ref)

    _mlstm_rows(qk_ref, v_ref, o_ref, gcol_ref, grow_ref, brow_ref, bcol_ref, ccol_ref, crow_ref, ghead_ref,
                c_ref, n_ref, m_ref, c_ref, n_ref, m_ref, y_ref, PROMPT_ROWS, PROMPT_ROWS)


def _mlstm_sample_body(qk_ref, v_ref, o_ref, gcol_ref, grow_ref, brow_ref, bcol_ref, ccol_ref, crow_ref,
                       ghead_ref, c_in, n_in, m_in, y_any, y_ref, c_out, n_out, m_out, *, seg_len):
    del y_any
    _mlstm_rows(qk_ref, v_ref, o_ref, gcol_ref, grow_ref, brow_ref, bcol_ref, ccol_ref, crow_ref, ghead_ref,
                c_in, n_in, m_in, c_out, n_out, m_out, y_ref, SAMPLE_ROWS, seg_len)


def _gla_constants(rows, seg_len):
    t, seg, same, cum = _segment_structure(rows, seg_len)
    mats = [cum, same & (t[None, :] > t[:, None])]
    for hb in LEVELS:
        if 2 * hb > seg_len:
            continue
        ref = (t // (2 * hb)) * (2 * hb) + hb - 1
        upper_half = (t % (2 * hb)) >= hb
        q_side = (t[None, :] > ref[:, None]) & (t[None, :] <= t[:, None])
        k_side = (t[None, :] > t[:, None]) & (t[None, :] <= ref[:, None])
        mats.append(np.where(upper_half[:, None], q_side, k_side))
    emat = np.concatenate(mats, axis=0).astype(np.float32)
    lane = np.arange(SUBLANES * DQK) // DQK
    bd = (lane[:, None] == (t[None, :] % SUBLANES)).astype(np.float32)
    segind = (seg[:, None] == np.arange(LANES)[None, :]).astype(np.float32)
    return jnp.asarray(emat, BF16), jnp.asarray(bd, BF16), jnp.asarray(segind, BF16)


def _gla_rows(qk_ref, v_ref, r_ref, glr_ref, w2_ref, bg_ref, emat_ref, bd_ref, segind_ref, ghead_ref,
              s_in, s_out, y_ref, rows, seg_len):
    nseg = rows // seg_len
    levels = [hb for hb in LEVELS if 2 * hb <= seg_len]
    r = lax.broadcasted_iota(jnp.int32, (rows, rows), 0)
    c = lax.broadcasted_iota(jnp.int32, (rows, rows), 1)
    row_id = lax.broadcasted_iota(jnp.int32, (rows, DQK), 0)
    row_seg = row_id // seg_len
    row_in_tile = row_id % SUBLANES

    pre = _dot(glr_ref[...], w2_ref[...]) + bg_ref[...]
    g = _log_sigmoid(pre) * (1.0 / GATE_TAU)
    g_hi, g_lo = _split_hi_lo(g)
    emat = emat_ref[...]
    e = jnp.dot(emat, g_hi, preferred_element_type=F32) + jnp.dot(emat, g_lo, preferred_element_type=F32)
    segind = segind_ref[...]
    total_col = (lax.dot_general(g_hi, segind, (((0,), (0,)), ((), ())), preferred_element_type=F32)
                 + lax.dot_general(g_lo, segind, (((0,), (0,)), ((), ())), preferred_element_type=F32))

    qk = qk_ref[...]
    v_all = v_ref[...]
    r_all = r_ref[...]
    ghead = ghead_ref[...]
    bd = bd_ref[...]

    for h in range(N_HEADS):
        sl = slice(h * DQK, (h + 1) * DQK)
        q = qk[:, sl] * (DQK ** -0.5)
        k = qk[:, QK_WIDTH + h * DQK:QK_WIDTH + (h + 1) * DQK]
        v_bf = v_all[:, h * DV:(h + 1) * DV].astype(BF16)
        bc = e[0:rows, sl]
        brem = e[rows:2 * rows, sl]

        a = jnp.zeros((rows, rows), F32)
        for lvl, hb in enumerate(levels):
            e_l = e[(2 + lvl) * rows:(3 + lvl) * rows, sl]
            u = (jnp.where((row_id % (2 * hb)) >= hb, q, k) * jnp.exp(e_l)).astype(BF16)
            z = _dot_nt(u, u)
            mask = ((r // (2 * hb)) == (c // (2 * hb))) & ((r % (2 * hb)) >= hb) & ((c % (2 * hb)) < hb)
            a = a + jnp.where(mask, z, 0.0)

        prods = []
        for s_loc in range(SUBLANES):
            kb = jnp.concatenate(
                [jnp.broadcast_to(k[t * SUBLANES + s_loc:t * SUBLANES + s_loc + 1, :], (SUBLANES, DQK))
                 for t in range(rows // SUBLANES)], axis=0)
            bcb = jnp.concatenate(
                [jnp.broadcast_to(bc[t * SUBLANES + s_loc:t * SUBLANES + s_loc + 1, :], (SUBLANES, DQK))
                 for t in range(rows // SUBLANES)], axis=0)
            expo = jnp.where(row_in_tile >= s_loc, bc - bcb, -jnp.inf)
            prods.append((q * kb * jnp.exp(expo)).astype(BF16))
        tile_sums = jnp.dot(jnp.concatenate(prods, axis=1), bd, preferred_element_type=F32)
        a = a + jnp.where((r // SUBLANES) == (c // SUBLANES), tile_sums, 0.0)

        o = jnp.dot(a.astype(BF16), v_bf, preferred_element_type=F32)
        qq = (q * jnp.exp(bc)).astype(BF16)
        kk = k * jnp.exp(brem)
        inter_parts = []
        for s in range(nseg):
            st = s_in[s, h]
            oi = jnp.dot(qq, st.astype(BF16), preferred_element_type=F32)
            inter_parts.append(oi[s * seg_len:(s + 1) * seg_len] if nseg > 1 else oi)
            kk_s = kk if nseg == 1 else jnp.where(row_seg == s, kk, 0.0)
            s_out[s, h] = st * jnp.exp(total_col[sl, s:s + 1]) + _dot_tn(kk_s, v_bf)
        o = o + (inter_parts[0] if nseg == 1 else jnp.concatenate(inter_parts, axis=0))

        hn = _rms(o, ghead[:, h * DV:(h + 1) * DV])
        gate = r_all[:, h * DV:(h + 1) * DV]
        y_ref[:, h * DV:(h + 1) * DV] = (gate * _sigmoid(gate) * hn).astype(y_ref.dtype)


def _gla_prompt_body(qk_ref, v_ref, r_ref, glr_ref, w2_ref, bg_ref, emat_ref, bd_ref, segind_ref, ghead_ref,
                     y_ref, s_ref):
    @pl.when(pl.program_id(1) == 0)
    def _():
        s_ref[...] = jnp.zeros_like(s_ref)

    _gla_rows(qk_ref, v_ref, r_ref, glr_ref, w2_ref, bg_ref, emat_ref, bd_ref, segind_ref, ghead_ref,
              s_ref, s_ref, y_ref, PROMPT_ROWS, PROMPT_ROWS)


def _gla_sample_body(qk_ref, v_ref, r_ref, glr_ref, w2_ref, bg_ref, emat_ref, bd_ref, segind_ref, ghead_ref,
                     s_in, y_any, y_ref, s_out, *, seg_len):
    del y_any
    _gla_rows(qk_ref, v_ref, r_ref, glr_ref, w2_ref, bg_ref, emat_ref, bd_ref, segind_ref, ghead_ref,
              s_in, s_out, y_ref, SAMPLE_ROWS, seg_len)


def _mixer_row_specs(rows, row_block):
    return [
        pl.BlockSpec((rows, 2 * QK_WIDTH), lambda *ids: (row_block(*ids), 0)),
        pl.BlockSpec((rows, V_WIDTH), lambda *ids: (row_block(*ids), 1)),
        pl.BlockSpec((rows, V_WIDTH), lambda *ids: (row_block(*ids), 2)),
        pl.BlockSpec((rows, LANES), lambda *ids: (row_block(*ids), 0)),
    ]


def _state_spec(n, trailing):
    nd = 1 + len(trailing)
    return pl.BlockSpec((n,) + trailing, lambda *ids: (ids[0],) + (0,) * (nd - 1))


def _mlstm_mixer(proj, gcol, bias_i, bias_f, ghead, c0, n0, m0, n_prompt_seq, prompt_len):
    m_rows = proj.shape[0]
    n_prompt = n_prompt_seq * prompt_len
    n_sample_seq = c0.shape[0]
    seg_len = (m_rows - n_prompt) // n_sample_seq
    gates = gcol[:, :2 * N_HEADS]
    tiles = lambda a, rows: jnp.transpose(a.reshape(-1, rows, 2 * N_HEADS), (0, 2, 1))
    bias = jnp.concatenate([bias_i, bias_f]).astype(F32)
    bias_row = jnp.zeros((1, LANES), F32).at[0, :2 * N_HEADS].set(bias)
    bias_col = bias.reshape(2 * N_HEADS, 1)
    ghead = ghead.reshape(1, V_WIDTH)

    def const_inputs(rows, seg):
        ccol, crow = _mlstm_constants(rows, seg)
        arrays = [bias_row, bias_col, ccol, crow, ghead]
        return arrays, [_const_spec(a.shape) for a in arrays]

    rows = PROMPT_ROWS
    chunks = prompt_len // rows
    blk = lambda b, ci: b * chunks + ci
    arrays, specs = const_inputs(rows, rows)
    y, c_p, n_p, m_p = pl.pallas_call(
        _mlstm_prompt_body,
        grid=(n_prompt_seq, chunks),
        in_specs=_mixer_row_specs(rows, blk)
        + [pl.BlockSpec((1, 2 * N_HEADS, rows), lambda b, ci: (blk(b, ci), 0, 0))] + specs,
        out_specs=[
            pl.BlockSpec((rows, V_WIDTH), lambda b, ci: (blk(b, ci), 0)),
            _state_spec(1, (N_HEADS, DQK, DV)), _state_spec(1, (N_HEADS, DQK)), _state_spec(1, (N_HEADS, LANES)),
        ],
        out_shape=[
            jax.ShapeDtypeStruct((m_rows, V_WIDTH), BF16),
            jax.ShapeDtypeStruct((n_prompt_seq, N_HEADS, DQK, DV), F32),
            jax.ShapeDtypeStruct((n_prompt_seq, N_HEADS, DQK), F32),
            jax.ShapeDtypeStruct((n_prompt_seq, N_HEADS, LANES), F32),
        ],
        compiler_params=pltpu.CompilerParams(dimension_semantics=("arbitrary",) * 2, vmem_limit_bytes=VMEM_LIMIT),
        name="mlstm_prompt",
    )(proj, proj, proj, gcol, tiles(gates[:n_prompt], rows), *arrays)

    rows = SAMPLE_ROWS
    spb = rows // seg_len
    first = n_prompt // rows
    sblk = lambda i: first + i
    m0b = jnp.broadcast_to(m0[:, :, None], m0.shape + (LANES,))
    arrays, specs = const_inputs(rows, seg_len)
    n_in = 5 + len(arrays) + 3
    y, c_s, n_s, m_s = pl.pallas_call(
        functools.partial(_mlstm_sample_body, seg_len=seg_len),
        grid=(n_sample_seq // spb,),
        in_specs=_mixer_row_specs(rows, sblk) + [pl.BlockSpec((1, 2 * N_HEADS, rows), lambda i: (i, 0, 0))] + specs
        + [_state_spec(spb, (N_HEADS, DQK, DV)), _state_spec(spb, (N_HEADS, DQK)), _state_spec(spb, (N_HEADS, LANES)),
           pl.BlockSpec(memory_space=pl.ANY)],
        out_specs=[
            pl.BlockSpec((rows, V_WIDTH), lambda i: (sblk(i), 0)),
            _state_spec(spb, (N_HEADS, DQK, DV)), _state_spec(spb, (N_HEADS, DQK)), _state_spec(spb, (N_HEADS, LANES)),
        ],
        out_shape=[
            jax.ShapeDtypeStruct((m_rows, V_WIDTH), BF16),
            jax.ShapeDtypeStruct(c0.shape, F32),
            jax.ShapeDtypeStruct(n0.shape, F32),
            jax.ShapeDtypeStruct(m0b.shape, F32),
        ],
        input_output_aliases={n_in: 0},
        compiler_params=pltpu.CompilerParams(dimension_semantics=("arbitrary",), vmem_limit_bytes=VMEM_LIMIT),
        name="mlstm_sample",
    )(proj, proj, proj, gcol, tiles(gates[n_prompt:], rows), *arrays, c0, n0, m0b, y)

    return y, (c_p, n_p, m_p[:, :, 0]), (c_s, n_s, m_s[:, :, 0])


def _gla_mixer(proj, glr, w_gate2, b_gate, ghead, s0, n_prompt_seq, prompt_len):
    m_rows = proj.shape[0]
    n_prompt = n_prompt_seq * prompt_len
    n_sample_seq = s0.shape[0]
    seg_len = (m_rows - n_prompt) // n_sample_seq
    w2 = jnp.zeros((LANES, QK_WIDTH), BF16).at[:GATE_RANK].set(w_gate2.astype(BF16))
    bg = b_gate.reshape(1, QK_WIDTH).astype(F32)
    ghead = ghead.reshape(1, V_WIDTH)

    def const_inputs(rows, seg):
        emat, bd, segind = _gla_constants(rows, seg)
        arrays = [w2, bg, emat, bd, segind, ghead]
        return arrays, [_const_spec(a.shape) for a in arrays]

    rows = PROMPT_ROWS
    chunks = prompt_len // rows
    blk = lambda b, ci: b * chunks + ci
    arrays, specs = const_inputs(rows, rows)
    y, s_p = pl.pallas_call(
        _gla_prompt_body,
        grid=(n_prompt_seq, chunks),
        in_specs=_mixer_row_specs(rows, blk) + specs,
        out_specs=[
            pl.BlockSpec((rows, V_WIDTH), lambda b, ci: (blk(b, ci), 0)),
            _state_spec(1, (N_HEADS, DQK, DV)),
        ],
        out_shape=[
            jax.ShapeDtypeStruct((m_rows, V_WIDTH), BF16),
            jax.ShapeDtypeStruct((n_prompt_seq, N_HEADS, DQK, DV), F32),
        ],
        compiler_params=pltpu.CompilerParams(dimension_semantics=("arbitrary",) * 2, vmem_limit_bytes=VMEM_LIMIT),
        name="gla_prompt",
    )(proj, proj, proj, glr, *arrays)

    rows = SAMPLE_ROWS
    spb = rows // seg_len
    first = n_prompt // rows
    sblk = lambda i: first + i
    arrays, specs = const_inputs(rows, seg_len)
    n_in = 4 + len(arrays) + 1
    y, s_s = pl.pallas_call(
        functools.partial(_gla_sample_body, seg_len=seg_len),
        grid=(n_sample_seq // spb,),
        in_specs=_mixer_row_specs(rows, sblk) + specs
        + [_state_spec(spb, (N_HEADS, DQK, DV)), pl.BlockSpec(memory_space=pl.ANY)],
        out_specs=[
            pl.BlockSpec((rows, V_WIDTH), lambda i: (sblk(i), 0)),
            _state_spec(spb, (N_HEADS, DQK, DV)),
        ],
        out_shape=[
            jax.ShapeDtypeStruct((m_rows, V_WIDTH), BF16),
            jax.ShapeDtypeStruct(s0.shape, F32),
        ],
        input_output_aliases={n_in: 0},
        compiler_params=pltpu.CompilerParams(dimension_semantics=("arbitrary",), vmem_limit_bytes=VMEM_LIMIT),
        name="gla_sample",
    )(proj, proj, proj, glr, *arrays, s0, y)

    return y, s_p, s_s


def _pad_gate_columns(w):
    return jnp.zeros((D_MODEL, LANES), BF16).at[:, :w.shape[1]].set(w.astype(BF16))


def kernel(x_prompt, x_sample, state_mlstm_C, state_mlstm_n, state_mlstm_m, state_gla_S, g_pre_mix, g_post_mix, g_pre_ffn, g_post_ffn, w_in_mlstm, b_i_mlstm, b_f_mlstm, g_head_mlstm, w_out_mlstm, w_in_gla, w_gate2_gla, b_gate_gla, g_head_gla, w_out_gla, w_ffn_gate, w_ffn_up, w_ffn_down):
    bp, sp, d = x_prompt.shape
    bs, ss, _ = x_sample.shape
    depth = g_pre_mix.shape[0]
    assert d == D_MODEL and sp % PROMPT_ROWS == 0 and SAMPLE_ROWS % ss == 0 and (bs * ss) % SAMPLE_ROWS == 0
    assert (bp * sp) % TM == 0 and (bs * ss) % TM == 0

    xp = x_prompt.reshape(bp * sp, d)
    xs = x_sample.reshape(bs * ss, d)
    vec = lambda g: g.reshape(1, D_MODEL).astype(F32)

    prompt_states = {"C": [], "n": [], "m": [], "S": []}
    sample_states = {"C": [], "n": [], "m": [], "S": []}
    for layer in range(depth):
        j = layer // 2
        if layer % 2 == 0:
            w_in = w_in_mlstm[j]
            proj, gcol = _inproj(xp, xs, vec(g_pre_mix[layer]), w_in[:, :MAIN_WIDTH].astype(BF16),
                                 _pad_gate_columns(w_in[:, MAIN_WIDTH:]))
            y, st_p, st_s = _mlstm_mixer(proj, gcol, b_i_mlstm[j], b_f_mlstm[j], g_head_mlstm[j],
                                         state_mlstm_C[j], state_mlstm_n[j], state_mlstm_m[j], bp, sp)
            for dst, st in ((prompt_states, st_p), (sample_states, st_s)):
                dst["C"].append(st[0]); dst["n"].append(st[1]); dst["m"].append(st[2])
            w_out = w_out_mlstm[j]
        else:
            w_in = w_in_gla[j]
            proj, glr = _inproj(xp, xs, vec(g_pre_mix[layer]), w_in[:, :MAIN_WIDTH].astype(BF16),
                                _pad_gate_columns(w_in[:, MAIN_WIDTH:]))
            y, s_p, s_s = _gla_mixer(proj, glr, w_gate2_gla[j], b_gate_gla[j], g_head_gla[j],
                                     state_gla_S[j], bp, sp)
            prompt_states["S"].append(s_p)
            sample_states["S"].append(s_s)
            w_out = w_out_gla[j]
        xp, xs = _block_tail(y, w_out.astype(BF16), xp, xs, vec(g_post_mix[layer]), vec(g_pre_ffn[layer]),
                             vec(g_post_ffn[layer]), w_ffn_gate[layer].astype(BF16), w_ffn_up[layer].astype(BF16),
                             w_ffn_down[layer].astype(BF16))

    stack = lambda xs_: jnp.stack(xs_)
    return (xp.reshape(bp, sp, d), xs.reshape(bs, ss, d),
            stack(prompt_states["C"]), stack(prompt_states["n"]), stack(prompt_states["m"]), stack(prompt_states["S"]),
            stack(sample_states["C"]), stack(sample_states["n"]), stack(sample_states["m"]), stack(sample_states["S"]))
```

```python
import functools

import numpy as np
import jax
import jax.numpy as jnp
from jax import lax
from jax.experimental import pallas as pl
from jax.experimental.pallas import tpu as pltpu

F32 = jnp.float32
BF16 = jnp.bfloat16

D_MODEL = 1024
N_HEADS = 4
DQK = 128
DV = 256
QK_WIDTH = N_HEADS * DQK
V_WIDTH = N_HEADS * DV
MAIN_WIDTH = 2 * QK_WIDTH + 2 * V_WIDTH
GATE_RANK = 16
GATE_TAU = 16.0
EPS = 1e-6
LANES = 128
SUBLANES = 8
MXU_DIM = 256
VMEM_LIMIT = 56 * 1024 * 1024

PROMPT_ROWS = 128
SAMPLE_ROWS = 64
LEVELS = (64, 32, 16, 8)
TM = 512
FF_SPLITS = 2


def _dot(a, b):
    return jnp.dot(a.astype(BF16), b.astype(BF16), preferred_element_type=F32)


def _dot_nt(a, b):
    return lax.dot_general(a.astype(BF16), b.astype(BF16), (((1,), (1,)), ((), ())),
                           preferred_element_type=F32)


def _dot_tn(a, b):
    return lax.dot_general(a.astype(BF16), b.astype(BF16), (((0,), (0,)), ((), ())),
                           preferred_element_type=F32)


def _split_hi_lo(x):
    hi = x.astype(BF16)
    lo = (x - hi.astype(F32)).astype(BF16)
    return hi, lo


def _log_sigmoid(x):
    return jnp.minimum(x, 0.0) - jnp.log1p(jnp.exp(-jnp.abs(x)))


def _sigmoid(x):
    return 1.0 / (1.0 + jnp.exp(-x))


def _rms(x, gain):
    return x * lax.rsqrt(jnp.mean(x * x, axis=-1, keepdims=True) + EPS) * gain


def _rows_per_segment(values, seg_len):
    parts = [jnp.broadcast_to(v, (seg_len, v.shape[1])) for v in values]
    return parts[0] if len(parts) == 1 else jnp.concatenate(parts, axis=0)


def _const_spec(shape):
    nd = len(shape)
    return pl.BlockSpec(shape, lambda *_: (0,) * nd)


def _two_group_specs(n_prompt_blocks, width):
    prompt = pl.BlockSpec((TM, width), lambda i: (jnp.minimum(i, n_prompt_blocks - 1), 0))
    sample = pl.BlockSpec((TM, width), lambda i: (jnp.maximum(i - n_prompt_blocks, 0), 0))
    return prompt, sample


def _resident_spec(shape):
    nd = len(shape)
    return pl.BlockSpec(shape, lambda i: (0,) * nd, pipeline_mode=pl.Buffered(1))


def _inproj_body(xp_ref, xs_ref, gain_ref, w_ref, wg_ref, o_ref, og_ref, h_scr, *, n_prompt_blocks):
    i = pl.program_id(0)

    def normalise(x_ref):
        h_scr[...] = _rms(x_ref[...], gain_ref[...]).astype(BF16)

    pl.when(i < n_prompt_blocks)(lambda: normalise(xp_ref))
    pl.when(i >= n_prompt_blocks)(lambda: normalise(xs_ref))
    h = h_scr[...]
    og_ref[...] = jnp.dot(h, wg_ref[...], preferred_element_type=F32)
    o_ref[...] = jnp.dot(h, w_ref[...], preferred_element_type=F32)


def _inproj(xp, xs, gain, w_main, w_gate):
    npb = xp.shape[0] // TM
    m = xp.shape[0] + xs.shape[0]
    spec_p, spec_s = _two_group_specs(npb, D_MODEL)
    return pl.pallas_call(
        functools.partial(_inproj_body, n_prompt_blocks=npb),
        grid=(m // TM,),
        in_specs=[spec_p, spec_s, _resident_spec((1, D_MODEL)), _resident_spec((D_MODEL, MAIN_WIDTH)),
                  _resident_spec((D_MODEL, LANES))],
        out_specs=[
            pl.BlockSpec((TM, MAIN_WIDTH), lambda i: (i, 0)),
            pl.BlockSpec((TM, LANES), lambda i: (i, 0)),
        ],
        out_shape=[
            jax.ShapeDtypeStruct((m, MAIN_WIDTH), F32),
            jax.ShapeDtypeStruct((m, LANES), F32),
        ],
        scratch_shapes=[pltpu.VMEM((TM, D_MODEL), BF16)],
        compiler_params=pltpu.CompilerParams(dimension_semantics=("arbitrary",), vmem_limit_bytes=VMEM_LIMIT),
        name="inproj",
    )(xp, xs, gain, w_main, w_gate)


def _tail_body(y_ref, wo_ref, xp_ref, xs_ref, gpm_ref, gpf_ref, gqf_ref, wg_ref, wu_ref, wd_ref,
               op_ref, os_ref, x1_scr, *, n_prompt_blocks):
    i = pl.program_id(0)
    mixed = _rms(jnp.dot(y_ref[...], wo_ref[...], preferred_element_type=F32), gpm_ref[...])

    def residual(x_ref):
        x1_scr[...] = x_ref[...] + mixed

    pl.when(i < n_prompt_blocks)(lambda: residual(xp_ref))
    pl.when(i >= n_prompt_blocks)(lambda: residual(xs_ref))

    x1 = x1_scr[...]
    h = _rms(x1, gpf_ref[...]).astype(BF16)
    d_ff = wg_ref.shape[1]
    n_tiles = d_ff // MXU_DIM
    bounds = [MXU_DIM * ((n_tiles * t + FF_SPLITS - 1) // FF_SPLITS) for t in range(FF_SPLITS + 1)]
    ffn = None
    for t in range(FF_SPLITS):
        cols = slice(bounds[t], bounds[t + 1])
        gate = jnp.dot(h, wg_ref[:, cols], preferred_element_type=F32)
        up = jnp.dot(h, wu_ref[:, cols], preferred_element_type=F32)
        act = (gate * _sigmoid(gate) * up).astype(BF16)
        part = jnp.dot(act, wd_ref[cols, :], preferred_element_type=F32)
        ffn = part if ffn is None else ffn + part
    out = x1 + _rms(ffn, gqf_ref[...])

    @pl.when(i < n_prompt_blocks)
    def _():
        op_ref[...] = out

    @pl.when(i >= n_prompt_blocks)
    def _():
        os_ref[...] = out


def _block_tail(y, w_out, xp, xs, g_post_mix, g_pre_ffn, g_post_ffn, w_gate, w_up, w_down):
    npb = xp.shape[0] // TM
    m = xp.shape[0] + xs.shape[0]
    d_ff = w_gate.shape[1]
    assert d_ff % MXU_DIM == 0
    spec_p, spec_s = _two_group_specs(npb, D_MODEL)
    vec = _resident_spec((1, D_MODEL))
    return pl.pallas_call(
        functools.partial(_tail_body, n_prompt_blocks=npb),
        grid=(m // TM,),
        in_specs=[
            pl.BlockSpec((TM, V_WIDTH), lambda i: (i, 0)),
            _resident_spec((V_WIDTH, D_MODEL)),
            spec_p, spec_s,
            vec, vec, vec,
            _resident_spec((D_MODEL, d_ff)), _resident_spec((D_MODEL, d_ff)), _resident_spec((d_ff, D_MODEL)),
        ],
        out_specs=[spec_p, spec_s],
        out_shape=[jax.ShapeDtypeStruct(xp.shape, F32), jax.ShapeDtypeStruct(xs.shape, F32)],
        scratch_shapes=[pltpu.VMEM((TM, D_MODEL), F32)],
        compiler_params=pltpu.CompilerParams(dimension_semantics=("arbitrary",), vmem_limit_bytes=VMEM_LIMIT),
        name="block_tail",
    )(y, w_out, xp, xs, g_post_mix, g_pre_ffn, g_post_ffn, w_gate, w_up, w_down)


def _segment_structure(rows, seg_len):
    t = np.arange(rows)
    seg = t // seg_len
    same = seg[:, None] == seg[None, :]
    cum = same & (t[None, :] <= t[:, None])
    return t, seg, same, cum


def _mlstm_constants(rows, seg_len):
    _, _, same, cum = _segment_structure(rows, seg_len)
    col = np.concatenate([cum, same], axis=0).astype(np.float32)
    row = cum.T.astype(np.float32)
    return jnp.asarray(col, BF16), jnp.asarray(row, BF16)


def _mlstm_rows(qk_ref, v_ref, o_ref, gcol_ref, grow_ref, bias_row_ref, bias_col_ref, ccol_ref, crow_ref,
                ghead_ref, c_in, n_in, m_in, c_out, n_out, m_out, y_ref, rows, seg_len):
    nseg = rows // seg_len
    r = lax.broadcasted_iota(jnp.int32, (rows, rows), 0)
    c = lax.broadcasted_iota(jnp.int32, (rows, rows), 1)
    same = None if nseg == 1 else (r // seg_len) == (c // seg_len)
    lower = (c <= r) if nseg == 1 else same & (c <= r)
    row_seg = lax.broadcasted_iota(jnp.int32, (rows, DQK), 0) // seg_len

    pre_col = gcol_ref[...] + bias_row_ref[...]
    pre_row = grow_ref[0] + bias_col_ref[...]
    ccol = ccol_ref[...]
    crow = crow_ref[...]
    hi, lo = _split_hi_lo(_log_sigmoid(pre_col))
    sums_col = jnp.dot(ccol, hi, preferred_element_type=F32) + jnp.dot(ccol, lo, preferred_element_type=F32)
    hi, lo = _split_hi_lo(_log_sigmoid(pre_row))
    sums_row = jnp.dot(hi, crow, preferred_element_type=F32) + jnp.dot(lo, crow, preferred_element_type=F32)

    qk = qk_ref[...]
    v_all = v_ref[...]
    o_all = o_ref[...]
    ghead = ghead_ref[...]

    for h in range(N_HEADS):
        f = N_HEADS + h
        q = qk[:, h * DQK:(h + 1) * DQK]
        k = qk[:, QK_WIDTH + h * DQK:QK_WIDTH + (h + 1) * DQK] * (DQK ** -0.5)
        v_bf = v_all[:, h * DV:(h + 1) * DV].astype(BF16)
        i_col = pre_col[:, h:h + 1]
        b_col = sums_col[0:rows, f:f + 1]
        b_last = sums_col[rows:2 * rows, f:f + 1]
        src = pre_row[h:h + 1, :] - sums_row[f:f + 1, :]
        m_prev = _rows_per_segment([m_in[s, h:h + 1, 0:1] for s in range(nseg)], seg_len)
        n_prev = _rows_per_segment([n_in[s, h:h + 1, :] for s in range(nseg)], seg_len)

        dmat = jnp.where(lower, b_col + src, -jnp.inf)
        inter = b_col + m_prev
        m_t = jnp.maximum(inter, jnp.max(dmat, axis=1, keepdims=True))
        w_inter = jnp.exp(inter - m_t)
        a = _dot_nt(q, k) * jnp.exp(dmat - m_t)

        q_bf = q.astype(BF16)
        qc_parts = []
        for s in range(nseg):
            qc = jnp.dot(q_bf, c_in[s, h].astype(BF16), preferred_element_type=F32)
            qc_parts.append(qc[s * seg_len:(s + 1) * seg_len] if nseg > 1 else qc)
        qc = qc_parts[0] if nseg == 1 else jnp.concatenate(qc_parts, axis=0)

        num = jnp.dot(a.astype(BF16), v_bf, preferred_element_type=F32) + w_inter * qc
        den = jnp.sum(a, axis=1, keepdims=True) + w_inter * jnp.sum(q * n_prev, axis=1, keepdims=True)
        hval = num * (1.0 / jnp.maximum(jnp.abs(den), jnp.exp(-m_t)))

        dlast = b_last + src
        if nseg > 1:
            dlast = jnp.where(same, dlast, -jnp.inf)
        m_new = jnp.maximum(b_last + m_prev, jnp.max(dlast, axis=1, keepdims=True))
        w_s = jnp.exp(b_last - b_col + i_col - m_new)
        decay = jnp.exp(b_last + m_prev - m_new)
        kw = k * w_s
        for s in range(nseg):
            kw_s = kw if nseg == 1 else jnp.where(row_seg == s, kw, 0.0)
            dec = decay[s * seg_len:s * seg_len + 1, :]
            c_out[s, h] = dec * c_in[s, h] + _dot_tn(kw_s, v_bf)
            n_out[s, h:h + 1, :] = dec * n_in[s, h:h + 1, :] + jnp.sum(kw_s, axis=0, keepdims=True)
            m_out[s, h:h + 1, :] = jnp.broadcast_to(m_new[s * seg_len:s * seg_len + 1, :], (1, LANES))

        hn = _rms(hval, ghead[:, h * DV:(h + 1) * DV])
        y_ref[:, h * DV:(h + 1) * DV] = (_sigmoid(o_all[:, h * DV:(h + 1) * DV]) * hn).astype(y_ref.dtype)


def _mlstm_prompt_body(qk_ref, v_ref, o_ref, gcol_ref, grow_ref, brow_ref, bcol_ref, ccol_ref, crow_ref,
                       ghead_ref, y_ref, c_ref, n_ref, m_ref):
    @pl.when(pl.program_id(1) == 0)
    def _():
        c_ref[...] = jnp.zeros_like(c_ref)
        n_ref[...] = jnp.zeros_like(n_ref)
        m_ref[...] = jnp.zeros_like(m_ref)

    _mlstm_rows(qk_ref, v_ref, o_ref, gcol_ref, grow_ref, brow_ref, bcol_ref, ccol_ref, crow_ref, ghead_ref,
                c_ref, n_ref, m_ref, c_ref, n_ref, m_ref, y_ref, PROMPT_ROWS, PROMPT_ROWS)


def _mlstm_sample_body(qk_ref, v_ref, o_ref, gcol_ref, grow_ref, brow_ref, bcol_ref, ccol_ref, crow_ref,
                       ghead_ref, c_in, n_in, m_in, y_any, y_ref, c_out, n_out, m_out, *, seg_len):
    del y_any
    _mlstm_rows(qk_ref, v_ref, o_ref, gcol_ref, grow_ref, brow_ref, bcol_ref, ccol_ref, crow_ref, ghead_ref,
                c_in, n_in, m_in, c_out, n_out, m_out, y_ref, SAMPLE_ROWS, seg_len)


def _gla_constants(rows, seg_len):
    t, seg, same, cum = _segment_structure(rows, seg_len)
    mats = [cum, same & (t[None, :] > t[:, None])]
    for hb in LEVELS:
        if 2 * hb > seg_len:
            continue
        ref = (t // (2 * hb)) * (2 * hb) + hb - 1
        upper_half = (t % (2 * hb)) >= hb
        q_side = (t[None, :] > ref[:, None]) & (t[None, :] <= t[:, None])
        k_side = (t[None, :] > t[:, None]) & (t[None, :] <= ref[:, None])
        mats.append(np.where(upper_half[:, None], q_side, k_side))
    emat = np.concatenate(mats, axis=0).astype(np.float32)
    lane = np.arange(SUBLANES * DQK) // DQK
    bd = (lane[:, None] == (t[None, :] % SUBLANES)).astype(np.float32)
    segind = (seg[:, None] == np.arange(LANES)[None, :]).astype(np.float32)
    return jnp.asarray(emat, BF16), jnp.asarray(bd, BF16), jnp.asarray(segind, BF16)


def _gla_rows(qk_ref, v_ref, r_ref, glr_ref, w2_ref, bg_ref, emat_ref, bd_ref, segind_ref, ghead_ref,
              s_in, s_out, y_ref, rows, seg_len):
    nseg = rows // seg_len
    levels = [hb for hb in LEVELS if 2 * hb <= seg_len]
    r = lax.broadcasted_iota(jnp.int32, (rows, rows), 0)
    c = lax.broadcasted_iota(jnp.int32, (rows, rows), 1)
    row_id = lax.broadcasted_iota(jnp.int32, (rows, DQK), 0)
    row_seg = row_id // seg_len
    row_in_tile = row_id % SUBLANES

    pre = _dot(glr_ref[...], w2_ref[...]) + bg_ref[...]
    g = _log_sigmoid(pre) * (1.0 / GATE_TAU)
    g_hi, g_lo = _split_hi_lo(g)
    emat = emat_ref[...]
    e = jnp.dot(emat, g_hi, preferred_element_type=F32) + jnp.dot(emat, g_lo, preferred_element_type=F32)
    segind = segind_ref[...]
    total_col = (lax.dot_general(g_hi, segind, (((0,), (0,)), ((), ())), preferred_element_type=F32)
                 + lax.dot_general(g_lo, segind, (((0,), (0,)), ((), ())), preferred_element_type=F32))

    qk = qk_ref[...]
    v_all = v_ref[...]
    r_all = r_ref[...]
    ghead = ghead_ref[...]
    bd = bd_ref[...]

    for h in range(N_HEADS):
        sl = slice(h * DQK, (h + 1) * DQK)
        q = qk[:, sl] * (DQK ** -0.5)
        k = qk[:, QK_WIDTH + h * DQK:QK_WIDTH + (h + 1) * DQK]
        v_bf = v_all[:, h * DV:(h + 1) * DV].astype(BF16)
        bc = e[0:rows, sl]
        brem = e[rows:2 * rows, sl]

        a = jnp.zeros((rows, rows), F32)
        for lvl, hb in enumerate(levels):
            e_l = e[(2 + lvl) * rows:(3 + lvl) * rows, sl]
            u = (jnp.where((row_id % (2 * hb)) >= hb, q, k) * jnp.exp(e_l)).astype(BF16)
            z = _dot_nt(u, u)
            mask = ((r // (2 * hb)) == (c // (2 * hb))) & ((r % (2 * hb)) >= hb) & ((c % (2 * hb)) < hb)
            a = a + jnp.where(mask, z, 0.0)

        prods = []
        for s_loc in range(SUBLANES):
            kb = jnp.concatenate(
                [jnp.broadcast_to(k[t * SUBLANES + s_loc:t * SUBLANES + s_loc + 1, :], (SUBLANES, DQK))
                 for t in range(rows // SUBLANES)], axis=0)
            bcb = jnp.concatenate(
                [jnp.broadcast_to(bc[t * SUBLANES + s_loc:t * SUBLANES + s_loc + 1, :], (SUBLANES, DQK))
                 for t in range(rows // SUBLANES)], axis=0)
            expo = jnp.where(row_in_tile >= s_loc, bc - bcb, -jnp.inf)
            prods.append((q * kb * jnp.exp(expo)).astype(BF16))
        tile_sums = jnp.dot(jnp.concatenate(prods, axis=1), bd, preferred_element_type=F32)
        a = a + jnp.where((r // SUBLANES) == (c // SUBLANES), tile_sums, 0.0)

        o = jnp.dot(a.astype(BF16), v_bf, preferred_element_type=F32)
        qq = (q * jnp.exp(bc)).astype(BF16)
        kk = k * jnp.exp(brem)
        inter_parts = []
        for s in range(nseg):
            st = s_in[s, h]
            oi = jnp.dot(qq, st.astype(BF16), preferred_element_type=F32)
            inter_parts.append(oi[s * seg_len:(s + 1) * seg_len] if nseg > 1 else oi)
            kk_s = kk if nseg == 1 else jnp.where(row_seg == s, kk, 0.0)
            s_out[s, h] = st * jnp.exp(total_col[sl, s:s + 1]) + _dot_tn(kk_s, v_bf)
        o = o + (inter_parts[0] if nseg == 1 else jnp.concatenate(inter_parts, axis=0))

        hn = _rms(o, ghead[:, h * DV:(h + 1) * DV])
        gate = r_all[:, h * DV:(h + 1) * DV]
        y_ref[:, h * DV:(h + 1) * DV] = (gate * _sigmoid(gate) * hn).astype(y_ref.dtype)


def _gla_prompt_body(qk_ref, v_ref, r_ref, glr_ref, w2_ref, bg_ref, emat_ref, bd_ref, segind_ref, ghead_ref,
                     y_ref, s_ref):
    @pl.when(pl.program_id(1) == 0)
    def _():
        s_ref[...] = jnp.zeros_like(s_ref)

    _gla_rows(qk_ref, v_ref, r_ref, glr_ref, w2_ref, bg_ref, emat_ref, bd_ref, segind_ref, ghead_ref,
              s_ref, s_ref, y_ref, PROMPT_ROWS, PROMPT_ROWS)


def _gla_sample_body(qk_ref, v_ref, r_ref, glr_ref, w2_ref, bg_ref, emat_ref, bd_ref, segind_ref, ghead_ref,
                     s_in, y_any, y_ref, s_out, *, seg_len):
    del y_any
    _gla_rows(qk_ref, v_ref, r_ref, glr_ref, w2_ref, bg_ref, emat_ref, bd_ref, segind_ref, ghead_ref,
              s_in, s_out, y_ref, SAMPLE_ROWS, seg_len)


def _mixer_row_specs(rows, row_block):
    return [
        pl.BlockSpec((rows, 2 * QK_WIDTH), lambda *ids: (row_block(*ids), 0)),
        pl.BlockSpec((rows, V_WIDTH), lambda *ids: (row_block(*ids), 1)),
        pl.BlockSpec((rows, V_WIDTH), lambda *ids: (row_block(*ids), 2)),
        pl.BlockSpec((rows, LANES), lambda *ids: (row_block(*ids), 0)),
    ]


def _state_spec(n, trailing):
    nd = 1 + len(trailing)
    return pl.BlockSpec((n,) + trailing, lambda *ids: (ids[0],) + (0,) * (nd - 1))


def _mlstm_mixer(proj, gcol, bias_i, bias_f, ghead, c0, n0, m0, n_prompt_seq, prompt_len):
    m_rows = proj.shape[0]
    n_prompt = n_prompt_seq * prompt_len
    n_sample_seq = c0.shape[0]
    seg_len = (m_rows - n_prompt) // n_sample_seq
    gates = gcol[:, :2 * N_HEADS]
    tiles = lambda a, rows: jnp.transpose(a.reshape(-1, rows, 2 * N_HEADS), (0, 2, 1))
    bias = jnp.concatenate([bias_i, bias_f]).astype(F32)
    bias_row = jnp.zeros((1, LANES), F32).at[0, :2 * N_HEADS].set(bias)
    bias_col = bias.reshape(2 * N_HEADS, 1)
    ghead = ghead.reshape(1, V_WIDTH)

    def const_inputs(rows, seg):
        ccol, crow = _mlstm_constants(rows, seg)
        arrays = [bias_row, bias_col, ccol, crow, ghead]
        return arrays, [_const_spec(a.shape) for a in arrays]

    rows = PROMPT_ROWS
    chunks = prompt_len // rows
    blk = lambda b, ci: b * chunks + ci
    arrays, specs = const_inputs(rows, rows)
    y, c_p, n_p, m_p = pl.pallas_call(
        _mlstm_prompt_body,
        grid=(n_prompt_seq, chunks),
        in_specs=_mixer_row_specs(rows, blk)
        + [pl.BlockSpec((1, 2 * N_HEADS, rows), lambda b, ci: (blk(b, ci), 0, 0))] + specs,
        out_specs=[
            pl.BlockSpec((rows, V_WIDTH), lambda b, ci: (blk(b, ci), 0)),
            _state_spec(1, (N_HEADS, DQK, DV)), _state_spec(1, (N_HEADS, DQK)), _state_spec(1, (N_HEADS, LANES)),
        ],
        out_shape=[
            jax.ShapeDtypeStruct((m_rows, V_WIDTH), BF16),
            jax.ShapeDtypeStruct((n_prompt_seq, N_HEADS, DQK, DV), F32),
            jax.ShapeDtypeStruct((n_prompt_seq, N_HEADS, DQK), F32),
            jax.ShapeDtypeStruct((n_prompt_seq, N_HEADS, LANES), F32),
        ],
        compiler_params=pltpu.CompilerParams(dimension_semantics=("arbitrary",) * 2, vmem_limit_bytes=VMEM_LIMIT),
        name="mlstm_prompt",
    )(proj, proj, proj, gcol, tiles(gates[:n_prompt], rows), *arrays)

    rows = SAMPLE_ROWS
    spb = rows // seg_len
    first = n_prompt // rows
    sblk = lambda i: first + i
    m0b = jnp.broadcast_to(m0[:, :, None], m0.shape + (LANES,))
    arrays, specs = const_inputs(rows, seg_len)
    n_in = 5 + len(arrays) + 3
    y, c_s, n_s, m_s = pl.pallas_call(
        functools.partial(_mlstm_sample_body, seg_len=seg_len),
        grid=(n_sample_seq // spb,),
        in_specs=_mixer_row_specs(rows, sblk) + [pl.BlockSpec((1, 2 * N_HEADS, rows), lambda i: (i, 0, 0))] + specs
        + [_state_spec(spb, (N_HEADS, DQK, DV)), _state_spec(spb, (N_HEADS, DQK)), _state_spec(spb, (N_HEADS, LANES)),
           pl.BlockSpec(memory_space=pl.ANY)],
        out_specs=[
            pl.BlockSpec((rows, V_WIDTH), lambda i: (sblk(i), 0)),
            _state_spec(spb, (N_HEADS, DQK, DV)), _state_spec(spb, (N_HEADS, DQK)), _state_spec(spb, (N_HEADS, LANES)),
        ],
        out_shape=[
            jax.ShapeDtypeStruct((m_rows, V_WIDTH), BF16),
            jax.ShapeDtypeStruct(c0.shape, F32),
            jax.ShapeDtypeStruct(n0.shape, F32),
            jax.ShapeDtypeStruct(m0b.shape, F32),
        ],
        input_output_aliases={n_in: 0},
        compiler_params=pltpu.CompilerParams(dimension_semantics=("arbitrary",), vmem_limit_bytes=VMEM_LIMIT),
        name="mlstm_sample",
    )(proj, proj, proj, gcol, tiles(gates[n_prompt:], rows), *arrays, c0, n0, m0b, y)

    return y, (c_p, n_p, m_p[:, :, 0]), (c_s, n_s, m_s[:, :, 0])


def _gla_mixer(proj, glr, w_gate2, b_gate, ghead, s0, n_prompt_seq, prompt_len):
    m_rows = proj.shape[0]
    n_prompt = n_prompt_seq * prompt_len
    n_sample_seq = s0.shape[0]
    seg_len = (m_rows - n_prompt) // n_sample_seq
    w2 = jnp.zeros((LANES, QK_WIDTH), BF16).at[:GATE_RANK].set(w_gate2.astype(BF16))
    bg = b_gate.reshape(1, QK_WIDTH).astype(F32)
    ghead = ghead.reshape(1, V_WIDTH)

    def const_inputs(rows, seg):
        emat, bd, segind = _gla_constants(rows, seg)
        arrays = [w2, bg, emat, bd, segind, ghead]
        return arrays, [_const_spec(a.shape) for a in arrays]

    rows = PROMPT_ROWS
    chunks = prompt_len // rows
    blk = lambda b, ci: b * chunks + ci
    arrays, specs = const_inputs(rows, rows)
    y, s_p = pl.pallas_call(
        _gla_prompt_body,
        grid=(n_prompt_seq, chunks),
        in_specs=_mixer_row_specs(rows, blk) + specs,
        out_specs=[
            pl.BlockSpec((rows, V_WIDTH), lambda b, ci: (blk(b, ci), 0)),
            _state_spec(1, (N_HEADS, DQK, DV)),
        ],
        out_shape=[
            jax.ShapeDtypeStruct((m_rows, V_WIDTH), BF16),
            jax.ShapeDtypeStruct((n_prompt_seq, N_HEADS, DQK, DV), F32),
        ],
        compiler_params=pltpu.CompilerParams(dimension_semantics=("arbitrary",) * 2, vmem_limit_bytes=VMEM_LIMIT),
        name="gla_prompt",
    )(proj, proj, proj, glr, *arrays)

    rows = SAMPLE_ROWS
    spb = rows // seg_len
    first = n_prompt // rows
    sblk = lambda i: first + i
    arrays, specs = const_inputs(rows, seg_len)
    n_in = 4 + len(arrays) + 1
    y, s_s = pl.pallas_call(
        functools.partial(_gla_sample_body, seg_len=seg_len),
        grid=(n_sample_seq // spb,),
        in_specs=_mixer_row_specs(rows, sblk) + specs
        + [_state_spec(spb, (N_HEADS, DQK, DV)), pl.BlockSpec(memory_space=pl.ANY)],
        out_specs=[
            pl.BlockSpec((rows, V_WIDTH), lambda i: (sblk(i), 0)),
            _state_spec(spb, (N_HEADS, DQK, DV)),
        ],
        out_shape=[
            jax.ShapeDtypeStruct((m_rows, V_WIDTH), BF16),
            jax.ShapeDtypeStruct(s0.shape, F32),
        ],
        input_output_aliases={n_in: 0},
        compiler_params=pltpu.CompilerParams(dimension_semantics=("arbitrary",), vmem_limit_bytes=VMEM_LIMIT),
        name="gla_sample",
    )(proj, proj, proj, glr, *arrays, s0, y)

    return y, s_p, s_s


def _pad_gate_columns(w):
    return jnp.zeros((D_MODEL, LANES), BF16).at[:, :w.shape[1]].set(w.astype(BF16))


def kernel(x_prompt, x_sample, state_mlstm_C, state_mlstm_n, state_mlstm_m, state_gla_S, g_pre_mix, g_post_mix, g_pre_ffn, g_post_ffn, w_in_mlstm, b_i_mlstm, b_f_mlstm, g_head_mlstm, w_out_mlstm, w_in_gla, w_gate2_gla, b_gate_gla, g_head_gla, w_out_gla, w_ffn_gate, w_ffn_up, w_ffn_down):
    bp, sp, d = x_prompt.shape
    bs, ss, _ = x_sample.shape
    depth = g_pre_mix.shape[0]
    assert d == D_MODEL and sp % PROMPT_ROWS == 0 and SAMPLE_ROWS % ss == 0 and (bs * ss) % SAMPLE_ROWS == 0
    assert (bp * sp) % TM == 0 and (bs * ss) % TM == 0

    xp = x_prompt.reshape(bp * sp, d)
    xs = x_sample.reshape(bs * ss, d)
    vec = lambda g: g.reshape(1, D_MODEL).astype(F32)

    prompt_states = {"C": [], "n": [], "m": [], "S": []}
    sample_states = {"C": [], "n": [], "m": [], "S": []}
    for layer in range(depth):
        j = layer // 2
        if layer % 2 == 0:
            w_in = w_in_mlstm[j]
            proj, gcol = _inproj(xp, xs, vec(g_pre_mix[layer]), w_in[:, :MAIN_WIDTH].astype(BF16),
                                 _pad_gate_columns(w_in[:, MAIN_WIDTH:]))
            y, st_p, st_s = _mlstm_mixer(proj, gcol, b_i_mlstm[j], b_f_mlstm[j], g_head_mlstm[j],
                                         state_mlstm_C[j], state_mlstm_n[j], state_mlstm_m[j], bp, sp)
            for dst, st in ((prompt_states, st_p), (sample_states, st_s)):
                dst["C"].append(st[0]); dst["n"].append(st[1]); dst["m"].append(st[2])
            w_out = w_out_mlstm[j]
        else:
            w_in = w_in_gla[j]
            proj, glr = _inproj(xp, xs, vec(g_pre_mix[layer]), w_in[:, :MAIN_WIDTH].astype(BF16),
                                _pad_gate_columns(w_in[:, MAIN_WIDTH:]))
            y, s_p, s_s = _gla_mixer(proj, glr, w_gate2_gla[j], b_gate_gla[j], g_head_gla[j],
                                     state_gla_S[j], bp, sp)
            prompt_states["S"].append(s_p)
            sample_states["S"].append(s_s)
            w_out = w_out_gla[j]
        xp, xs = _block_tail(y, w_out.astype(BF16), xp, xs, vec(g_post_mix[layer]), vec(g_pre_ffn[layer]),
                             vec(g_post_ffn[layer]), w_ffn_gate[layer].astype(BF16), w_ffn_up[layer].astype(BF16),
                             w_ffn_down[layer].astype(BF16))

    stack = lambda xs_: jnp.stack(xs_)
    return (xp.reshape(bp, sp, d), xs.reshape(bs, ss, d),
            stack(prompt_states["C"]), stack(prompt_states["n"]), stack(prompt_states["m"]), stack(prompt_states["S"]),
            stack(sample_states["C"]), stack(sample_states["n"]), stack(sample_states["m"]), stack(sample_states["S"]))
```

```python
import functools

import numpy as np
import jax
import jax.numpy as jnp
from jax import lax
from jax.experimental import pallas as pl
from jax.experimental.pallas import tpu as pltpu

F32 = jnp.float32
BF16 = jnp.bfloat16

D_MODEL = 1024
N_HEADS = 4
DQK = 128
DV = 256
QK_WIDTH = N_HEADS * DQK
V_WIDTH = N_HEADS * DV
MAIN_WIDTH = 2 * QK_WIDTH + 2 * V_WIDTH
GATE_RANK = 16
GATE_TAU = 16.0
EPS = 1e-6
LANES = 128
SUBLANES = 8
MXU_DIM = 256
VMEM_LIMIT = 56 * 1024 * 1024

PROMPT_ROWS = 128
SAMPLE_ROWS = 64
LEVELS = (64, 32, 16, 8)
TM = 512
FF_SPLITS = 2


def _dot(a, b):
    return jnp.dot(a.astype(BF16), b.astype(BF16), preferred_element_type=F32)


def _dot_nt(a, b):
    return lax.dot_general(a.astype(BF16), b.astype(BF16), (((1,), (1,)), ((), ())),
                           preferred_element_type=F32)


def _dot_tn(a, b):
    return lax.dot_general(a.astype(BF16), b.astype(BF16), (((0,), (0,)), ((), ())),
                           preferred_element_type=F32)


def _split_hi_lo(x):
    hi = x.astype(BF16)
    lo = (x - hi.astype(F32)).astype(BF16)
    return hi, lo


def _log_sigmoid(x):
    return jnp.minimum(x, 0.0) - jnp.log1p(jnp.exp(-jnp.abs(x)))


def _sigmoid(x):
    return 1.0 / (1.0 + jnp.exp(-x))


def _rms(x, gain):
    return x * lax.rsqrt(jnp.mean(x * x, axis=-1, keepdims=True) + EPS) * gain


def _rows_per_segment(values, seg_len):
    parts = [jnp.broadcast_to(v, (seg_len, v.shape[1])) for v in values]
    return parts[0] if len(parts) == 1 else jnp.concatenate(parts, axis=0)


def _const_spec(shape):
    nd = len(shape)
    return pl.BlockSpec(shape, lambda *_: (0,) * nd)


def _two_group_specs(n_prompt_blocks, width):
    prompt = pl.BlockSpec((TM, width), lambda i: (jnp.minimum(i, n_prompt_blocks - 1), 0))
    sample = pl.BlockSpec((TM, width), lambda i: (jnp.maximum(i - n_prompt_blocks, 0), 0))
    return prompt, sample


def _resident_spec(shape):
    nd = len(shape)
    return pl.BlockSpec(shape, lambda i: (0,) * nd, pipeline_mode=pl.Buffered(1))


def _inproj_body(xp_ref, xs_ref, gain_ref, w_ref, wg_ref, o_ref, og_ref, h_scr, *, n_prompt_blocks):
    i = pl.program_id(0)

    def normalise(x_ref):
        h_scr[...] = _rms(x_ref[...], gain_ref[...]).astype(BF16)

    pl.when(i < n_prompt_blocks)(lambda: normalise(xp_ref))
    pl.when(i >= n_prompt_blocks)(lambda: normalise(xs_ref))
    h = h_scr[...]
    og_ref[...] = jnp.dot(h, wg_ref[...], preferred_element_type=F32)
    o_ref[...] = jnp.dot(h, w_ref[...], preferred_element_type=F32)


def _inproj(xp, xs, gain, w_main, w_gate):
    npb = xp.shape[0] // TM
    m = xp.shape[0] + xs.shape[0]
    spec_p, spec_s = _two_group_specs(npb, D_MODEL)
    return pl.pallas_call(
        functools.partial(_inproj_body, n_prompt_blocks=npb),
        grid=(m // TM,),
        in_specs=[spec_p, spec_s, _resident_spec((1, D_MODEL)), _resident_spec((D_MODEL, MAIN_WIDTH)),
                  _resident_spec((D_MODEL, LANES))],
        out_specs=[
            pl.BlockSpec((TM, MAIN_WIDTH), lambda i: (i, 0)),
            pl.BlockSpec((TM, LANES), lambda i: (i, 0)),
        ],
        out_shape=[
            jax.ShapeDtypeStruct((m, MAIN_WIDTH), F32),
            jax.ShapeDtypeStruct((m, LANES), F32),
        ],
        scratch_shapes=[pltpu.VMEM((TM, D_MODEL), BF16)],
        compiler_params=pltpu.CompilerParams(dimension_semantics=("arbitrary",), vmem_limit_bytes=VMEM_LIMIT),
        name="inproj",
    )(xp, xs, gain, w_main, w_gate)


def _tail_body(yp_ref, ys_ref, wo_ref, xp_ref, xs_ref, gpm_ref, gpf_ref, gqf_ref, wg_ref, wu_ref, wd_ref,
               op_ref, os_ref, x1_scr, *, n_prompt_blocks):
    i = pl.program_id(0)

    def residual(y_ref, x_ref):
        mix = jnp.dot(y_ref[...], wo_ref[...], preferred_element_type=F32)
        x1_scr[...] = x_ref[...] + _rms(mix, gpm_ref[...])

    pl.when(i < n_prompt_blocks)(lambda: residual(yp_ref, xp_ref))
    pl.when(i >= n_prompt_blocks)(lambda: residual(ys_ref, xs_ref))

    x1 = x1_scr[...]
    h = _rms(x1, gpf_ref[...]).astype(BF16)
    d_ff = wg_ref.shape[1]
    n_tiles = d_ff // MXU_DIM
    bounds = [MXU_DIM * ((n_tiles * t + FF_SPLITS - 1) // FF_SPLITS) for t in range(FF_SPLITS + 1)]
    ffn = None
    for t in range(FF_SPLITS):
        cols = slice(bounds[t], bounds[t + 1])
        gate = jnp.dot(h, wg_ref[:, cols], preferred_element_type=F32)
        up = jnp.dot(h, wu_ref[:, cols], preferred_element_type=F32)
        act = (gate * _sigmoid(gate) * up).astype(BF16)
        part = jnp.dot(act, wd_ref[cols, :], preferred_element_type=F32)
        ffn = part if ffn is None else ffn + part
    out = x1 + _rms(ffn, gqf_ref[...])

    @pl.when(i < n_prompt_blocks)
    def _():
        op_ref[...] = out

    @pl.when(i >= n_prompt_blocks)
    def _():
        os_ref[...] = out


def _block_tail(yp, ys, w_out, xp, xs, g_post_mix, g_pre_ffn, g_post_ffn, w_gate, w_up, w_down):
    npb = xp.shape[0] // TM
    m = xp.shape[0] + xs.shape[0]
    d_ff = w_gate.shape[1]
    assert d_ff % MXU_DIM == 0
    spec_p, spec_s = _two_group_specs(npb, D_MODEL)
    vec = _resident_spec((1, D_MODEL))
    return pl.pallas_call(
        functools.partial(_tail_body, n_prompt_blocks=npb),
        grid=(m // TM,),
        in_specs=[
            spec_p, spec_s,
            _resident_spec((V_WIDTH, D_MODEL)),
            spec_p, spec_s,
            vec, vec, vec,
            _resident_spec((D_MODEL, d_ff)), _resident_spec((D_MODEL, d_ff)), _resident_spec((d_ff, D_MODEL)),
        ],
        out_specs=[spec_p, spec_s],
        out_shape=[jax.ShapeDtypeStruct(xp.shape, F32), jax.ShapeDtypeStruct(xs.shape, F32)],
        scratch_shapes=[pltpu.VMEM((TM, D_MODEL), F32)],
        compiler_params=pltpu.CompilerParams(dimension_semantics=("arbitrary",), vmem_limit_bytes=VMEM_LIMIT),
        name="block_tail",
    )(yp, ys, w_out, xp, xs, g_post_mix, g_pre_ffn, g_post_ffn, w_gate, w_up, w_down)


def _segment_structure(rows, seg_len):
    t = np.arange(rows)
    seg = t // seg_len
    same = seg[:, None] == seg[None, :]
    cum = same & (t[None, :] <= t[:, None])
    return t, seg, same, cum


def _mlstm_constants(rows, seg_len):
    _, _, same, cum = _segment_structure(rows, seg_len)
    col = np.concatenate([cum, same], axis=0).astype(np.float32)
    row = cum.T.astype(np.float32)
    return jnp.asarray(col, BF16), jnp.asarray(row, BF16)


def _mlstm_rows(qk_ref, v_ref, o_ref, gcol_ref, grow_ref, bias_row_ref, bias_col_ref, ccol_ref, crow_ref,
                ghead_ref, c_in, n_in, m_in, c_out, n_out, m_out, y_ref, rows, seg_len):
    nseg = rows // seg_len
    r = lax.broadcasted_iota(jnp.int32, (rows, rows), 0)
    c = lax.broadcasted_iota(jnp.int32, (rows, rows), 1)
    same = None if nseg == 1 else (r // seg_len) == (c // seg_len)
    lower = (c <= r) if nseg == 1 else same & (c <= r)
    row_seg = lax.broadcasted_iota(jnp.int32, (rows, DQK), 0) // seg_len

    pre_col = gcol_ref[...] + bias_row_ref[...]
    pre_row = grow_ref[0] + bias_col_ref[...]
    ccol = ccol_ref[...]
    crow = crow_ref[...]
    hi, lo = _split_hi_lo(_log_sigmoid(pre_col))
    sums_col = jnp.dot(ccol, hi, preferred_element_type=F32) + jnp.dot(ccol, lo, preferred_element_type=F32)
    hi, lo = _split_hi_lo(_log_sigmoid(pre_row))
    sums_row = jnp.dot(hi, crow, preferred_element_type=F32) + jnp.dot(lo, crow, preferred_element_type=F32)

    qk = qk_ref[...]
    v_all = v_ref[...]
    o_all = o_ref[...]
    ghead = ghead_ref[...]
    c_prev = [[c_in[s, h] for h in range(N_HEADS)] for s in range(nseg)]
    n_state = [[n_in[s, h:h + 1, :] for h in range(N_HEADS)] for s in range(nseg)]
    m_state = [[m_in[s, h:h + 1, 0:1] for h in range(N_HEADS)] for s in range(nseg)]
    writes = []

    heads = range(N_HEADS)
    q = [qk[:, h * DQK:(h + 1) * DQK] for h in heads]
    k = [qk[:, QK_WIDTH + h * DQK:QK_WIDTH + (h + 1) * DQK] * (DQK ** -0.5) for h in heads]
    v_bf = [v_all[:, h * DV:(h + 1) * DV].astype(BF16) for h in heads]
    i_col = [pre_col[:, h:h + 1] for h in heads]
    b_col = [sums_col[0:rows, N_HEADS + h:N_HEADS + h + 1] for h in heads]
    b_last = [sums_col[rows:2 * rows, N_HEADS + h:N_HEADS + h + 1] for h in heads]
    src = [pre_row[h:h + 1, :] - sums_row[N_HEADS + h:N_HEADS + h + 1, :] for h in heads]
    m_prev = [_rows_per_segment([m_state[s][h] for s in range(nseg)], seg_len) for h in heads]
    n_prev = [_rows_per_segment([n_state[s][h] for s in range(nseg)], seg_len) for h in heads]

    dmat = [jnp.where(lower, b_col[h] + src[h], -jnp.inf) for h in heads]
    inter = [b_col[h] + m_prev[h] for h in heads]
    m_t = [jnp.maximum(inter[h], jnp.max(dmat[h], axis=1, keepdims=True)) for h in heads]
    dlast = [b_last[h] + src[h] for h in heads]
    if nseg > 1:
        dlast = [jnp.where(same, d, -jnp.inf) for d in dlast]
    m_new = [jnp.maximum(b_last[h] + m_prev[h], jnp.max(dlast[h], axis=1, keepdims=True)) for h in heads]
    scores = [_dot_nt(q[h], k[h]) for h in heads]
    q_bf = [q[h].astype(BF16) for h in heads]
    qc = []
    for h in heads:
        parts = []
        for s in range(nseg):
            full = jnp.dot(q_bf[h], c_prev[s][h].astype(BF16), preferred_element_type=F32)
            parts.append(full[s * seg_len:(s + 1) * seg_len] if nseg > 1 else full)
        qc.append(parts[0] if nseg == 1 else jnp.concatenate(parts, axis=0))
    qn = [jnp.sum(q[h] * n_prev[h], axis=1, keepdims=True) for h in heads]

    w_inter = [jnp.exp(inter[h] - m_t[h]) for h in heads]
    a = [scores[h] * jnp.exp(dmat[h] - m_t[h]) for h in heads]
    w_s = [jnp.exp(b_last[h] - b_col[h] + i_col[h] - m_new[h]) for h in heads]
    decay = [jnp.exp(b_last[h] + m_prev[h] - m_new[h]) for h in heads]
    kw = [k[h] * w_s[h] for h in heads]

    num = [jnp.dot(a[h].astype(BF16), v_bf[h], preferred_element_type=F32) + w_inter[h] * qc[h] for h in heads]
    den = [jnp.sum(a[h], axis=1, keepdims=True) + w_inter[h] * qn[h] for h in heads]
    for h in heads:
        for s in range(nseg):
            kw_s = kw[h] if nseg == 1 else jnp.where(row_seg == s, kw[h], 0.0)
            dec = decay[h][s * seg_len:s * seg_len + 1, :]
            writes.append((s, h, dec * c_prev[s][h] + _dot_tn(kw_s, v_bf[h]),
                           dec * n_state[s][h] + jnp.sum(kw_s, axis=0, keepdims=True),
                           jnp.broadcast_to(m_new[h][s * seg_len:s * seg_len + 1, :], (1, LANES))))

    hval = [num[h] * (1.0 / jnp.maximum(jnp.abs(den[h]), jnp.exp(-m_t[h]))) for h in heads]
    hn = [_rms(hval[h], ghead[:, h * DV:(h + 1) * DV]) for h in heads]
    for h in heads:
        y_ref[:, h * DV:(h + 1) * DV] = (_sigmoid(o_all[:, h * DV:(h + 1) * DV]) * hn[h]).astype(y_ref.dtype)

    for s, h, c_new, n_new, m_new in writes:
        c_out[s, h] = c_new
        n_out[s, h:h + 1, :] = n_new
        m_out[s, h:h + 1, :] = m_new


def _mlstm_prompt_body(qk_ref, v_ref, o_ref, gcol_ref, grow_ref, brow_ref, bcol_ref, ccol_ref, crow_ref,
                       ghead_ref, y_ref, c_ref, n_ref, m_ref):
    @pl.when(pl.program_id(1) == 0)
    def _():
        c_ref[...] = jnp.zeros_like(c_ref)
        n_ref[...] = jnp.zeros_like(n_ref)
        m_ref[...] = jnp.zeros_like(m_ref)

    _mlstm_rows(qk_ref, v_ref, o_ref, gcol_ref, grow_ref, brow_ref, bcol_ref, ccol_ref, crow_ref, ghead_ref,
                c_ref, n_ref, m_ref, c_ref, n_ref, m_ref, y_ref, PROMPT_ROWS, PROMPT_ROWS)


def _mlstm_sample_body(qk_ref, v_ref, o_ref, gcol_ref, grow_ref, brow_ref, bcol_ref, ccol_ref, crow_ref,
                       ghead_ref, c_in, n_in, m_in, y_ref, c_out, n_out, m_out, *, seg_len):
    _mlstm_rows(qk_ref, v_ref, o_ref, gcol_ref, grow_ref, brow_ref, bcol_ref, ccol_ref, crow_ref, ghead_ref,
                c_in, n_in, m_in, c_out, n_out, m_out, y_ref, SAMPLE_ROWS, seg_len)


def _gla_constants(rows, seg_len):
    t, seg, same, cum = _segment_structure(rows, seg_len)
    mats = [cum, same & (t[None, :] > t[:, None])]
    for hb in LEVELS:
        if 2 * hb > seg_len:
            continue
        ref = (t // (2 * hb)) * (2 * hb) + hb - 1
        upper_half = (t % (2 * hb)) >= hb
        q_side = (t[None, :] > ref[:, None]) & (t[None, :] <= t[:, None])
        k_side = (t[None, :] > t[:, None]) & (t[None, :] <= ref[:, None])
        mats.append(np.where(upper_half[:, None], q_side, k_side))
    emat = np.concatenate(mats, axis=0).astype(np.float32)
    lane = np.arange(SUBLANES * DQK) // DQK
    bd = (lane[:, None] == (t[None, :] % SUBLANES)).astype(np.float32)
    segind = (seg[:, None] == np.arange(LANES)[None, :]).astype(np.float32)
    return jnp.asarray(emat, BF16), jnp.asarray(bd, BF16), jnp.asarray(segind, BF16)


def _gla_rows(qk_ref, v_ref, r_ref, glr_ref, w2_ref, bg_ref, emat_ref, bd_ref, segind_ref, ghead_ref,
              s_in, s_out, y_ref, rows, seg_len):
    nseg = rows // seg_len
    levels = [hb for hb in LEVELS if 2 * hb <= seg_len]
    r = lax.broadcasted_iota(jnp.int32, (rows, rows), 0)
    c = lax.broadcasted_iota(jnp.int32, (rows, rows), 1)
    row_id = lax.broadcasted_iota(jnp.int32, (rows, DQK), 0)
    row_seg = row_id // seg_len
    row_in_tile = row_id % SUBLANES

    pre = _dot(glr_ref[...], w2_ref[...]) + bg_ref[...]
    g = _log_sigmoid(pre) * (1.0 / GATE_TAU)
    g_hi, g_lo = _split_hi_lo(g)
    emat = emat_ref[...]
    e = jnp.dot(emat, g_hi, preferred_element_type=F32) + jnp.dot(emat, g_lo, preferred_element_type=F32)
    segind = segind_ref[...]
    total_col = (lax.dot_general(g_hi, segind, (((0,), (0,)), ((), ())), preferred_element_type=F32)
                 + lax.dot_general(g_lo, segind, (((0,), (0,)), ((), ())), preferred_element_type=F32))

    qk = qk_ref[...]
    v_all = v_ref[...]
    r_all = r_ref[...]
    ghead = ghead_ref[...]
    bd = bd_ref[...]
    s_prev = [[s_in[s, h] for h in range(N_HEADS)] for s in range(nseg)]
    writes = []

    heads = range(N_HEADS)
    sl = [slice(h * DQK, (h + 1) * DQK) for h in heads]
    q = [qk[:, sl[h]] * (DQK ** -0.5) for h in heads]
    k = [qk[:, QK_WIDTH + h * DQK:QK_WIDTH + (h + 1) * DQK] for h in heads]
    v_bf = [v_all[:, h * DV:(h + 1) * DV].astype(BF16) for h in heads]
    bc = [e[0:rows, sl[h]] for h in heads]
    brem = [e[rows:2 * rows, sl[h]] for h in heads]

    qq = [(q[h] * jnp.exp(bc[h])).astype(BF16) for h in heads]
    kk = [k[h] * jnp.exp(brem[h]) for h in heads]
    o_inter = []
    for h in heads:
        parts = []
        for s in range(nseg):
            st = s_prev[s][h]
            oi = jnp.dot(qq[h], st.astype(BF16), preferred_element_type=F32)
            parts.append(oi[s * seg_len:(s + 1) * seg_len] if nseg > 1 else oi)
            kk_s = kk[h] if nseg == 1 else jnp.where(row_seg == s, kk[h], 0.0)
            writes.append((s, h, st * jnp.exp(total_col[sl[h], s:s + 1]) + _dot_tn(kk_s, v_bf[h])))
        o_inter.append(parts[0] if nseg == 1 else jnp.concatenate(parts, axis=0))

    a = [jnp.zeros((rows, rows), F32) for h in heads]
    for lvl, hb in enumerate(levels):
        mask = ((r // (2 * hb)) == (c // (2 * hb))) & ((r % (2 * hb)) >= hb) & ((c % (2 * hb)) < hb)
        upper_half = (row_id % (2 * hb)) >= hb
        u = [(jnp.where(upper_half, q[h], k[h]) * jnp.exp(e[(2 + lvl) * rows:(3 + lvl) * rows, sl[h]])).astype(BF16)
             for h in heads]
        z = [_dot_nt(u[h], u[h]) for h in heads]
        a = [a[h] + jnp.where(mask, z[h], 0.0) for h in heads]

    prods = [[] for h in heads]
    for s_loc in range(SUBLANES):
        valid = row_in_tile >= s_loc
        for h in heads:
            kb = jnp.concatenate(
                [jnp.broadcast_to(k[h][t * SUBLANES + s_loc:t * SUBLANES + s_loc + 1, :], (SUBLANES, DQK))
                 for t in range(rows // SUBLANES)], axis=0)
            bcb = jnp.concatenate(
                [jnp.broadcast_to(bc[h][t * SUBLANES + s_loc:t * SUBLANES + s_loc + 1, :], (SUBLANES, DQK))
                 for t in range(rows // SUBLANES)], axis=0)
            expo = jnp.where(valid, bc[h] - bcb, -jnp.inf)
            prods[h].append((q[h] * kb * jnp.exp(expo)).astype(BF16))
    same_tile = (r // SUBLANES) == (c // SUBLANES)
    tile_sums = [jnp.dot(jnp.concatenate(prods[h], axis=1), bd, preferred_element_type=F32) for h in heads]
    a = [a[h] + jnp.where(same_tile, tile_sums[h], 0.0) for h in heads]

    o = [jnp.dot(a[h].astype(BF16), v_bf[h], preferred_element_type=F32) + o_inter[h] for h in heads]
    hn = [_rms(o[h], ghead[:, h * DV:(h + 1) * DV]) for h in heads]
    for h in heads:
        gate = r_all[:, h * DV:(h + 1) * DV]
        y_ref[:, h * DV:(h + 1) * DV] = (gate * _sigmoid(gate) * hn[h]).astype(y_ref.dtype)

    for s, h, s_new in writes:
        s_out[s, h] = s_new


def _gla_prompt_body(qk_ref, v_ref, r_ref, glr_ref, w2_ref, bg_ref, emat_ref, bd_ref, segind_ref, ghead_ref,
                     y_ref, s_ref):
    @pl.when(pl.program_id(1) == 0)
    def _():
        s_ref[...] = jnp.zeros_like(s_ref)

    _gla_rows(qk_ref, v_ref, r_ref, glr_ref, w2_ref, bg_ref, emat_ref, bd_ref, segind_ref, ghead_ref,
              s_ref, s_ref, y_ref, PROMPT_ROWS, PROMPT_ROWS)


def _gla_sample_body(qk_ref, v_ref, r_ref, glr_ref, w2_ref, bg_ref, emat_ref, bd_ref, segind_ref, ghead_ref,
                     s_in, y_ref, s_out, *, seg_len):
    _gla_rows(qk_ref, v_ref, r_ref, glr_ref, w2_ref, bg_ref, emat_ref, bd_ref, segind_ref, ghead_ref,
              s_in, s_out, y_ref, SAMPLE_ROWS, seg_len)


def _mixer_row_specs(rows, row_block):
    return [
        pl.BlockSpec((rows, 2 * QK_WIDTH), lambda *ids: (row_block(*ids), 0)),
        pl.BlockSpec((rows, V_WIDTH), lambda *ids: (row_block(*ids), 1)),
        pl.BlockSpec((rows, V_WIDTH), lambda *ids: (row_block(*ids), 2)),
        pl.BlockSpec((rows, LANES), lambda *ids: (row_block(*ids), 0)),
    ]


def _state_spec(n, trailing):
    nd = 1 + len(trailing)
    return pl.BlockSpec((n,) + trailing, lambda *ids: (ids[0],) + (0,) * (nd - 1))


def _mlstm_mixer(proj, gcol, bias_i, bias_f, ghead, c0, n0, m0, n_prompt_seq, prompt_len):
    m_rows = proj.shape[0]
    n_prompt = n_prompt_seq * prompt_len
    n_sample_seq = c0.shape[0]
    seg_len = (m_rows - n_prompt) // n_sample_seq
    gates = gcol[:, :2 * N_HEADS]
    tiles = lambda a, rows: jnp.transpose(a.reshape(-1, rows, 2 * N_HEADS), (0, 2, 1))
    bias = jnp.concatenate([bias_i, bias_f]).astype(F32)
    bias_row = jnp.zeros((1, LANES), F32).at[0, :2 * N_HEADS].set(bias)
    bias_col = bias.reshape(2 * N_HEADS, 1)
    ghead = ghead.reshape(1, V_WIDTH)

    def const_inputs(rows, seg):
        ccol, crow = _mlstm_constants(rows, seg)
        arrays = [bias_row, bias_col, ccol, crow, ghead]
        return arrays, [_const_spec(a.shape) for a in arrays]

    rows = PROMPT_ROWS
    chunks = prompt_len // rows
    blk = lambda b, ci: b * chunks + ci
    arrays, specs = const_inputs(rows, rows)
    y_p, c_p, n_p, m_p = pl.pallas_call(
        _mlstm_prompt_body,
        grid=(n_prompt_seq, chunks),
        in_specs=_mixer_row_specs(rows, blk)
        + [pl.BlockSpec((1, 2 * N_HEADS, rows), lambda b, ci: (blk(b, ci), 0, 0))] + specs,
        out_specs=[
            pl.BlockSpec((rows, V_WIDTH), lambda b, ci: (blk(b, ci), 0)),
            _state_spec(1, (N_HEADS, DQK, DV)), _state_spec(1, (N_HEADS, DQK)), _state_spec(1, (N_HEADS, LANES)),
        ],
        out_shape=[
            jax.ShapeDtypeStruct((n_prompt, V_WIDTH), BF16),
            jax.ShapeDtypeStruct((n_prompt_seq, N_HEADS, DQK, DV), F32),
            jax.ShapeDtypeStruct((n_prompt_seq, N_HEADS, DQK), F32),
            jax.ShapeDtypeStruct((n_prompt_seq, N_HEADS, LANES), F32),
        ],
        compiler_params=pltpu.CompilerParams(dimension_semantics=("arbitrary",) * 2, vmem_limit_bytes=VMEM_LIMIT),
        name="mlstm_prompt",
    )(proj, proj, proj, gcol, tiles(gates[:n_prompt], rows), *arrays)

    rows = SAMPLE_ROWS
    spb = rows // seg_len
    first = n_prompt // rows
    sblk = lambda i: first + i
    m0b = jnp.broadcast_to(m0[:, :, None], m0.shape + (LANES,))
    arrays, specs = const_inputs(rows, seg_len)
    y_s, c_s, n_s, m_s = pl.pallas_call(
        functools.partial(_mlstm_sample_body, seg_len=seg_len),
        grid=(n_sample_seq // spb,),
        in_specs=_mixer_row_specs(rows, sblk) + [pl.BlockSpec((1, 2 * N_HEADS, rows), lambda i: (i, 0, 0))] + specs
        + [_state_spec(spb, (N_HEADS, DQK, DV)), _state_spec(spb, (N_HEADS, DQK)), _state_spec(spb, (N_HEADS, LANES))],
        out_specs=[
            pl.BlockSpec((rows, V_WIDTH), lambda i: (i, 0)),
            _state_spec(spb, (N_HEADS, DQK, DV)), _state_spec(spb, (N_HEADS, DQK)), _state_spec(spb, (N_HEADS, LANES)),
        ],
        out_shape=[
            jax.ShapeDtypeStruct((m_rows - n_prompt, V_WIDTH), BF16),
            jax.ShapeDtypeStruct(c0.shape, F32),
            jax.ShapeDtypeStruct(n0.shape, F32),
            jax.ShapeDtypeStruct(m0b.shape, F32),
        ],
        compiler_params=pltpu.CompilerParams(dimension_semantics=("arbitrary",), vmem_limit_bytes=VMEM_LIMIT),
        name="mlstm_sample",
    )(proj, proj, proj, gcol, tiles(gates[n_prompt:], rows), *arrays, c0, n0, m0b)

    return (y_p, y_s), (c_p, n_p, m_p[:, :, 0]), (c_s, n_s, m_s[:, :, 0])


def _gla_mixer(proj, glr, w_gate2, b_gate, ghead, s0, n_prompt_seq, prompt_len):
    m_rows = proj.shape[0]
    n_prompt = n_prompt_seq * prompt_len
    n_sample_seq = s0.shape[0]
    seg_len = (m_rows - n_prompt) // n_sample_seq
    w2 = jnp.zeros((LANES, QK_WIDTH), BF16).at[:GATE_RANK].set(w_gate2.astype(BF16))
    bg = b_gate.reshape(1, QK_WIDTH).astype(F32)
    ghead = ghead.reshape(1, V_WIDTH)

    def const_inputs(rows, seg):
        emat, bd, segind = _gla_constants(rows, seg)
        arrays = [w2, bg, emat, bd, segind, ghead]
        return arrays, [_const_spec(a.shape) for a in arrays]

    rows = PROMPT_ROWS
    chunks = prompt_len // rows
    blk = lambda b, ci: b * chunks + ci
    arrays, specs = const_inputs(rows, rows)
    y_p, s_p = pl.pallas_call(
        _gla_prompt_body,
        grid=(n_prompt_seq, chunks),
        in_specs=_mixer_row_specs(rows, blk) + specs,
        out_specs=[
            pl.BlockSpec((rows, V_WIDTH), lambda b, ci: (blk(b, ci), 0)),
            _state_spec(1, (N_HEADS, DQK, DV)),
        ],
        out_shape=[
            jax.ShapeDtypeStruct((n_prompt, V_WIDTH), BF16),
            jax.ShapeDtypeStruct((n_prompt_seq, N_HEADS, DQK, DV), F32),
        ],
        compiler_params=pltpu.CompilerParams(dimension_semantics=("arbitrary",) * 2, vmem_limit_bytes=VMEM_LIMIT),
        name="gla_prompt",
    )(proj, proj, proj, glr, *arrays)

    rows = SAMPLE_ROWS
    spb = rows // seg_len
    first = n_prompt // rows
    sblk = lambda i: first + i
    arrays, specs = const_inputs(rows, seg_len)
    y_s, s_s = pl.pallas_call(
        functools.partial(_gla_sample_body, seg_len=seg_len),
        grid=(n_sample_seq // spb,),
        in_specs=_mixer_row_specs(rows, sblk) + specs + [_state_spec(spb, (N_HEADS, DQK, DV))],
        out_specs=[
            pl.BlockSpec((rows, V_WIDTH), lambda i: (i, 0)),
            _state_spec(spb, (N_HEADS, DQK, DV)),
        ],
        out_shape=[
            jax.ShapeDtypeStruct((m_rows - n_prompt, V_WIDTH), BF16),
            jax.ShapeDtypeStruct(s0.shape, F32),
        ],
        compiler_params=pltpu.CompilerParams(dimension_semantics=("arbitrary",), vmem_limit_bytes=VMEM_LIMIT),
        name="gla_sample",
    )(proj, proj, proj, glr, *arrays, s0)

    return (y_p, y_s), s_p, s_s


def _pad_gate_columns(w):
    return jnp.zeros((D_MODEL, LANES), BF16).at[:, :w.shape[1]].set(w.astype(BF16))


def kernel(x_prompt, x_sample, state_mlstm_C, state_mlstm_n, state_mlstm_m, state_gla_S, g_pre_mix, g_post_mix, g_pre_ffn, g_post_ffn, w_in_mlstm, b_i_mlstm, b_f_mlstm, g_head_mlstm, w_out_mlstm, w_in_gla, w_gate2_gla, b_gate_gla, g_head_gla, w_out_gla, w_ffn_gate, w_ffn_up, w_ffn_down):
    bp, sp, d = x_prompt.shape
    bs, ss, _ = x_sample.shape
    depth = g_pre_mix.shape[0]
    assert d == D_MODEL and sp % PROMPT_ROWS == 0 and SAMPLE_ROWS % ss == 0 and (bs * ss) % SAMPLE_ROWS == 0
    assert (bp * sp) % TM == 0 and (bs * ss) % TM == 0

    xp = x_prompt.reshape(bp * sp, d)
    xs = x_sample.reshape(bs * ss, d)
    vec = lambda g: g.reshape(1, D_MODEL).astype(F32)

    prompt_states = {"C": [], "n": [], "m": [], "S": []}
    sample_states = {"C": [], "n": [], "m": [], "S": []}
    for layer in range(depth):
        j = layer // 2
        if layer % 2 == 0:
            w_in = w_in_mlstm[j]
            proj, gcol = _inproj(xp, xs, vec(g_pre_mix[layer]), w_in[:, :MAIN_WIDTH].astype(BF16),
                                 _pad_gate_columns(w_in[:, MAIN_WIDTH:]))
            y, st_p, st_s = _mlstm_mixer(proj, gcol, b_i_mlstm[j], b_f_mlstm[j], g_head_mlstm[j],
                                         state_mlstm_C[j], state_mlstm_n[j], state_mlstm_m[j], bp, sp)
            for dst, st in ((prompt_states, st_p), (sample_states, st_s)):
                dst["C"].append(st[0]); dst["n"].append(st[1]); dst["m"].append(st[2])
            w_out = w_out_mlstm[j]
        else:
            w_in = w_in_gla[j]
            proj, glr = _inproj(xp, xs, vec(g_pre_mix[layer]), w_in[:, :MAIN_WIDTH].astype(BF16),
                                _pad_gate_columns(w_in[:, MAIN_WIDTH:]))
            y, s_p, s_s = _gla_mixer(proj, glr, w_gate2_gla[j], b_gate_gla[j], g_head_gla[j],
                                     state_gla_S[j], bp, sp)
            prompt_states["S"].append(s_p)
            sample_states["S"].append(s_s)
            w_out = w_out_gla[j]
        xp, xs = _block_tail(y[0], y[1], w_out.astype(BF16), xp, xs, vec(g_post_mix[layer]), vec(g_pre_ffn[layer]),
                             vec(g_post_ffn[layer]), w_ffn_gate[layer].astype(BF16), w_ffn_up[layer].astype(BF16),
                             w_ffn_down[layer].astype(BF16))

    stack = lambda xs_: jnp.stack(xs_)
    return (xp.reshape(bp, sp, d), xs.reshape(bs, ss, d),
            stack(prompt_states["C"]), stack(prompt_states["n"]), stack(prompt_states["m"]), stack(prompt_states["S"]),
            stack(sample_states["C"]), stack(sample_states["n"]), stack(sample_states["m"]), stack(sample_states["S"]))
```

```python
import functools

import numpy as np
import jax
import jax.numpy as jnp
from jax import lax
from jax.experimental import pallas as pl
from jax.experimental.pallas import tpu as pltpu

F32 = jnp.float32
BF16 = jnp.bfloat16

D_MODEL = 1024
N_HEADS = 4
DQK = 128
DV = 256
QK_WIDTH = N_HEADS * DQK
V_WIDTH = N_HEADS * DV
MAIN_WIDTH = 2 * QK_WIDTH + 2 * V_WIDTH
GATE_RANK = 16
GATE_TAU = 16.0
EPS = 1e-6
LANES = 128
SUBLANES = 8
MXU_DIM = 256
VMEM_LIMIT = 56 * 1024 * 1024

PROMPT_ROWS = 128
SAMPLE_ROWS = 64
LEVELS = (64, 32, 16, 8, 4, 2, 1)
TM = 512
FF_SPLITS = 2


def _dot(a, b):
    return jnp.dot(a.astype(BF16), b.astype(BF16), preferred_element_type=F32)


def _dot_nt(a, b):
    return lax.dot_general(a.astype(BF16), b.astype(BF16), (((1,), (1,)), ((), ())),
                           preferred_element_type=F32)


def _dot_tn(a, b):
    return lax.dot_general(a.astype(BF16), b.astype(BF16), (((0,), (0,)), ((), ())),
                           preferred_element_type=F32)


def _split_hi_lo(x):
    hi = x.astype(BF16)
    lo = (x - hi.astype(F32)).astype(BF16)
    return hi, lo


def _log_sigmoid(x):
    return jnp.minimum(x, 0.0) - jnp.log(1.0 + jnp.exp(-jnp.abs(x)))


def _sigmoid(x):
    return 1.0 / (1.0 + jnp.exp(-x))


def _rms(x, gain):
    return x * lax.rsqrt(jnp.mean(x * x, axis=-1, keepdims=True) + EPS) * gain


def _rows_per_segment(values, seg_len):
    parts = [jnp.broadcast_to(v, (seg_len, v.shape[1])) for v in values]
    return parts[0] if len(parts) == 1 else jnp.concatenate(parts, axis=0)


def _const_spec(shape):
    nd = len(shape)
    return pl.BlockSpec(shape, lambda *_: (0,) * nd)


def _two_group_specs(n_prompt_blocks, width):
    prompt = pl.BlockSpec((TM, width), lambda i: (jnp.minimum(i, n_prompt_blocks - 1), 0))
    sample = pl.BlockSpec((TM, width), lambda i: (jnp.maximum(i - n_prompt_blocks, 0), 0))
    return prompt, sample


def _resident_spec(shape):
    nd = len(shape)
    return pl.BlockSpec(shape, lambda i: (0,) * nd, pipeline_mode=pl.Buffered(1))


def _inproj_body(xp_ref, xs_ref, gain_ref, w_ref, wg_ref, o_ref, og_ref, h_scr, *, n_prompt_blocks):
    i = pl.program_id(0)

    def normalise(x_ref):
        h_scr[...] = _rms(x_ref[...], gain_ref[...]).astype(BF16)

    pl.when(i < n_prompt_blocks)(lambda: normalise(xp_ref))
    pl.when(i >= n_prompt_blocks)(lambda: normalise(xs_ref))
    h = h_scr[...]
    og_ref[...] = jnp.dot(h, wg_ref[...], preferred_element_type=F32)
    o_ref[...] = jnp.dot(h, w_ref[...], preferred_element_type=F32)


def _inproj(xp, xs, gain, w_main, w_gate):
    npb = xp.shape[0] // TM
    m = xp.shape[0] + xs.shape[0]
    spec_p, spec_s = _two_group_specs(npb, D_MODEL)
    return pl.pallas_call(
        functools.partial(_inproj_body, n_prompt_blocks=npb),
        grid=(m // TM,),
        in_specs=[spec_p, spec_s, _resident_spec((1, D_MODEL)), _resident_spec((D_MODEL, MAIN_WIDTH)),
                  _resident_spec((D_MODEL, LANES))],
        out_specs=[
            pl.BlockSpec((TM, MAIN_WIDTH), lambda i: (i, 0)),
            pl.BlockSpec((TM, LANES), lambda i: (i, 0)),
        ],
        out_shape=[
            jax.ShapeDtypeStruct((m, MAIN_WIDTH), F32),
            jax.ShapeDtypeStruct((m, LANES), F32),
        ],
        scratch_shapes=[pltpu.VMEM((TM, D_MODEL), BF16)],
        compiler_params=pltpu.CompilerParams(dimension_semantics=("arbitrary",), vmem_limit_bytes=VMEM_LIMIT),
        name="inproj",
    )(xp, xs, gain, w_main, w_gate)


def _tail_body(yp_ref, ys_ref, wo_ref, xp_ref, xs_ref, gpm_ref, gpf_ref, gqf_ref, wg_ref, wu_ref, wd_ref,
               op_ref, os_ref, x1_scr, *, n_prompt_blocks):
    i = pl.program_id(0)

    def residual(y_ref, x_ref):
        mix = jnp.dot(y_ref[...], wo_ref[...], preferred_element_type=F32)
        x1_scr[...] = x_ref[...] + _rms(mix, gpm_ref[...])

    pl.when(i < n_prompt_blocks)(lambda: residual(yp_ref, xp_ref))
    pl.when(i >= n_prompt_blocks)(lambda: residual(ys_ref, xs_ref))

    x1 = x1_scr[...]
    h = _rms(x1, gpf_ref[...]).astype(BF16)
    d_ff = wg_ref.shape[1]
    n_tiles = d_ff // MXU_DIM
    bounds = [MXU_DIM * ((n_tiles * t + FF_SPLITS - 1) // FF_SPLITS) for t in range(FF_SPLITS + 1)]
    ffn = None
    for t in range(FF_SPLITS):
        cols = slice(bounds[t], bounds[t + 1])
        gate = jnp.dot(h, wg_ref[:, cols], preferred_element_type=F32)
        up = jnp.dot(h, wu_ref[:, cols], preferred_element_type=F32)
        act = (gate * _sigmoid(gate) * up).astype(BF16)
        part = jnp.dot(act, wd_ref[cols, :], preferred_element_type=F32)
        ffn = part if ffn is None else ffn + part
    out = x1 + _rms(ffn, gqf_ref[...])

    @pl.when(i < n_prompt_blocks)
    def _():
        op_ref[...] = out

    @pl.when(i >= n_prompt_blocks)
    def _():
        os_ref[...] = out


def _block_tail(yp, ys, w_out, xp, xs, g_post_mix, g_pre_ffn, g_post_ffn, w_gate, w_up, w_down):
    npb = xp.shape[0] // TM
    m = xp.shape[0] + xs.shape[0]
    d_ff = w_gate.shape[1]
    assert d_ff % MXU_DIM == 0
    spec_p, spec_s = _two_group_specs(npb, D_MODEL)
    vec = _resident_spec((1, D_MODEL))
    return pl.pallas_call(
        functools.partial(_tail_body, n_prompt_blocks=npb),
        grid=(m // TM,),
        in_specs=[
            spec_p, spec_s,
            _resident_spec((V_WIDTH, D_MODEL)),
            spec_p, spec_s,
            vec, vec, vec,
            _resident_spec((D_MODEL, d_ff)), _resident_spec((D_MODEL, d_ff)), _resident_spec((d_ff, D_MODEL)),
        ],
        out_specs=[spec_p, spec_s],
        out_shape=[jax.ShapeDtypeStruct(xp.shape, F32), jax.ShapeDtypeStruct(xs.shape, F32)],
        scratch_shapes=[pltpu.VMEM((TM, D_MODEL), F32)],
        compiler_params=pltpu.CompilerParams(dimension_semantics=("arbitrary",), vmem_limit_bytes=VMEM_LIMIT),
        name="block_tail",
    )(yp, ys, w_out, xp, xs, g_post_mix, g_pre_ffn, g_post_ffn, w_gate, w_up, w_down)


def _segment_structure(rows, seg_len):
    t = np.arange(rows)
    seg = t // seg_len
    same = seg[:, None] == seg[None, :]
    cum = same & (t[None, :] <= t[:, None])
    return t, seg, same, cum


def _mlstm_constants(rows, seg_len):
    _, _, same, cum = _segment_structure(rows, seg_len)
    col = np.concatenate([cum, same], axis=0).astype(np.float32)
    row = cum.T.astype(np.float32)
    return jnp.asarray(col, BF16), jnp.asarray(row, BF16)


def _mlstm_rows(qk_ref, v_ref, o_ref, gcol_ref, grow_ref, bias_row_ref, bias_col_ref, ccol_ref, crow_ref,
                ghead_ref, c_in, n_in, m_in, c_out, n_out, m_out, y_ref, rows, seg_len):
    nseg = rows // seg_len
    r = lax.broadcasted_iota(jnp.int32, (rows, rows), 0)
    c = lax.broadcasted_iota(jnp.int32, (rows, rows), 1)
    same = None if nseg == 1 else (r // seg_len) == (c // seg_len)
    lower = (c <= r) if nseg == 1 else same & (c <= r)
    row_seg = lax.broadcasted_iota(jnp.int32, (rows, DQK), 0) // seg_len

    pre_col = gcol_ref[...] + bias_row_ref[...]
    pre_row = grow_ref[0] + bias_col_ref[...]
    ccol = ccol_ref[...]
    crow = crow_ref[...]
    hi, lo = _split_hi_lo(_log_sigmoid(pre_col))
    sums_col = jnp.dot(ccol, hi, preferred_element_type=F32) + jnp.dot(ccol, lo, preferred_element_type=F32)
    hi, lo = _split_hi_lo(_log_sigmoid(pre_row))
    sums_row = jnp.dot(hi, crow, preferred_element_type=F32) + jnp.dot(lo, crow, preferred_element_type=F32)

    qk = qk_ref[...]
    v_all = v_ref[...]
    o_all = o_ref[...]
    ghead = ghead_ref[...]
    c_prev = [[c_in[s, h] for h in range(N_HEADS)] for s in range(nseg)]
    n_state = [[n_in[s, h:h + 1, :] for h in range(N_HEADS)] for s in range(nseg)]
    m_state = [[m_in[s, h:h + 1, 0:1] for h in range(N_HEADS)] for s in range(nseg)]
    writes = []

    heads = range(N_HEADS)
    q = [qk[:, h * DQK:(h + 1) * DQK] for h in heads]
    k = [qk[:, QK_WIDTH + h * DQK:QK_WIDTH + (h + 1) * DQK] * (DQK ** -0.5) for h in heads]
    v_bf = [v_all[:, h * DV:(h + 1) * DV].astype(BF16) for h in heads]
    i_col = [pre_col[:, h:h + 1] for h in heads]
    b_col = [sums_col[0:rows, N_HEADS + h:N_HEADS + h + 1] for h in heads]
    b_last = [sums_col[rows:2 * rows, N_HEADS + h:N_HEADS + h + 1] for h in heads]
    src = [pre_row[h:h + 1, :] - sums_row[N_HEADS + h:N_HEADS + h + 1, :] for h in heads]
    m_prev = [_rows_per_segment([m_state[s][h] for s in range(nseg)], seg_len) for h in heads]
    n_prev = [_rows_per_segment([n_state[s][h] for s in range(nseg)], seg_len) for h in heads]

    dmat = [jnp.where(lower, b_col[h] + src[h], -jnp.inf) for h in heads]
    inter = [b_col[h] + m_prev[h] for h in heads]
    m_t = [jnp.maximum(inter[h], jnp.max(dmat[h], axis=1, keepdims=True)) for h in heads]
    dlast = [b_last[h] + src[h] for h in heads]
    if nseg > 1:
        dlast = [jnp.where(same, d, -jnp.inf) for d in dlast]
    m_new = [jnp.maximum(b_last[h] + m_prev[h], jnp.max(dlast[h], axis=1, keepdims=True)) for h in heads]
    scores = [_dot_nt(q[h], k[h]) for h in heads]
    q_bf = [q[h].astype(BF16) for h in heads]
    qc = []
    for h in heads:
        parts = []
        for s in range(nseg):
            full = jnp.dot(q_bf[h], c_prev[s][h].astype(BF16), preferred_element_type=F32)
            parts.append(full[s * seg_len:(s + 1) * seg_len] if nseg > 1 else full)
        qc.append(parts[0] if nseg == 1 else jnp.concatenate(parts, axis=0))
    qn = [jnp.sum(q[h] * n_prev[h], axis=1, keepdims=True) for h in heads]

    w_inter = [jnp.exp(inter[h] - m_t[h]) for h in heads]
    a = [scores[h] * jnp.exp(dmat[h] - m_t[h]) for h in heads]
    w_s = [jnp.exp(b_last[h] - b_col[h] + i_col[h] - m_new[h]) for h in heads]
    decay = [jnp.exp(b_last[h] + m_prev[h] - m_new[h]) for h in heads]
    kw = [k[h] * w_s[h] for h in heads]

    num = [jnp.dot(a[h].astype(BF16), v_bf[h], preferred_element_type=F32) + w_inter[h] * qc[h] for h in heads]
    den = [jnp.sum(a[h], axis=1, keepdims=True) + w_inter[h] * qn[h] for h in heads]
    for h in heads:
        for s in range(nseg):
            kw_s = kw[h] if nseg == 1 else jnp.where(row_seg == s, kw[h], 0.0)
            dec = decay[h][s * seg_len:s * seg_len + 1, :]
            writes.append((s, h, dec * c_prev[s][h] + _dot_tn(kw_s, v_bf[h]),
                           dec * n_state[s][h] + jnp.sum(kw_s, axis=0, keepdims=True),
                           jnp.broadcast_to(m_new[h][s * seg_len:s * seg_len + 1, :], (1, LANES))))

    hval = [num[h] * (1.0 / jnp.maximum(jnp.abs(den[h]), jnp.exp(-m_t[h]))) for h in heads]
    hn = [_rms(hval[h], ghead[:, h * DV:(h + 1) * DV]) for h in heads]
    for h in heads:
        y_ref[:, h * DV:(h + 1) * DV] = (_sigmoid(o_all[:, h * DV:(h + 1) * DV]) * hn[h]).astype(y_ref.dtype)

    for s, h, c_new, n_new, m_new in writes:
        c_out[s, h] = c_new
        n_out[s, h:h + 1, :] = n_new
        m_out[s, h:h + 1, :] = m_new


def _mlstm_prompt_body(qk_ref, v_ref, o_ref, gcol_ref, grow_ref, brow_ref, bcol_ref, ccol_ref, crow_ref,
                       ghead_ref, y_ref, c_ref, n_ref, m_ref):
    @pl.when(pl.program_id(1) == 0)
    def _():
        c_ref[...] = jnp.zeros_like(c_ref)
        n_ref[...] = jnp.zeros_like(n_ref)
        m_ref[...] = jnp.zeros_like(m_ref)

    _mlstm_rows(qk_ref, v_ref, o_ref, gcol_ref, grow_ref, brow_ref, bcol_ref, ccol_ref, crow_ref, ghead_ref,
                c_ref, n_ref, m_ref, c_ref, n_ref, m_ref, y_ref, PROMPT_ROWS, PROMPT_ROWS)


def _mlstm_sample_body(qk_ref, v_ref, o_ref, gcol_ref, grow_ref, brow_ref, bcol_ref, ccol_ref, crow_ref,
                       ghead_ref, c_in, n_in, m_in, y_ref, c_out, n_out, m_out, *, seg_len):
    _mlstm_rows(qk_ref, v_ref, o_ref, gcol_ref, grow_ref, brow_ref, bcol_ref, ccol_ref, crow_ref, ghead_ref,
                c_in, n_in, m_in, c_out, n_out, m_out, y_ref, SAMPLE_ROWS, seg_len)


def _gla_constants(rows, seg_len):
    t, seg, _, cum = _segment_structure(rows, seg_len)
    level_id = np.full((rows, rows), -1, np.int32)
    level_id[t, t] = 0
    for idx, hb in enumerate(LEVELS):
        if 2 * hb > seg_len:
            continue
        block = t // (2 * hb)
        upper = (t % (2 * hb)) >= hb
        level_id[(block[:, None] == block[None, :]) & upper[:, None] & ~upper[None, :]] = idx + 1
    segind = (seg[:, None] == np.arange(LANES)[None, :]).astype(np.float32)
    return jnp.asarray(cum.astype(np.float32), BF16), jnp.asarray(level_id), jnp.asarray(segind, BF16)


def _row_bcast(x, row, n):
    return jnp.broadcast_to(x[row:row + 1, :], (n, x.shape[1]))


def _gla_rows(qk_ref, v_ref, r_ref, glr_ref, w2_ref, bg_ref, cum_ref, lvl_ref, segind_ref, ghead_ref,
              s_in, s_out, y_ref, rows, seg_len):
    nseg = rows // seg_len
    ntile = rows // SUBLANES
    row_id = lax.broadcasted_iota(jnp.int32, (rows, DQK), 0)
    row_seg = row_id // seg_len
    row_in_tile = row_id % SUBLANES

    pre = _dot(glr_ref[...], w2_ref[...]) + bg_ref[...]
    g_all = _log_sigmoid(pre) * (1.0 / GATE_TAU)
    g_hi, g_lo = _split_hi_lo(g_all)
    cum = cum_ref[...]
    bc_all = jnp.dot(cum, g_hi, preferred_element_type=F32) + jnp.dot(cum, g_lo, preferred_element_type=F32)
    level_tiles = [lvl_ref[t * SUBLANES:(t + 1) * SUBLANES, :] for t in range(ntile)]
    segind = segind_ref[...]
    total_col = (lax.dot_general(g_hi, segind, (((0,), (0,)), ((), ())), preferred_element_type=F32)
                 + lax.dot_general(g_lo, segind, (((0,), (0,)), ((), ())), preferred_element_type=F32))

    qk = qk_ref[...]
    v_all = v_ref[...]
    r_all = r_ref[...]
    ghead = ghead_ref[...]
    s_prev = [[s_in[s, h] for h in range(N_HEADS)] for s in range(nseg)]
    writes = []

    heads = range(N_HEADS)
    sl = [slice(h * DQK, (h + 1) * DQK) for h in heads]
    q = [qk[:, sl[h]] * (DQK ** -0.5) for h in heads]
    k = [qk[:, QK_WIDTH + h * DQK:QK_WIDTH + (h + 1) * DQK] for h in heads]
    v_bf = [v_all[:, h * DV:(h + 1) * DV].astype(BF16) for h in heads]
    g = [g_all[:, sl[h]] for h in heads]
    bc = [bc_all[:, sl[h]] for h in heads]
    seg_total = [jnp.concatenate([_row_bcast(bc[h], (s + 1) * seg_len - 1, seg_len) for s in range(nseg)], axis=0)
                 if nseg > 1 else _row_bcast(bc[h], rows - 1, rows) for h in heads]

    qq = [(q[h] * jnp.exp(bc[h])).astype(BF16) for h in heads]
    kk = [k[h] * jnp.exp(seg_total[h] - bc[h]) for h in heads]
    o_inter = []
    for h in heads:
        parts = []
        for s in range(nseg):
            st = s_prev[s][h]
            oi = jnp.dot(qq[h], st.astype(BF16), preferred_element_type=F32)
            parts.append(oi[s * seg_len:(s + 1) * seg_len] if nseg > 1 else oi)
            kk_s = kk[h] if nseg == 1 else jnp.where(row_seg == s, kk[h], 0.0)
            writes.append((s, h, st * jnp.exp(total_col[sl[h], s:s + 1]) + _dot_tn(kk_s, v_bf[h])))
        o_inter.append(parts[0] if nseg == 1 else jnp.concatenate(parts, axis=0))

    a = [[None] * ntile for h in heads]

    def put(h, t, level, z_rows):
        prev = 0.0 if a[h][t] is None else a[h][t]
        a[h][t] = jnp.where(level_tiles[t] == level, z_rows, prev)

    for h in heads:
        z = _dot_nt(q[h], k[h])
        for t in range(ntile):
            put(h, t, 0, z[t * SUBLANES:(t + 1) * SUBLANES])

    for idx, hb in enumerate(LEVELS):
        if 2 * hb > seg_len:
            continue
        level = idx + 1
        if hb >= SUBLANES:
            starts = range(0, rows, 2 * hb)
            for h in heads:
                e_parts, x_parts = [], []
                for r0 in starts:
                    ref = _row_bcast(bc[h], r0 + hb - 1, hb)
                    e_parts += [ref - bc[h][r0:r0 + hb], bc[h][r0 + hb:r0 + 2 * hb] - ref]
                    x_parts += [k[h][r0:r0 + hb], q[h][r0 + hb:r0 + 2 * hb]]
                u = jnp.concatenate(x_parts, axis=0) * jnp.exp(jnp.concatenate(e_parts, axis=0))
                u_upper = jnp.concatenate([u[r0 + hb:r0 + 2 * hb] for r0 in starts], axis=0)
                z = _dot_nt(u_upper, u)
                upper_tiles = [t for r0 in starts for t in range((r0 + hb) // SUBLANES, (r0 + 2 * hb) // SUBLANES)]
                for i, t in enumerate(upper_tiles):
                    put(h, t, level, z[i * SUBLANES:(i + 1) * SUBLANES])
        else:
            upper = (row_id % (2 * hb)) >= hb
            for h in heads:
                if hb == 1:
                    e_l = jnp.where(upper, g[h], 0.0)
                else:
                    tiles = range(ntile)
                    if hb == 4:
                        ref = jnp.concatenate([_row_bcast(bc[h], t * SUBLANES + 3, SUBLANES) for t in tiles], axis=0)
                    else:
                        lo = jnp.concatenate([_row_bcast(bc[h], t * SUBLANES + 1, SUBLANES) for t in tiles], axis=0)
                        hi = jnp.concatenate([_row_bcast(bc[h], t * SUBLANES + 5, SUBLANES) for t in tiles], axis=0)
                        ref = jnp.where(row_in_tile < 4, lo, hi)
                    d = bc[h] - ref
                    e_l = jnp.where(upper, d, -d)
                u = jnp.where(upper, q[h], k[h]) * jnp.exp(e_l)
                z = _dot_nt(u, u)
                for t in range(ntile):
                    put(h, t, level, z[t * SUBLANES:(t + 1) * SUBLANES])

    o = [jnp.dot(jnp.concatenate(a[h], axis=0).astype(BF16), v_bf[h], preferred_element_type=F32) + o_inter[h]
         for h in heads]
    hn = [_rms(o[h], ghead[:, h * DV:(h + 1) * DV]) for h in heads]
    for h in heads:
        gate = r_all[:, h * DV:(h + 1) * DV]
        y_ref[:, h * DV:(h + 1) * DV] = (gate * _sigmoid(gate) * hn[h]).astype(y_ref.dtype)

    for s, h, s_new in writes:
        s_out[s, h] = s_new


def _gla_prompt_body(qk_ref, v_ref, r_ref, glr_ref, w2_ref, bg_ref, cum_ref, lvl_ref, segind_ref, ghead_ref,
                     y_ref, s_ref):
    @pl.when(pl.program_id(1) == 0)
    def _():
        s_ref[...] = jnp.zeros_like(s_ref)

    _gla_rows(qk_ref, v_ref, r_ref, glr_ref, w2_ref, bg_ref, cum_ref, lvl_ref, segind_ref, ghead_ref,
              s_ref, s_ref, y_ref, PROMPT_ROWS, PROMPT_ROWS)


def _gla_sample_body(qk_ref, v_ref, r_ref, glr_ref, w2_ref, bg_ref, cum_ref, lvl_ref, segind_ref, ghead_ref,
                     s_in, y_ref, s_out, *, seg_len):
    _gla_rows(qk_ref, v_ref, r_ref, glr_ref, w2_ref, bg_ref, cum_ref, lvl_ref, segind_ref, ghead_ref,
              s_in, s_out, y_ref, SAMPLE_ROWS, seg_len)


def _mixer_params(n_axes):
    return pltpu.CompilerParams(dimension_semantics=("arbitrary",) * n_axes, vmem_limit_bytes=VMEM_LIMIT)


def _mixer_row_specs(rows, row_block):
    return [
        pl.BlockSpec((rows, 2 * QK_WIDTH), lambda *ids: (row_block(*ids), 0)),
        pl.BlockSpec((rows, V_WIDTH), lambda *ids: (row_block(*ids), 1)),
        pl.BlockSpec((rows, V_WIDTH), lambda *ids: (row_block(*ids), 2)),
        pl.BlockSpec((rows, LANES), lambda *ids: (row_block(*ids), 0)),
    ]


def _state_spec(n, trailing):
    nd = 1 + len(trailing)
    return pl.BlockSpec((n,) + trailing, lambda *ids: (ids[0],) + (0,) * (nd - 1))


def _mlstm_mixer(proj, gcol, bias_i, bias_f, ghead, c0, n0, m0, n_prompt_seq, prompt_len):
    m_rows = proj.shape[0]
    n_prompt = n_prompt_seq * prompt_len
    n_sample_seq = c0.shape[0]
    seg_len = (m_rows - n_prompt) // n_sample_seq
    gates = gcol[:, :2 * N_HEADS]
    tiles = lambda a, rows: jnp.transpose(a.reshape(-1, rows, 2 * N_HEADS), (0, 2, 1))
    bias = jnp.concatenate([bias_i, bias_f]).astype(F32)
    bias_row = jnp.zeros((1, LANES), F32).at[0, :2 * N_HEADS].set(bias)
    bias_col = bias.reshape(2 * N_HEADS, 1)
    ghead = ghead.reshape(1, V_WIDTH)

    def const_inputs(rows, seg):
        ccol, crow = _mlstm_constants(rows, seg)
        arrays = [bias_row, bias_col, ccol, crow, ghead]
        return arrays, [_const_spec(a.shape) for a in arrays]

    rows = PROMPT_ROWS
    chunks = prompt_len // rows
    blk = lambda b, ci: b * chunks + ci
    arrays, specs = const_inputs(rows, rows)
    y_p, c_p, n_p, m_p = pl.pallas_call(
        _mlstm_prompt_body,
        grid=(n_prompt_seq, chunks),
        in_specs=_mixer_row_specs(rows, blk)
        + [pl.BlockSpec((1, 2 * N_HEADS, rows), lambda b, ci: (blk(b, ci), 0, 0))] + specs,
        out_specs=[
            pl.BlockSpec((rows, V_WIDTH), lambda b, ci: (blk(b, ci), 0)),
            _state_spec(1, (N_HEADS, DQK, DV)), _state_spec(1, (N_HEADS, DQK)), _state_spec(1, (N_HEADS, LANES)),
        ],
        out_shape=[
            jax.ShapeDtypeStruct((n_prompt, V_WIDTH), BF16),
            jax.ShapeDtypeStruct((n_prompt_seq, N_HEADS, DQK, DV), F32),
            jax.ShapeDtypeStruct((n_prompt_seq, N_HEADS, DQK), F32),
            jax.ShapeDtypeStruct((n_prompt_seq, N_HEADS, LANES), F32),
        ],
        compiler_params=_mixer_params(2),
        name="mlstm_prompt",
    )(proj, proj, proj, gcol, tiles(gates[:n_prompt], rows), *arrays)

    rows = SAMPLE_ROWS
    spb = rows // seg_len
    first = n_prompt // rows
    sblk = lambda i: first + i
    m0b = jnp.broadcast_to(m0[:, :, None], m0.shape + (LANES,))
    arrays, specs = const_inputs(rows, seg_len)
    y_s, c_s, n_s, m_s = pl.pallas_call(
        functools.partial(_mlstm_sample_body, seg_len=seg_len),
        grid=(n_sample_seq // spb,),
        in_specs=_mixer_row_specs(rows, sblk) + [pl.BlockSpec((1, 2 * N_HEADS, rows), lambda i: (i, 0, 0))] + specs
        + [_state_spec(spb, (N_HEADS, DQK, DV)), _state_spec(spb, (N_HEADS, DQK)), _state_spec(spb, (N_HEADS, LANES))],
        out_specs=[
            pl.BlockSpec((rows, V_WIDTH), lambda i: (i, 0)),
            _state_spec(spb, (N_HEADS, DQK, DV)), _state_spec(spb, (N_HEADS, DQK)), _state_spec(spb, (N_HEADS, LANES)),
        ],
        out_shape=[
            jax.ShapeDtypeStruct((m_rows - n_prompt, V_WIDTH), BF16),
            jax.ShapeDtypeStruct(c0.shape, F32),
            jax.ShapeDtypeStruct(n0.shape, F32),
            jax.ShapeDtypeStruct(m0b.shape, F32),
        ],
        compiler_params=_mixer_params(1),
        name="mlstm_sample",
    )(proj, proj, proj, gcol, tiles(gates[n_prompt:], rows), *arrays, c0, n0, m0b)

    return (y_p, y_s), (c_p, n_p, m_p[:, :, 0]), (c_s, n_s, m_s[:, :, 0])


def _gla_mixer(proj, glr, w_gate2, b_gate, ghead, s0, n_prompt_seq, prompt_len):
    m_rows = proj.shape[0]
    n_prompt = n_prompt_seq * prompt_len
    n_sample_seq = s0.shape[0]
    seg_len = (m_rows - n_prompt) // n_sample_seq
    w2 = jnp.zeros((LANES, QK_WIDTH), BF16).at[:GATE_RANK].set(w_gate2.astype(BF16))
    bg = b_gate.reshape(1, QK_WIDTH).astype(F32)
    ghead = ghead.reshape(1, V_WIDTH)

    def const_inputs(rows, seg):
        cum, level_id, segind = _gla_constants(rows, seg)
        arrays = [w2, bg, cum, level_id, segind, ghead]
        return arrays, [_const_spec(a.shape) for a in arrays]

    rows = PROMPT_ROWS
    chunks = prompt_len // rows
    blk = lambda b, ci: b * chunks + ci
    arrays, specs = const_inputs(rows, rows)
    y_p, s_p = pl.pallas_call(
        _gla_prompt_body,
        grid=(n_prompt_seq, chunks),
        in_specs=_mixer_row_specs(rows, blk) + specs,
        out_specs=[
            pl.BlockSpec((rows, V_WIDTH), lambda b, ci: (blk(b, ci), 0)),
            _state_spec(1, (N_HEADS, DQK, DV)),
        ],
        out_shape=[
            jax.ShapeDtypeStruct((n_prompt, V_WIDTH), BF16),
            jax.ShapeDtypeStruct((n_prompt_seq, N_HEADS, DQK, DV), F32),
        ],
        compiler_params=_mixer_params(2),
        name="gla_prompt",
    )(proj, proj, proj, glr, *arrays)

    rows = SAMPLE_ROWS
    spb = rows // seg_len
    first = n_prompt // rows
    sblk = lambda i: first + i
    arrays, specs = const_inputs(rows, seg_len)
    y_s, s_s = pl.pallas_call(
        functools.partial(_gla_sample_body, seg_len=seg_len),
        grid=(n_sample_seq // spb,),
        in_specs=_mixer_row_specs(rows, sblk) + specs + [_state_spec(spb, (N_HEADS, DQK, DV))],
        out_specs=[
            pl.BlockSpec((rows, V_WIDTH), lambda i: (i, 0)),
            _state_spec(spb, (N_HEADS, DQK, DV)),
        ],
        out_shape=[
            jax.ShapeDtypeStruct((m_rows - n_prompt, V_WIDTH), BF16),
            jax.ShapeDtypeStruct(s0.shape, F32),
        ],
        compiler_params=_mixer_params(1),
        name="gla_sample",
    )(proj, proj, proj, glr, *arrays, s0)

    return (y_p, y_s), s_p, s_s


def _pad_gate_columns(w):
    return jnp.zeros((D_MODEL, LANES), BF16).at[:, :w.shape[1]].set(w.astype(BF16))


def kernel(x_prompt, x_sample, state_mlstm_C, state_mlstm_n, state_mlstm_m, state_gla_S, g_pre_mix, g_post_mix, g_pre_ffn, g_post_ffn, w_in_mlstm, b_i_mlstm, b_f_mlstm, g_head_mlstm, w_out_mlstm, w_in_gla, w_gate2_gla, b_gate_gla, g_head_gla, w_out_gla, w_ffn_gate, w_ffn_up, w_ffn_down):
    bp, sp, d = x_prompt.shape
    bs, ss, _ = x_sample.shape
    depth = g_pre_mix.shape[0]
    assert d == D_MODEL and sp % PROMPT_ROWS == 0 and SAMPLE_ROWS % ss == 0 and (bs * ss) % SAMPLE_ROWS == 0
    assert (bp * sp) % TM == 0 and (bs * ss) % TM == 0

    xp = x_prompt.reshape(bp * sp, d)
    xs = x_sample.reshape(bs * ss, d)
    vec = lambda g: g.reshape(1, D_MODEL).astype(F32)

    prompt_states = {"C": [], "n": [], "m": [], "S": []}
    sample_states = {"C": [], "n": [], "m": [], "S": []}
    for layer in range(depth):
        j = layer // 2
        if layer % 2 == 0:
            w_in = w_in_mlstm[j]
            proj, gcol = _inproj(xp, xs, vec(g_pre_mix[layer]), w_in[:, :MAIN_WIDTH].astype(BF16),
                                 _pad_gate_columns(w_in[:, MAIN_WIDTH:]))
            y, st_p, st_s = _mlstm_mixer(proj, gcol, b_i_mlstm[j], b_f_mlstm[j], g_head_mlstm[j],
                                         state_mlstm_C[j], state_mlstm_n[j], state_mlstm_m[j], bp, sp)
            for dst, st in ((prompt_states, st_p), (sample_states, st_s)):
                dst["C"].append(st[0]); dst["n"].append(st[1]); dst["m"].append(st[2])
            w_out = w_out_mlstm[j]
        else:
            w_in = w_in_gla[j]
            proj, glr = _inproj(xp, xs, vec(g_pre_mix[layer]), w_in[:, :MAIN_WIDTH].astype(BF16),
                                _pad_gate_columns(w_in[:, MAIN_WIDTH:]))
            y, s_p, s_s = _gla_mixer(proj, glr, w_gate2_gla[j], b_gate_gla[j], g_head_gla[j],
                                     state_gla_S[j], bp, sp)
            prompt_states["S"].append(s_p)
            sample_states["S"].append(s_s)
            w_out = w_out_gla[j]
        xp, xs = _block_tail(y[0], y[1], w_out.astype(BF16), xp, xs, vec(g_post_mix[layer]), vec(g_pre_ffn[layer]),
                             vec(g_post_ffn[layer]), w_ffn_gate[layer].astype(BF16), w_ffn_up[layer].astype(BF16),
                             w_ffn_down[layer].astype(BF16))

    stack = lambda xs_: jnp.stack(xs_)
    return (xp.reshape(bp, sp, d), xs.reshape(bs, ss, d),
            stack(prompt_states["C"]), stack(prompt_states["n"]), stack(prompt_states["m"]), stack(prompt_states["S"]),
            stack(sample_states["C"]), stack(sample_states["n"]), stack(sample_states["m"]), stack(sample_states["S"]))
```

```python
import functools

import numpy as np
import jax
import jax.numpy as jnp
from jax import lax
from jax.experimental import pallas as pl
from jax.experimental.pallas import tpu as pltpu

F32 = jnp.float32
BF16 = jnp.bfloat16

D_MODEL = 1024
N_HEADS = 4
DQK = 128
DV = 256
QK_WIDTH = N_HEADS * DQK
V_WIDTH = N_HEADS * DV
MAIN_WIDTH = 2 * QK_WIDTH + 2 * V_WIDTH
GATE_RANK = 16
GATE_TAU = 16.0
EPS = 1e-6
LANES = 128
SUBLANES = 8
MXU_DIM = 256
VMEM_LIMIT = 56 * 1024 * 1024

PROMPT_ROWS = 128
SAMPLE_ROWS = 64
LEVELS = (64, 32, 16, 8, 4, 2, 1)
TM = 512
FF_SPLITS = 2
ROW_GROUPS = 2


def _dot(a, b):
    return jnp.dot(a.astype(BF16), b.astype(BF16), preferred_element_type=F32)


def _dot_nt(a, b):
    return lax.dot_general(a.astype(BF16), b.astype(BF16), (((1,), (1,)), ((), ())),
                           preferred_element_type=F32)


def _dot_tn(a, b):
    return lax.dot_general(a.astype(BF16), b.astype(BF16), (((0,), (0,)), ((), ())),
                           preferred_element_type=F32)


def _split_hi_lo(x):
    hi = x.astype(BF16)
    lo = (x - hi.astype(F32)).astype(BF16)
    return hi, lo


def _log_sigmoid(x):
    return jnp.minimum(x, 0.0) - jnp.log(1.0 + jnp.exp(-jnp.abs(x)))


def _sigmoid(x):
    return 1.0 / (1.0 + jnp.exp(-x))


def _rms(x, gain):
    return x * lax.rsqrt(jnp.mean(x * x, axis=-1, keepdims=True) + EPS) * gain


def _rows_per_segment(values, seg_len):
    parts = [jnp.broadcast_to(v, (seg_len, v.shape[1])) for v in values]
    return parts[0] if len(parts) == 1 else jnp.concatenate(parts, axis=0)


def _const_spec(shape):
    nd = len(shape)
    return pl.BlockSpec(shape, lambda *_: (0,) * nd)


def _two_group_specs(n_prompt_blocks, width):
    prompt = pl.BlockSpec((TM, width), lambda i: (jnp.minimum(i, n_prompt_blocks - 1), 0))
    sample = pl.BlockSpec((TM, width), lambda i: (jnp.maximum(i - n_prompt_blocks, 0), 0))
    return prompt, sample


def _resident_spec(shape):
    nd = len(shape)
    return pl.BlockSpec(shape, lambda i: (0,) * nd, pipeline_mode=pl.Buffered(1))


def _inproj_body(xp_ref, xs_ref, gain_ref, w_ref, wg_ref, o_ref, og_ref, *, n_prompt_blocks):
    is_prompt = pl.program_id(0) < n_prompt_blocks
    group_rows = TM // ROW_GROUPS
    groups = [slice(a * group_rows, (a + 1) * group_rows) for a in range(ROW_GROUPS)]
    h = [None] * ROW_GROUPS
    for a in range(ROW_GROUPS + 1):
        if a < ROW_GROUPS:
            x = jnp.where(is_prompt, xp_ref[groups[a], :], xs_ref[groups[a], :])
            h[a] = _rms(x, gain_ref[...]).astype(BF16)
        if a > 0:
            og_ref[groups[a - 1], :] = jnp.dot(h[a - 1], wg_ref[...], preferred_element_type=F32)
            o_ref[groups[a - 1], :] = jnp.dot(h[a - 1], w_ref[...], preferred_element_type=F32)


def _inproj(xp, xs, gain, w_main, w_gate):
    npb = xp.shape[0] // TM
    m = xp.shape[0] + xs.shape[0]
    spec_p, spec_s = _two_group_specs(npb, D_MODEL)
    return pl.pallas_call(
        functools.partial(_inproj_body, n_prompt_blocks=npb),
        grid=(m // TM,),
        in_specs=[spec_p, spec_s, _resident_spec((1, D_MODEL)), _resident_spec((D_MODEL, MAIN_WIDTH)),
                  _resident_spec((D_MODEL, LANES))],
        out_specs=[
            pl.BlockSpec((TM, MAIN_WIDTH), lambda i: (i, 0)),
            pl.BlockSpec((TM, LANES), lambda i: (i, 0)),
        ],
        out_shape=[
            jax.ShapeDtypeStruct((m, MAIN_WIDTH), F32),
            jax.ShapeDtypeStruct((m, LANES), F32),
        ],
        compiler_params=pltpu.CompilerParams(dimension_semantics=("arbitrary",), vmem_limit_bytes=VMEM_LIMIT),
        name="inproj",
    )(xp, xs, gain, w_main, w_gate)


def _tail_body(yp_ref, ys_ref, wo_ref, xp_ref, xs_ref, gpm_ref, gpf_ref, gqf_ref, wg_ref, wu_ref, wd_ref,
               op_ref, os_ref, out_scr, *, n_prompt_blocks):
    i = pl.program_id(0)
    is_prompt = i < n_prompt_blocks
    d_ff = wg_ref.shape[1]
    n_tiles = d_ff // MXU_DIM
    bounds = [MXU_DIM * ((n_tiles * t + FF_SPLITS - 1) // FF_SPLITS) for t in range(FF_SPLITS + 1)]

    def stages(rows):
        y = jnp.where(is_prompt, yp_ref[rows, :], ys_ref[rows, :])
        mix = jnp.dot(y, wo_ref[...], preferred_element_type=F32)
        yield
        x1 = jnp.where(is_prompt, xp_ref[rows, :], xs_ref[rows, :]) + _rms(mix, gpm_ref[...])
        h = _rms(x1, gpf_ref[...]).astype(BF16)
        yield
        ffn = None
        for t in range(FF_SPLITS):
            cols = slice(bounds[t], bounds[t + 1])
            gate = jnp.dot(h, wg_ref[:, cols], preferred_element_type=F32)
            up = jnp.dot(h, wu_ref[:, cols], preferred_element_type=F32)
            yield
            act = (gate * _sigmoid(gate) * up).astype(BF16)
            yield
            part = jnp.dot(act, wd_ref[cols, :], preferred_element_type=F32)
            ffn = part if ffn is None else ffn + part
            yield
        out_scr[rows, :] = x1 + _rms(ffn, gqf_ref[...])
        yield

    group_rows = TM // ROW_GROUPS
    chains = [stages(slice(a * group_rows, (a + 1) * group_rows)) for a in range(ROW_GROUPS)]
    live = []
    while chains or live:
        if chains:
            live.append(chains.pop(0))
        for chain in list(live):
            if next(chain, StopIteration) is StopIteration:
                live.remove(chain)

    @pl.when(is_prompt)
    def _():
        op_ref[...] = out_scr[...]

    @pl.when(jnp.logical_not(is_prompt))
    def _():
        os_ref[...] = out_scr[...]


def _block_tail(yp, ys, w_out, xp, xs, g_post_mix, g_pre_ffn, g_post_ffn, w_gate, w_up, w_down):
    npb = xp.shape[0] // TM
    m = xp.shape[0] + xs.shape[0]
    d_ff = w_gate.shape[1]
    assert d_ff % MXU_DIM == 0
    spec_p, spec_s = _two_group_specs(npb, D_MODEL)
    vec = _resident_spec((1, D_MODEL))
    return pl.pallas_call(
        functools.partial(_tail_body, n_prompt_blocks=npb),
        grid=(m // TM,),
        in_specs=[
            spec_p, spec_s,
            _resident_spec((V_WIDTH, D_MODEL)),
            spec_p, spec_s,
            vec, vec, vec,
            _resident_spec((D_MODEL, d_ff)), _resident_spec((D_MODEL, d_ff)), _resident_spec((d_ff, D_MODEL)),
        ],
        out_specs=[spec_p, spec_s],
        out_shape=[jax.ShapeDtypeStruct(xp.shape, F32), jax.ShapeDtypeStruct(xs.shape, F32)],
        scratch_shapes=[pltpu.VMEM((TM, D_MODEL), F32)],
        compiler_params=pltpu.CompilerParams(dimension_semantics=("arbitrary",), vmem_limit_bytes=VMEM_LIMIT),
        name="block_tail",
    )(yp, ys, w_out, xp, xs, g_post_mix, g_pre_ffn, g_post_ffn, w_gate, w_up, w_down)


def _segment_structure(rows, seg_len):
    t = np.arange(rows)
    seg = t // seg_len
    same = seg[:, None] == seg[None, :]
    cum = same & (t[None, :] <= t[:, None])
    return t, seg, same, cum


def _mlstm_constants(rows, seg_len):
    _, _, same, cum = _segment_structure(rows, seg_len)
    col = np.concatenate([cum, same], axis=0).astype(np.float32)
    row = cum.T.astype(np.float32)
    return jnp.asarray(col, BF16), jnp.asarray(row, BF16)


def _mlstm_rows(qk_ref, v_ref, o_ref, gcol_ref, grow_ref, bias_row_ref, bias_col_ref, ccol_ref, crow_ref,
                ghead_ref, c_in, n_in, m_in, c_out, n_out, m_out, y_ref, rows, seg_len):
    nseg = rows // seg_len
    r = lax.broadcasted_iota(jnp.int32, (rows, rows), 0)
    c = lax.broadcasted_iota(jnp.int32, (rows, rows), 1)
    same = None if nseg == 1 else (r // seg_len) == (c // seg_len)
    lower = (c <= r) if nseg == 1 else same & (c <= r)
    row_seg = lax.broadcasted_iota(jnp.int32, (rows, DQK), 0) // seg_len

    pre_col = gcol_ref[...] + bias_row_ref[...]
    pre_row = grow_ref[0] + bias_col_ref[...]
    ccol = ccol_ref[...]
    crow = crow_ref[...]
    hi, lo = _split_hi_lo(_log_sigmoid(pre_col))
    sums_col = jnp.dot(ccol, hi, preferred_element_type=F32) + jnp.dot(ccol, lo, preferred_element_type=F32)
    hi, lo = _split_hi_lo(_log_sigmoid(pre_row))
    sums_row = jnp.dot(hi, crow, preferred_element_type=F32) + jnp.dot(lo, crow, preferred_element_type=F32)

    qk = qk_ref[...]
    v_all = v_ref[...]
    o_all = o_ref[...]
    ghead = ghead_ref[...]
    c_prev = [[c_in[s, h] for h in range(N_HEADS)] for s in range(nseg)]
    n_state = [[n_in[s, h:h + 1, :] for h in range(N_HEADS)] for s in range(nseg)]
    m_state = [[m_in[s, h:h + 1, 0:1] for h in range(N_HEADS)] for s in range(nseg)]
    writes = []

    heads = range(N_HEADS)
    q = [qk[:, h * DQK:(h + 1) * DQK] for h in heads]
    k = [qk[:, QK_WIDTH + h * DQK:QK_WIDTH + (h + 1) * DQK] * (DQK ** -0.5) for h in heads]
    v_bf = [v_all[:, h * DV:(h + 1) * DV].astype(BF16) for h in heads]
    i_col = [pre_col[:, h:h + 1] for h in heads]
    b_col = [sums_col[0:rows, N_HEADS + h:N_HEADS + h + 1] for h in heads]
    b_last = [sums_col[rows:2 * rows, N_HEADS + h:N_HEADS + h + 1] for h in heads]
    src = [pre_row[h:h + 1, :] - sums_row[N_HEADS + h:N_HEADS + h + 1, :] for h in heads]
    m_prev = [_rows_per_segment([m_state[s][h] for s in range(nseg)], seg_len) for h in heads]
    n_prev = [_rows_per_segment([n_state[s][h] for s in range(nseg)], seg_len) for h in heads]

    dmat = [jnp.where(lower, b_col[h] + src[h], -jnp.inf) for h in heads]
    inter = [b_col[h] + m_prev[h] for h in heads]
    m_t = [jnp.maximum(inter[h], jnp.max(dmat[h], axis=1, keepdims=True)) for h in heads]
    dlast = [b_last[h] + src[h] for h in heads]
    if nseg > 1:
        dlast = [jnp.where(same, d, -jnp.inf) for d in dlast]
    m_new = [jnp.maximum(b_last[h] + m_prev[h], jnp.max(dlast[h], axis=1, keepdims=True)) for h in heads]
    scores = [_dot_nt(q[h], k[h]) for h in heads]
    q_bf = [q[h].astype(BF16) for h in heads]
    qc = []
    for h in heads:
        parts = []
        for s in range(nseg):
            full = jnp.dot(q_bf[h], c_prev[s][h].astype(BF16), preferred_element_type=F32)
            parts.append(full[s * seg_len:(s + 1) * seg_len] if nseg > 1 else full)
        qc.append(parts[0] if nseg == 1 else jnp.concatenate(parts, axis=0))
    qn = [jnp.sum(q[h] * n_prev[h], axis=1, keepdims=True) for h in heads]

    w_inter = [jnp.exp(inter[h] - m_t[h]) for h in heads]
    a = [scores[h] * jnp.exp(dmat[h] - m_t[h]) for h in heads]
    w_s = [jnp.exp(b_last[h] - b_col[h] + i_col[h] - m_new[h]) for h in heads]
    decay = [jnp.exp(b_last[h] + m_prev[h] - m_new[h]) for h in heads]
    kw = [k[h] * w_s[h] for h in heads]

    num = [jnp.dot(a[h].astype(BF16), v_bf[h], preferred_element_type=F32) + w_inter[h] * qc[h] for h in heads]
    den = [jnp.sum(a[h], axis=1, keepdims=True) + w_inter[h] * qn[h] for h in heads]
    for h in heads:
        for s in range(nseg):
            kw_s = kw[h] if nseg == 1 else jnp.where(row_seg == s, kw[h], 0.0)
            dec = decay[h][s * seg_len:s * seg_len + 1, :]
            writes.append((s, h, dec * c_prev[s][h] + _dot_tn(kw_s, v_bf[h]),
                           dec * n_state[s][h] + jnp.sum(kw_s, axis=0, keepdims=True),
                           jnp.broadcast_to(m_new[h][s * seg_len:s * seg_len + 1, :], (1, LANES))))

    hval = [num[h] * (1.0 / jnp.maximum(jnp.abs(den[h]), jnp.exp(-m_t[h]))) for h in heads]
    hn = [_rms(hval[h], ghead[:, h * DV:(h + 1) * DV]) for h in heads]
    for h in heads:
        y_ref[:, h * DV:(h + 1) * DV] = (_sigmoid(o_all[:, h * DV:(h + 1) * DV]) * hn[h]).astype(y_ref.dtype)

    for s, h, c_new, n_new, m_new in writes:
        c_out[s, h] = c_new
        n_out[s, h:h + 1, :] = n_new
        m_out[s, h:h + 1, :] = m_new


def _mlstm_prompt_body(qk_ref, v_ref, o_ref, gcol_ref, grow_ref, brow_ref, bcol_ref, ccol_ref, crow_ref,
                       ghead_ref, y_ref, c_ref, n_ref, m_ref):
    @pl.when(pl.program_id(1) == 0)
    def _():
        c_ref[...] = jnp.zeros_like(c_ref)
        n_ref[...] = jnp.zeros_like(n_ref)
        m_ref[...] = jnp.zeros_like(m_ref)

    _mlstm_rows(qk_ref, v_ref, o_ref, gcol_ref, grow_ref, brow_ref, bcol_ref, ccol_ref, crow_ref, ghead_ref,
                c_ref, n_ref, m_ref, c_ref, n_ref, m_ref, y_ref, PROMPT_ROWS, PROMPT_ROWS)


def _mlstm_sample_body(qk_ref, v_ref, o_ref, gcol_ref, grow_ref, brow_ref, bcol_ref, ccol_ref, crow_ref,
                       ghead_ref, c_in, n_in, m_in, y_ref, c_out, n_out, m_out, *, seg_len):
    _mlstm_rows(qk_ref, v_ref, o_ref, gcol_ref, grow_ref, brow_ref, bcol_ref, ccol_ref, crow_ref, ghead_ref,
                c_in, n_in, m_in, c_out, n_out, m_out, y_ref, SAMPLE_ROWS, seg_len)


def _gla_constants(rows, seg_len):
    t, seg, _, cum = _segment_structure(rows, seg_len)
    level_id = np.full((rows, rows), -1, np.int32)
    level_id[t, t] = 0
    for idx, hb in enumerate(LEVELS):
        if 2 * hb > seg_len:
            continue
        block = t // (2 * hb)
        upper = (t % (2 * hb)) >= hb
        level_id[(block[:, None] == block[None, :]) & upper[:, None] & ~upper[None, :]] = idx + 1
    segind = (seg[:, None] == np.arange(LANES)[None, :]).astype(np.float32)
    return jnp.asarray(cum.astype(np.float32), BF16), jnp.asarray(level_id), jnp.asarray(segind, BF16)


def _row_bcast(x, row, n):
    return jnp.broadcast_to(x[row:row + 1, :], (n, x.shape[1]))


def _gla_rows(qk_ref, v_ref, r_ref, glr_ref, w2_ref, bg_ref, cum_ref, lvl_ref, segind_ref, ghead_ref,
              s_in, s_out, y_ref, rows, seg_len):
    nseg = rows // seg_len
    ntile = rows // SUBLANES
    row_id = lax.broadcasted_iota(jnp.int32, (rows, DQK), 0)
    row_seg = row_id // seg_len
    row_in_tile = row_id % SUBLANES

    pre = _dot(glr_ref[...], w2_ref[...]) + bg_ref[...]
    g_all = _log_sigmoid(pre) * (1.0 / GATE_TAU)
    g_hi, g_lo = _split_hi_lo(g_all)
    cum = cum_ref[...]
    bc_all = jnp.dot(cum, g_hi, preferred_element_type=F32) + jnp.dot(cum, g_lo, preferred_element_type=F32)
    level_tiles = [lvl_ref[t * SUBLANES:(t + 1) * SUBLANES, :] for t in range(ntile)]
    segind = segind_ref[...]
    total_col = (lax.dot_general(g_hi, segind, (((0,), (0,)), ((), ())), preferred_element_type=F32)
                 + lax.dot_general(g_lo, segind, (((0,), (0,)), ((), ())), preferred_element_type=F32))

    qk = qk_ref[...]
    v_all = v_ref[...]
    r_all = r_ref[...]
    ghead = ghead_ref[...]
    s_prev = [[s_in[s, h] for h in range(N_HEADS)] for s in range(nseg)]
    writes = []

    heads = range(N_HEADS)
    sl = [slice(h * DQK, (h + 1) * DQK) for h in heads]
    q = [qk[:, sl[h]] * (DQK ** -0.5) for h in heads]
    k = [qk[:, QK_WIDTH + h * DQK:QK_WIDTH + (h + 1) * DQK] for h in heads]
    v_bf = [v_all[:, h * DV:(h + 1) * DV].astype(BF16) for h in heads]
    g = [g_all[:, sl[h]] for h in heads]
    bc = [bc_all[:, sl[h]] for h in heads]
    seg_total = [jnp.concatenate([_row_bcast(bc[h], (s + 1) * seg_len - 1, seg_len) for s in range(nseg)], axis=0)
                 if nseg > 1 else _row_bcast(bc[h], rows - 1, rows) for h in heads]

    qq = [(q[h] * jnp.exp(bc[h])).astype(BF16) for h in heads]
    kk = [k[h] * jnp.exp(seg_total[h] - bc[h]) for h in heads]
    o_inter = []
    for h in heads:
        parts = []
        for s in range(nseg):
            st = s_prev[s][h]
            oi = jnp.dot(qq[h], st.astype(BF16), preferred_element_type=F32)
            parts.append(oi[s * seg_len:(s + 1) * seg_len] if nseg > 1 else oi)
            kk_s = kk[h] if nseg == 1 else jnp.where(row_seg == s, kk[h], 0.0)
            writes.append((s, h, st * jnp.exp(total_col[sl[h], s:s + 1]) + _dot_tn(kk_s, v_bf[h])))
        o_inter.append(parts[0] if nseg == 1 else jnp.concatenate(parts, axis=0))

    a = [[None] * ntile for h in heads]

    def put(h, t, level, z_rows):
        prev = 0.0 if a[h][t] is None else a[h][t]
        a[h][t] = jnp.where(level_tiles[t] == level, z_rows, prev)

    for h in heads:
        z = _dot_nt(q[h], k[h])
        for t in range(ntile):
            put(h, t, 0, z[t * SUBLANES:(t + 1) * SUBLANES])

    for idx, hb in enumerate(LEVELS):
        if 2 * hb > seg_len:
            continue
        level = idx + 1
        if hb >= SUBLANES:
            starts = range(0, rows, 2 * hb)
            for h in heads:
                e_parts, x_parts = [], []
                for r0 in starts:
                    ref = _row_bcast(bc[h], r0 + hb - 1, hb)
                    e_parts += [ref - bc[h][r0:r0 + hb], bc[h][r0 + hb:r0 + 2 * hb] - ref]
                    x_parts += [k[h][r0:r0 + hb], q[h][r0 + hb:r0 + 2 * hb]]
                u = jnp.concatenate(x_parts, axis=0) * jnp.exp(jnp.concatenate(e_parts, axis=0))
                u_upper = jnp.concatenate([u[r0 + hb:r0 + 2 * hb] for r0 in starts], axis=0)
                z = _dot_nt(u_upper, u)
                upper_tiles = [t for r0 in starts for t in range((r0 + hb) // SUBLANES, (r0 + 2 * hb) // SUBLANES)]
                for i, t in enumerate(upper_tiles):
                    put(h, t, level, z[i * SUBLANES:(i + 1) * SUBLANES])
        else:
            upper = (row_id % (2 * hb)) >= hb
            for h in heads:
                if hb == 1:
                    e_l = jnp.where(upper, g[h], 0.0)
                else:
                    tiles = range(ntile)
                    if hb == 4:
                        ref = jnp.concatenate([_row_bcast(bc[h], t * SUBLANES + 3, SUBLANES) for t in tiles], axis=0)
                    else:
                        lo = jnp.concatenate([_row_bcast(bc[h], t * SUBLANES + 1, SUBLANES) for t in tiles], axis=0)
                        hi = jnp.concatenate([_row_bcast(bc[h], t * SUBLANES + 5, SUBLANES) for t in tiles], axis=0)
                        ref = jnp.where(row_in_tile < 4, lo, hi)
                    d = bc[h] - ref
                    e_l = jnp.where(upper, d, -d)
                u = jnp.where(upper, q[h], k[h]) * jnp.exp(e_l)
                z = _dot_nt(u, u)
                for t in range(ntile):
                    put(h, t, level, z[t * SUBLANES:(t + 1) * SUBLANES])

    o = [jnp.dot(jnp.concatenate(a[h], axis=0).astype(BF16), v_bf[h], preferred_element_type=F32) + o_inter[h]
         for h in heads]
    hn = [_rms(o[h], ghead[:, h * DV:(h + 1) * DV]) for h in heads]
    for h in heads:
        gate = r_all[:, h * DV:(h + 1) * DV]
        y_ref[:, h * DV:(h + 1) * DV] = (gate * _sigmoid(gate) * hn[h]).astype(y_ref.dtype)

    for s, h, s_new in writes:
        s_out[s, h] = s_new


def _gla_prompt_body(qk_ref, v_ref, r_ref, glr_ref, w2_ref, bg_ref, cum_ref, lvl_ref, segind_ref, ghead_ref,
                     y_ref, s_ref):
    @pl.when(pl.program_id(1) == 0)
    def _():
        s_ref[...] = jnp.zeros_like(s_ref)

    _gla_rows(qk_ref, v_ref, r_ref, glr_ref, w2_ref, bg_ref, cum_ref, lvl_ref, segind_ref, ghead_ref,
              s_ref, s_ref, y_ref, PROMPT_ROWS, PROMPT_ROWS)


def _gla_sample_body(qk_ref, v_ref, r_ref, glr_ref, w2_ref, bg_ref, cum_ref, lvl_ref, segind_ref, ghead_ref,
                     s_in, y_ref, s_out, *, seg_len):
    _gla_rows(qk_ref, v_ref, r_ref, glr_ref, w2_ref, bg_ref, cum_ref, lvl_ref, segind_ref, ghead_ref,
              s_in, s_out, y_ref, SAMPLE_ROWS, seg_len)


def _mixer_params(n_axes):
    return pltpu.CompilerParams(dimension_semantics=("arbitrary",) * n_axes, vmem_limit_bytes=VMEM_LIMIT)


def _mixer_row_specs(rows, row_block):
    return [
        pl.BlockSpec((rows, 2 * QK_WIDTH), lambda *ids: (row_block(*ids), 0)),
        pl.BlockSpec((rows, V_WIDTH), lambda *ids: (row_block(*ids), 1)),
        pl.BlockSpec((rows, V_WIDTH), lambda *ids: (row_block(*ids), 2)),
        pl.BlockSpec((rows, LANES), lambda *ids: (row_block(*ids), 0)),
    ]


def _state_spec(n, trailing):
    nd = 1 + len(trailing)
    return pl.BlockSpec((n,) + trailing, lambda *ids: (ids[0],) + (0,) * (nd - 1))


def _mlstm_mixer(proj, gcol, bias_i, bias_f, ghead, c0, n0, m0, n_prompt_seq, prompt_len):
    m_rows = proj.shape[0]
    n_prompt = n_prompt_seq * prompt_len
    n_sample_seq = c0.shape[0]
    seg_len = (m_rows - n_prompt) // n_sample_seq
    gates = gcol[:, :2 * N_HEADS]
    tiles = lambda a, rows: jnp.transpose(a.reshape(-1, rows, 2 * N_HEADS), (0, 2, 1))
    bias = jnp.concatenate([bias_i, bias_f]).astype(F32)
    bias_row = jnp.zeros((1, LANES), F32).at[0, :2 * N_HEADS].set(bias)
    bias_col = bias.reshape(2 * N_HEADS, 1)
    ghead = ghead.reshape(1, V_WIDTH)

    def const_inputs(rows, seg):
        ccol, crow = _mlstm_constants(rows, seg)
        arrays = [bias_row, bias_col, ccol, crow, ghead]
        return arrays, [_const_spec(a.shape) for a in arrays]

    rows = PROMPT_ROWS
    chunks = prompt_len // rows
    blk = lambda b, ci: b * chunks + ci
    arrays, specs = const_inputs(rows, rows)
    y_p, c_p, n_p, m_p = pl.pallas_call(
        _mlstm_prompt_body,
        grid=(n_prompt_seq, chunks),
        in_specs=_mixer_row_specs(rows, blk)
        + [pl.BlockSpec((1, 2 * N_HEADS, rows), lambda b, ci: (blk(b, ci), 0, 0))] + specs,
        out_specs=[
            pl.BlockSpec((rows, V_WIDTH), lambda b, ci: (blk(b, ci), 0)),
            _state_spec(1, (N_HEADS, DQK, DV)), _state_spec(1, (N_HEADS, DQK)), _state_spec(1, (N_HEADS, LANES)),
        ],
        out_shape=[
            jax.ShapeDtypeStruct((n_prompt, V_WIDTH), BF16),
            jax.ShapeDtypeStruct((n_prompt_seq, N_HEADS, DQK, DV), F32),
            jax.ShapeDtypeStruct((n_prompt_seq, N_HEADS, DQK), F32),
            jax.ShapeDtypeStruct((n_prompt_seq, N_HEADS, LANES), F32),
        ],
        compiler_params=_mixer_params(2),
        name="mlstm_prompt",
    )(proj, proj, proj, gcol, tiles(gates[:n_prompt], rows), *arrays)

    rows = SAMPLE_ROWS
    spb = rows // seg_len
    first = n_prompt // rows
    sblk = lambda i: first + i
    m0b = jnp.broadcast_to(m0[:, :, None], m0.shape + (LANES,))
    arrays, specs = const_inputs(rows, seg_len)
    y_s, c_s, n_s, m_s = pl.pallas_call(
        functools.partial(_mlstm_sample_body, seg_len=seg_len),
        grid=(n_sample_seq // spb,),
        in_specs=_mixer_row_specs(rows, sblk) + [pl.BlockSpec((1, 2 * N_HEADS, rows), lambda i: (i, 0, 0))] + specs
        + [_state_spec(spb, (N_HEADS, DQK, DV)), _state_spec(spb, (N_HEADS, DQK)), _state_spec(spb, (N_HEADS, LANES))],
        out_specs=[
            pl.BlockSpec((rows, V_WIDTH), lambda i: (i, 0)),
            _state_spec(spb, (N_HEADS, DQK, DV)), _state_spec(spb, (N_HEADS, DQK)), _state_spec(spb, (N_HEADS, LANES)),
        ],
        out_shape=[
            jax.ShapeDtypeStruct((m_rows - n_prompt, V_WIDTH), BF16),
            jax.ShapeDtypeStruct(c0.shape, F32),
            jax.ShapeDtypeStruct(n0.shape, F32),
            jax.ShapeDtypeStruct(m0b.shape, F32),
        ],
        compiler_params=_mixer_params(1),
        name="mlstm_sample",
    )(proj, proj, proj, gcol, tiles(gates[n_prompt:], rows), *arrays, c0, n0, m0b)

    return (y_p, y_s), (c_p, n_p, m_p[:, :, 0]), (c_s, n_s, m_s[:, :, 0])


def _gla_mixer(proj, glr, w_gate2, b_gate, ghead, s0, n_prompt_seq, prompt_len):
    m_rows = proj.shape[0]
    n_prompt = n_prompt_seq * prompt_len
    n_sample_seq = s0.shape[0]
    seg_len = (m_rows - n_prompt) // n_sample_seq
    w2 = jnp.zeros((LANES, QK_WIDTH), BF16).at[:GATE_RANK].set(w_gate2.astype(BF16))
    bg = b_gate.reshape(1, QK_WIDTH).astype(F32)
    ghead = ghead.reshape(1, V_WIDTH)

    def const_inputs(rows, seg):
        cum, level_id, segind = _gla_constants(rows, seg)
        arrays = [w2, bg, cum, level_id, segind, ghead]
        return arrays, [_const_spec(a.shape) for a in arrays]

    rows = PROMPT_ROWS
    chunks = prompt_len // rows
    blk = lambda b, ci: b * chunks + ci
    arrays, specs = const_inputs(rows, rows)
    y_p, s_p = pl.pallas_call(
        _gla_prompt_body,
        grid=(n_prompt_seq, chunks),
        in_specs=_mixer_row_specs(rows, blk) + specs,
        out_specs=[
            pl.BlockSpec((rows, V_WIDTH), lambda b, ci: (blk(b, ci), 0)),
            _state_spec(1, (N_HEADS, DQK, DV)),
        ],
        out_shape=[
            jax.ShapeDtypeStruct((n_prompt, V_WIDTH), BF16),
            jax.ShapeDtypeStruct((n_prompt_seq, N_HEADS, DQK, DV), F32),
        ],
        compiler_params=_mixer_params(2),
        name="gla_prompt",
    )(proj, proj, proj, glr, *arrays)

    rows = SAMPLE_ROWS
    spb = rows // seg_len
    first = n_prompt // rows
    sblk = lambda i: first + i
    arrays, specs = const_inputs(rows, seg_len)
    y_s, s_s = pl.pallas_call(
        functools.partial(_gla_sample_body, seg_len=seg_len),
        grid=(n_sample_seq // spb,),
        in_specs=_mixer_row_specs(rows, sblk) + specs + [_state_spec(spb, (N_HEADS, DQK, DV))],
        out_specs=[
            pl.BlockSpec((rows, V_WIDTH), lambda i: (i, 0)),
            _state_spec(spb, (N_HEADS, DQK, DV)),
        ],
        out_shape=[
            jax.ShapeDtypeStruct((m_rows - n_prompt, V_WIDTH), BF16),
            jax.ShapeDtypeStruct(s0.shape, F32),
        ],
        compiler_params=_mixer_params(1),
        name="gla_sample",
    )(proj, proj, proj, glr, *arrays, s0)

    return (y_p, y_s), s_p, s_s


def _pad_gate_columns(w):
    return jnp.zeros((D_MODEL, LANES), BF16).at[:, :w.shape[1]].set(w.astype(BF16))


def kernel(x_prompt, x_sample, state_mlstm_C, state_mlstm_n, state_mlstm_m, state_gla_S, g_pre_mix, g_post_mix, g_pre_ffn, g_post_ffn, w_in_mlstm, b_i_mlstm, b_f_mlstm, g_head_mlstm, w_out_mlstm, w_in_gla, w_gate2_gla, b_gate_gla, g_head_gla, w_out_gla, w_ffn_gate, w_ffn_up, w_ffn_down):
    bp, sp, d = x_prompt.shape
    bs, ss, _ = x_sample.shape
    depth = g_pre_mix.shape[0]
    assert d == D_MODEL and sp % PROMPT_ROWS == 0 and SAMPLE_ROWS % ss == 0 and (bs * ss) % SAMPLE_ROWS == 0
    assert (bp * sp) % TM == 0 and (bs * ss) % TM == 0

    xp = x_prompt.reshape(bp * sp, d)
    xs = x_sample.reshape(bs * ss, d)
    vec = lambda g: g.reshape(1, D_MODEL).astype(F32)

    prompt_states = {"C": [], "n": [], "m": [], "S": []}
    sample_states = {"C": [], "n": [], "m": [], "S": []}
    for layer in range(depth):
        j = layer // 2
        if layer % 2 == 0:
            w_in = w_in_mlstm[j]
            proj, gcol = _inproj(xp, xs, vec(g_pre_mix[layer]), w_in[:, :MAIN_WIDTH].astype(BF16),
                                 _pad_gate_columns(w_in[:, MAIN_WIDTH:]))
            y, st_p, st_s = _mlstm_mixer(proj, gcol, b_i_mlstm[j], b_f_mlstm[j], g_head_mlstm[j],
                                         state_mlstm_C[j], state_mlstm_n[j], state_mlstm_m[j], bp, sp)
            for dst, st in ((prompt_states, st_p), (sample_states, st_s)):
                dst["C"].append(st[0]); dst["n"].append(st[1]); dst["m"].append(st[2])
            w_out = w_out_mlstm[j]
        else:
            w_in = w_in_gla[j]
            proj, glr = _inproj(xp, xs, vec(g_pre_mix[layer]), w_in[:, :MAIN_WIDTH].astype(BF16),
                                _pad_gate_columns(w_in[:, MAIN_WIDTH:]))
            y, s_p, s_s = _gla_mixer(proj, glr, w_gate2_gla[j], b_gate_gla[j], g_head_gla[j],
                                     state_gla_S[j], bp, sp)
            prompt_states["S"].append(s_p)
            sample_states["S"].append(s_s)
            w_out = w_out_gla[j]
        xp, xs = _block_tail(y[0], y[1], w_out.astype(BF16), xp, xs, vec(g_post_mix[layer]), vec(g_pre_ffn[layer]),
                             vec(g_post_ffn[layer]), w_ffn_gate[layer].astype(BF16), w_ffn_up[layer].astype(BF16),
                             w_ffn_down[layer].astype(BF16))

    stack = lambda xs_: jnp.stack(xs_)
    return (xp.reshape(bp, sp, d), xs.reshape(bs, ss, d),
            stack(prompt_states["C"]), stack(prompt_states["n"]), stack(prompt_states["m"]), stack(prompt_states["S"]),
            stack(sample_states["C"]), stack(sample_states["n"]), stack(sample_states["m"]), stack(sample_states["S"]))
```

```python
import functools

import numpy as np
import jax
import jax.numpy as jnp
from jax import lax
from jax.experimental import pallas as pl
from jax.experimental.pallas import tpu as pltpu

F32 = jnp.float32
BF16 = jnp.bfloat16

D_MODEL = 1024
N_HEADS = 4
DQK = 128
DV = 256
QK_WIDTH = N_HEADS * DQK
V_WIDTH = N_HEADS * DV
MAIN_WIDTH = 2 * QK_WIDTH + 2 * V_WIDTH
GATE_RANK = 16
GATE_TAU = 16.0
EPS = 1e-6
LANES = 128
SUBLANES = 8
MXU_DIM = 256
VMEM_LIMIT = 56 * 1024 * 1024

PROMPT_ROWS = 128
SAMPLE_ROWS = 64
LEVELS = (64, 32, 16, 8, 4, 2, 1)
TM = 512
FF_SPLITS = 2
ROW_GROUPS = 2


def _dot(a, b):
    return jnp.dot(a.astype(BF16), b.astype(BF16), preferred_element_type=F32)


def _dot_nt(a, b):
    return lax.dot_general(a.astype(BF16), b.astype(BF16), (((1,), (1,)), ((), ())),
                           preferred_element_type=F32)


def _dot_tn(a, b):
    return lax.dot_general(a.astype(BF16), b.astype(BF16), (((0,), (0,)), ((), ())),
                           preferred_element_type=F32)


def _split_hi_lo(x):
    hi = x.astype(BF16)
    lo = (x - hi.astype(F32)).astype(BF16)
    return hi, lo


def _log_sigmoid(x):
    return jnp.minimum(x, 0.0) - jnp.log(1.0 + jnp.exp(-jnp.abs(x)))


def _sigmoid(x):
    return 1.0 / (1.0 + jnp.exp(-x))


def _rms(x, gain):
    return x * lax.rsqrt(jnp.mean(x * x, axis=-1, keepdims=True) + EPS) * gain


def _rows_per_segment(values, seg_len):
    parts = [jnp.broadcast_to(v, (seg_len, v.shape[1])) for v in values]
    return parts[0] if len(parts) == 1 else jnp.concatenate(parts, axis=0)


def _const_spec(shape):
    nd = len(shape)
    return pl.BlockSpec(shape, lambda *_: (0,) * nd)


def _two_group_specs(n_prompt_blocks, width, single_buffer_sample=False):
    prompt = pl.BlockSpec((TM, width), lambda i: (jnp.minimum(i, n_prompt_blocks - 1), 0))
    mode = dict(pipeline_mode=pl.Buffered(1)) if single_buffer_sample else {}
    sample = pl.BlockSpec((TM, width), lambda i: (jnp.maximum(i - n_prompt_blocks, 0), 0), **mode)
    return prompt, sample


def _resident_spec(shape, layer=None):
    nd = len(shape)
    if layer is None:
        return pl.BlockSpec(shape, lambda i: (0,) * nd, pipeline_mode=pl.Buffered(1))
    return pl.BlockSpec((None,) + tuple(shape), lambda i: (layer,) + (0,) * nd, pipeline_mode=pl.Buffered(1))


def _inproj_body(xp_ref, xs_ref, gain_ref, w_ref, wg_ref, scale_ref, oqk_ref, ov_ref, ogate_ref, og_ref,
                 *, n_prompt_blocks, silu_gate):
    is_prompt = pl.program_id(0) < n_prompt_blocks
    group_rows = TM // ROW_GROUPS
    groups = [slice(a * group_rows, (a + 1) * group_rows) for a in range(ROW_GROUPS)]
    h = [None] * ROW_GROUPS
    for a in range(ROW_GROUPS + 1):
        if a < ROW_GROUPS:
            x = jnp.where(is_prompt, xp_ref[groups[a], :], xs_ref[groups[a], :])
            h[a] = _rms(x, gain_ref[...]).astype(BF16)
        if a > 0:
            rows, hb = groups[a - 1], h[a - 1]
            og_ref[rows, :] = jnp.dot(hb, wg_ref[...], preferred_element_type=F32)
            qk = jnp.dot(hb, w_ref[:, 0:2 * QK_WIDTH], preferred_element_type=F32)
            oqk_ref[rows, :] = qk * scale_ref[...]
            v = jnp.dot(hb, w_ref[:, 2 * QK_WIDTH:2 * QK_WIDTH + V_WIDTH], preferred_element_type=F32)
            ov_ref[rows, :] = v.astype(BF16)
            gate = jnp.dot(hb, w_ref[:, 2 * QK_WIDTH + V_WIDTH:MAIN_WIDTH], preferred_element_type=F32)
            ogate_ref[rows, :] = gate * _sigmoid(gate) if silu_gate else _sigmoid(gate)


def _inproj(xp, xs, gain, w_main, w_gate, qk_scale, silu_gate):
    npb = xp.shape[0] // TM
    m = xp.shape[0] + xs.shape[0]
    spec_p, spec_s = _two_group_specs(npb, D_MODEL)
    rows_spec = lambda width: pl.BlockSpec((TM, width), lambda i: (i, 0))
    return pl.pallas_call(
        functools.partial(_inproj_body, n_prompt_blocks=npb, silu_gate=silu_gate),
        grid=(m // TM,),
        in_specs=[spec_p, spec_s, _resident_spec((1, D_MODEL)), _resident_spec((D_MODEL, MAIN_WIDTH)),
                  _resident_spec((D_MODEL, LANES)), _resident_spec((1, 2 * QK_WIDTH))],
        out_specs=[rows_spec(2 * QK_WIDTH), rows_spec(V_WIDTH), rows_spec(V_WIDTH), rows_spec(LANES)],
        out_shape=[
            jax.ShapeDtypeStruct((m, 2 * QK_WIDTH), F32),
            jax.ShapeDtypeStruct((m, V_WIDTH), BF16),
            jax.ShapeDtypeStruct((m, V_WIDTH), F32),
            jax.ShapeDtypeStruct((m, LANES), F32),
        ],
        compiler_params=pltpu.CompilerParams(dimension_semantics=("arbitrary",), vmem_limit_bytes=VMEM_LIMIT),
        name="inproj",
    )(xp, xs, gain, w_main, w_gate, qk_scale)


def _tail_body(hp_ref, hs_ref, gate_ref, ghead_ref, wo_ref, xp_ref, xs_ref, gpm_ref, gpf_ref, gqf_ref,
               wg_ref, wu_ref, wd_ref, op_ref, os_ref, out_scr, *, n_prompt_blocks):
    i = pl.program_id(0)
    is_prompt = i < n_prompt_blocks
    d_ff = wg_ref.shape[1]
    n_tiles = d_ff // MXU_DIM
    bounds = [MXU_DIM * ((n_tiles * t + FF_SPLITS - 1) // FF_SPLITS) for t in range(FF_SPLITS + 1)]

    def stages(rows):
        raw = jnp.where(is_prompt, hp_ref[rows, :], hs_ref[rows, :])
        gain = ghead_ref[...]
        y = jnp.concatenate(
            [gate_ref[rows, h * DV:(h + 1) * DV] * _rms(raw[:, h * DV:(h + 1) * DV], gain[:, h * DV:(h + 1) * DV])
             for h in range(N_HEADS)], axis=1).astype(BF16)
        yield
        mix = jnp.dot(y, wo_ref[...], preferred_element_type=F32)
        yield
        x1 = jnp.where(is_prompt, xp_ref[rows, :], xs_ref[rows, :]) + _rms(mix, gpm_ref[...])
        h = _rms(x1, gpf_ref[...]).astype(BF16)
        yield
        ffn = None
        for t in range(FF_SPLITS):
            cols = slice(bounds[t], bounds[t + 1])
            gate = jnp.dot(h, wg_ref[:, cols], preferred_element_type=F32)
            up = jnp.dot(h, wu_ref[:, cols], preferred_element_type=F32)
            yield
            act = (gate * _sigmoid(gate) * up).astype(BF16)
            yield
            part = jnp.dot(act, wd_ref[cols, :], preferred_element_type=F32)
            ffn = part if ffn is None else ffn + part
            yield
        out_scr[rows, :] = x1 + _rms(ffn, gqf_ref[...])
        yield

    group_rows = TM // ROW_GROUPS
    chains = [stages(slice(a * group_rows, (a + 1) * group_rows)) for a in range(ROW_GROUPS)]
    live = []
    while chains or live:
        if chains:
            live.append(chains.pop(0))
        for chain in list(live):
            if next(chain, StopIteration) is StopIteration:
                live.remove(chain)

    @pl.when(is_prompt)
    def _():
        op_ref[...] = out_scr[...]

    @pl.when(jnp.logical_not(is_prompt))
    def _():
        os_ref[...] = out_scr[...]


def _block_tail(hp, hs, gate, ghead, w_out, xp, xs, g_post_mix, g_pre_ffn, g_post_ffn, w_gate, w_up, w_down, layer):
    npb = xp.shape[0] // TM
    m = xp.shape[0] + xs.shape[0]
    d_ff = w_gate.shape[2]
    assert d_ff % MXU_DIM == 0
    in_p, in_s = _two_group_specs(npb, D_MODEL, single_buffer_sample=True)
    vec = _resident_spec((1, D_MODEL))
    return pl.pallas_call(
        functools.partial(_tail_body, n_prompt_blocks=npb),
        grid=(m // TM,),
        in_specs=[
            in_p, in_s,
            pl.BlockSpec((TM, V_WIDTH), lambda i: (i, 0)),
            _resident_spec((1, V_WIDTH)),
            _resident_spec((V_WIDTH, D_MODEL)),
            in_p, in_s,
            vec, vec, vec,
            _resident_spec((D_MODEL, d_ff), layer), _resident_spec((D_MODEL, d_ff), layer),
            _resident_spec((d_ff, D_MODEL), layer),
        ],
        out_specs=list(_two_group_specs(npb, D_MODEL)),
        out_shape=[jax.ShapeDtypeStruct(xp.shape, F32), jax.ShapeDtypeStruct(xs.shape, F32)],
        scratch_shapes=[pltpu.VMEM((TM, D_MODEL), F32)],
        compiler_params=pltpu.CompilerParams(dimension_semantics=("arbitrary",), vmem_limit_bytes=VMEM_LIMIT),
        name="block_tail",
    )(hp, hs, gate, ghead, w_out, xp, xs, g_post_mix, g_pre_ffn, g_post_ffn, w_gate, w_up, w_down)


def _segment_structure(rows, seg_len):
    t = np.arange(rows)
    seg = t // seg_len
    same = seg[:, None] == seg[None, :]
    cum = same & (t[None, :] <= t[:, None])
    return t, seg, same, cum


def _mlstm_constants(rows, seg_len):
    _, _, same, cum = _segment_structure(rows, seg_len)
    col = np.concatenate([cum, same], axis=0).astype(np.float32)
    row = cum.T.astype(np.float32)
    return jnp.asarray(col, BF16), jnp.asarray(row, BF16)


def _mlstm_rows(qk_ref, v_ref, gcol_ref, grow_ref, bias_row_ref, bias_col_ref, ccol_ref, crow_ref,
                c_in, n_in, m_in, c_out, n_out, m_out, h_ref, rows, seg_len):
    nseg = rows // seg_len
    r = lax.broadcasted_iota(jnp.int32, (rows, rows), 0)
    c = lax.broadcasted_iota(jnp.int32, (rows, rows), 1)
    same = None if nseg == 1 else (r // seg_len) == (c // seg_len)
    lower = (c <= r) if nseg == 1 else same & (c <= r)
    row_seg = lax.broadcasted_iota(jnp.int32, (rows, DQK), 0) // seg_len

    pre_col = gcol_ref[...] + bias_row_ref[...]
    pre_row = grow_ref[0] + bias_col_ref[...]
    ccol = ccol_ref[...]
    crow = crow_ref[...]
    hi, lo = _split_hi_lo(_log_sigmoid(pre_col))
    sums_col = jnp.dot(ccol, hi, preferred_element_type=F32) + jnp.dot(ccol, lo, preferred_element_type=F32)
    hi, lo = _split_hi_lo(_log_sigmoid(pre_row))
    sums_row = jnp.dot(hi, crow, preferred_element_type=F32) + jnp.dot(lo, crow, preferred_element_type=F32)

    qk = qk_ref[...]
    c_prev = [[c_in[s, h] for h in range(N_HEADS)] for s in range(nseg)]
    n_state = [[n_in[s, h:h + 1, :] for h in range(N_HEADS)] for s in range(nseg)]
    m_state = [[m_in[s, h:h + 1, 0:1] for h in range(N_HEADS)] for s in range(nseg)]
    writes = []

    heads = range(N_HEADS)
    q = [qk[:, h * DQK:(h + 1) * DQK] for h in heads]
    k = [qk[:, QK_WIDTH + h * DQK:QK_WIDTH + (h + 1) * DQK] for h in heads]
    v_bf = [v_ref[:, h * DV:(h + 1) * DV] for h in heads]
    i_col = [pre_col[:, h:h + 1] for h in heads]
    b_col = [sums_col[0:rows, N_HEADS + h:N_HEADS + h + 1] for h in heads]
    b_last = [sums_col[rows:2 * rows, N_HEADS + h:N_HEADS + h + 1] for h in heads]
    src = [pre_row[h:h + 1, :] - sums_row[N_HEADS + h:N_HEADS + h + 1, :] for h in heads]
    m_prev = [_rows_per_segment([m_state[s][h] for s in range(nseg)], seg_len) for h in heads]
    n_prev = [_rows_per_segment([n_state[s][h] for s in range(nseg)], seg_len) for h in heads]

    dmat = [jnp.where(lower, b_col[h] + src[h], -jnp.inf) for h in heads]
    inter = [b_col[h] + m_prev[h] for h in heads]
    m_t = [jnp.maximum(inter[h], jnp.max(dmat[h], axis=1, keepdims=True)) for h in heads]
    dlast = [b_last[h] + src[h] for h in heads]
    if nseg > 1:
        dlast = [jnp.where(same, d, -jnp.inf) for d in dlast]
    m_new = [jnp.maximum(b_last[h] + m_prev[h], jnp.max(dlast[h], axis=1, keepdims=True)) for h in heads]
    scores = [_dot_nt(q[h], k[h]) for h in heads]
    q_bf = [q[h].astype(BF16) for h in heads]
    qc = []
    for h in heads:
        parts = []
        for s in range(nseg):
            full = jnp.dot(q_bf[h], c_prev[s][h].astype(BF16), preferred_element_type=F32)
            parts.append(full[s * seg_len:(s + 1) * seg_len] if nseg > 1 else full)
        qc.append(parts[0] if nseg == 1 else jnp.concatenate(parts, axis=0))
    qn = [jnp.sum(q[h] * n_prev[h], axis=1, keepdims=True) for h in heads]

    w_inter = [jnp.exp(inter[h] - m_t[h]) for h in heads]
    a = [scores[h] * jnp.exp(dmat[h] - m_t[h]) for h in heads]
    w_s = [jnp.exp(b_last[h] - b_col[h] + i_col[h] - m_new[h]) for h in heads]
    decay = [jnp.exp(b_last[h] + m_prev[h] - m_new[h]) for h in heads]
    kw = [k[h] * w_s[h] for h in heads]

    num = [jnp.dot(a[h].astype(BF16), v_bf[h], preferred_element_type=F32) + w_inter[h] * qc[h] for h in heads]
    den = [jnp.sum(a[h], axis=1, keepdims=True) + w_inter[h] * qn[h] for h in heads]
    for h in heads:
        for s in range(nseg):
            kw_s = kw[h] if nseg == 1 else jnp.where(row_seg == s, kw[h], 0.0)
            dec = decay[h][s * seg_len:s * seg_len + 1, :]
            writes.append((s, h, dec * c_prev[s][h] + _dot_tn(kw_s, v_bf[h]),
                           dec * n_state[s][h] + jnp.sum(kw_s, axis=0, keepdims=True),
                           jnp.broadcast_to(m_new[h][s * seg_len:s * seg_len + 1, :], (1, LANES))))

    for h in heads:
        h_ref[:, h * DV:(h + 1) * DV] = num[h] * (1.0 / jnp.maximum(jnp.abs(den[h]), jnp.exp(-m_t[h])))

    for s, h, c_new, n_new, m_new in writes:
        c_out[s, h] = c_new
        n_out[s, h:h + 1, :] = n_new
        m_out[s, h:h + 1, :] = m_new


def _mlstm_prompt_body(qk_ref, v_ref, gcol_ref, grow_ref, brow_ref, bcol_ref, ccol_ref, crow_ref,
                       h_ref, c_ref, n_ref, m_ref):
    @pl.when(pl.program_id(1) == 0)
    def _():
        c_ref[...] = jnp.zeros_like(c_ref)
        n_ref[...] = jnp.zeros_like(n_ref)
        m_ref[...] = jnp.zeros_like(m_ref)

    _mlstm_rows(qk_ref, v_ref, gcol_ref, grow_ref, brow_ref, bcol_ref, ccol_ref, crow_ref,
                c_ref, n_ref, m_ref, c_ref, n_ref, m_ref, h_ref, PROMPT_ROWS, PROMPT_ROWS)


def _mlstm_sample_body(qk_ref, v_ref, gcol_ref, grow_ref, brow_ref, bcol_ref, ccol_ref, crow_ref,
                       c_in, n_in, m_in, h_ref, c_out, n_out, m_out, *, seg_len):
    _mlstm_rows(qk_ref, v_ref, gcol_ref, grow_ref, brow_ref, bcol_ref, ccol_ref, crow_ref,
                c_in, n_in, m_in, c_out, n_out, m_out, h_ref, SAMPLE_ROWS, seg_len)


def _gla_constants(rows, seg_len):
    t, seg, _, cum = _segment_structure(rows, seg_len)
    level_id = np.full((rows, rows), -1, np.int32)
    level_id[t, t] = 0
    for idx, hb in enumerate(LEVELS):
        if 2 * hb > seg_len:
            continue
        block = t // (2 * hb)
        upper = (t % (2 * hb)) >= hb
        level_id[(block[:, None] == block[None, :]) & upper[:, None] & ~upper[None, :]] = idx + 1
    segind = (seg[:, None] == np.arange(LANES)[None, :]).astype(np.float32)
    return jnp.asarray(cum.astype(np.float32), BF16), jnp.asarray(level_id), jnp.asarray(segind, BF16)


def _row_bcast(x, row, n):
    return jnp.broadcast_to(x[row:row + 1, :], (n, x.shape[1]))


def _gla_rows(qk_ref, v_ref, glr_ref, w2_ref, bg_ref, cum_ref, lvl_ref, segind_ref,
              s_in, s_out, o_ref, rows, seg_len):
    nseg = rows // seg_len
    ntile = rows // SUBLANES
    row_id = lax.broadcasted_iota(jnp.int32, (rows, DQK), 0)
    row_seg = row_id // seg_len
    row_in_tile = row_id % SUBLANES

    pre = _dot(glr_ref[...], w2_ref[...]) + bg_ref[...]
    g_all = _log_sigmoid(pre) * (1.0 / GATE_TAU)
    g_hi, g_lo = _split_hi_lo(g_all)
    cum = cum_ref[...]
    bc_all = jnp.dot(cum, g_hi, preferred_element_type=F32) + jnp.dot(cum, g_lo, preferred_element_type=F32)
    level_tiles = [lvl_ref[t * SUBLANES:(t + 1) * SUBLANES, :] for t in range(ntile)]
    segind = segind_ref[...]
    total_col = (lax.dot_general(g_hi, segind, (((0,), (0,)), ((), ())), preferred_element_type=F32)
                 + lax.dot_general(g_lo, segind, (((0,), (0,)), ((), ())), preferred_element_type=F32))

    qk = qk_ref[...]
    s_prev = [[s_in[s, h] for h in range(N_HEADS)] for s in range(nseg)]
    writes = []

    heads = range(N_HEADS)
    sl = [slice(h * DQK, (h + 1) * DQK) for h in heads]
    q = [qk[:, sl[h]] for h in heads]
    k = [qk[:, QK_WIDTH + h * DQK:QK_WIDTH + (h + 1) * DQK] for h in heads]
    v_bf = [v_ref[:, h * DV:(h + 1) * DV] for h in heads]
    g = [g_all[:, sl[h]] for h in heads]
    bc = [bc_all[:, sl[h]] for h in heads]
    seg_total = [jnp.concatenate([_row_bcast(bc[h], (s + 1) * seg_len - 1, seg_len) for s in range(nseg)], axis=0)
                 if nseg > 1 else _row_bcast(bc[h], rows - 1, rows) for h in heads]

    qq = [(q[h] * jnp.exp(bc[h])).astype(BF16) for h in heads]
    kk = [k[h] * jnp.exp(seg_total[h] - bc[h]) for h in heads]
    o_inter = []
    for h in heads:
        parts = []
        for s in range(nseg):
            st = s_prev[s][h]
            oi = jnp.dot(qq[h], st.astype(BF16), preferred_element_type=F32)
            parts.append(oi[s * seg_len:(s + 1) * seg_len] if nseg > 1 else oi)
            kk_s = kk[h] if nseg == 1 else jnp.where(row_seg == s, kk[h], 0.0)
            writes.append((s, h, st * jnp.exp(total_col[sl[h], s:s + 1]) + _dot_tn(kk_s, v_bf[h])))
        o_inter.append(parts[0] if nseg == 1 else jnp.concatenate(parts, axis=0))

    a = [[None] * ntile for h in heads]

    def put(h, t, level, z_rows):
        prev = 0.0 if a[h][t] is None else a[h][t]
        a[h][t] = jnp.where(level_tiles[t] == level, z_rows, prev)

    for h in heads:
        z = _dot_nt(q[h], k[h])
        for t in range(ntile):
            put(h, t, 0, z[t * SUBLANES:(t + 1) * SUBLANES])

    for idx, hb in enumerate(LEVELS):
        if 2 * hb > seg_len:
            continue
        level = idx + 1
        if hb >= SUBLANES:
            starts = range(0, rows, 2 * hb)
            for h in heads:
                e_parts, x_parts = [], []
                for r0 in starts:
                    ref = _row_bcast(bc[h], r0 + hb - 1, hb)
                    e_parts += [ref - bc[h][r0:r0 + hb], bc[h][r0 + hb:r0 + 2 * hb] - ref]
                    x_parts += [k[h][r0:r0 + hb], q[h][r0 + hb:r0 + 2 * hb]]
                u = jnp.concatenate(x_parts, axis=0) * jnp.exp(jnp.concatenate(e_parts, axis=0))
                u_upper = jnp.concatenate([u[r0 + hb:r0 + 2 * hb] for r0 in starts], axis=0)
                z = _dot_nt(u_upper, u)
                upper_tiles = [t for r0 in starts for t in range((r0 + hb) // SUBLANES, (r0 + 2 * hb) // SUBLANES)]
                for i, t in enumerate(upper_tiles):
                    put(h, t, level, z[i * SUBLANES:(i + 1) * SUBLANES])
        else:
            upper = (row_id % (2 * hb)) >= hb
            for h in heads:
                if hb == 1:
                    e_l = jnp.where(upper, g[h], 0.0)
                else:
                    tiles = range(ntile)
                    if hb == 4:
                        ref = jnp.concatenate([_row_bcast(bc[h], t * SUBLANES + 3, SUBLANES) for t in tiles], axis=0)
                    else:
                        lo = jnp.concatenate([_row_bcast(bc[h], t * SUBLANES + 1, SUBLANES) for t in tiles], axis=0)
                        hi = jnp.concatenate([_row_bcast(bc[h], t * SUBLANES + 5, SUBLANES) for t in tiles], axis=0)
                        ref = jnp.where(row_in_tile < 4, lo, hi)
                    d = bc[h] - ref
                    e_l = jnp.where(upper, d, -d)
                u = jnp.where(upper, q[h], k[h]) * jnp.exp(e_l)
                z = _dot_nt(u, u)
                for t in range(ntile):
                    put(h, t, level, z[t * SUBLANES:(t + 1) * SUBLANES])

    for h in heads:
        o_ref[:, h * DV:(h + 1) * DV] = jnp.dot(jnp.concatenate(a[h], axis=0).astype(BF16), v_bf[h],
                                                preferred_element_type=F32) + o_inter[h]

    for s, h, s_new in writes:
        s_out[s, h] = s_new


def _gla_prompt_body(qk_ref, v_ref, glr_ref, w2_ref, bg_ref, cum_ref, lvl_ref, segind_ref, o_ref, s_ref):
    @pl.when(pl.program_id(1) == 0)
    def _():
        s_ref[...] = jnp.zeros_like(s_ref)

    _gla_rows(qk_ref, v_ref, glr_ref, w2_ref, bg_ref, cum_ref, lvl_ref, segind_ref,
              s_ref, s_ref, o_ref, PROMPT_ROWS, PROMPT_ROWS)


def _gla_sample_body(qk_ref, v_ref, glr_ref, w2_ref, bg_ref, cum_ref, lvl_ref, segind_ref,
                     s_in, o_ref, s_out, *, seg_len):
    _gla_rows(qk_ref, v_ref, glr_ref, w2_ref, bg_ref, cum_ref, lvl_ref, segind_ref,
              s_in, s_out, o_ref, SAMPLE_ROWS, seg_len)


def _mixer_params(n_axes):
    return pltpu.CompilerParams(dimension_semantics=("arbitrary",) * n_axes, vmem_limit_bytes=VMEM_LIMIT)


def _mixer_row_specs(rows, row_block):
    return [
        pl.BlockSpec((rows, 2 * QK_WIDTH), lambda *ids: (row_block(*ids), 0)),
        pl.BlockSpec((rows, V_WIDTH), lambda *ids: (row_block(*ids), 0)),
        pl.BlockSpec((rows, LANES), lambda *ids: (row_block(*ids), 0)),
    ]


def _state_spec(n, trailing):
    nd = 1 + len(trailing)
    return pl.BlockSpec((n,) + trailing, lambda *ids: (ids[0],) + (0,) * (nd - 1))


def _mlstm_mixer(qk, v, gcol, bias_i, bias_f, c0, n0, m0, n_prompt_seq, prompt_len):
    m_rows = qk.shape[0]
    n_prompt = n_prompt_seq * prompt_len
    n_sample_seq = c0.shape[0]
    seg_len = (m_rows - n_prompt) // n_sample_seq
    gates = gcol[:, :2 * N_HEADS]
    tiles = lambda a, rows: jnp.transpose(a.reshape(-1, rows, 2 * N_HEADS), (0, 2, 1))
    bias = jnp.concatenate([bias_i, bias_f]).astype(F32)
    bias_row = jnp.zeros((1, LANES), F32).at[0, :2 * N_HEADS].set(bias)
    bias_col = bias.reshape(2 * N_HEADS, 1)

    def const_inputs(rows, seg):
        ccol, crow = _mlstm_constants(rows, seg)
        arrays = [bias_row, bias_col, ccol, crow]
        return arrays, [_const_spec(a.shape) for a in arrays]

    rows = PROMPT_ROWS
    chunks = prompt_len // rows
    blk = lambda b, ci: b * chunks + ci
    arrays, specs = const_inputs(rows, rows)
    y_p, c_p, n_p, m_p = pl.pallas_call(
        _mlstm_prompt_body,
        grid=(n_prompt_seq, chunks),
        in_specs=_mixer_row_specs(rows, blk)
        + [pl.BlockSpec((1, 2 * N_HEADS, rows), lambda b, ci: (blk(b, ci), 0, 0))] + specs,
        out_specs=[
            pl.BlockSpec((rows, V_WIDTH), lambda b, ci: (blk(b, ci), 0)),
            _state_spec(1, (N_HEADS, DQK, DV)), _state_spec(1, (N_HEADS, DQK)), _state_spec(1, (N_HEADS, LANES)),
        ],
        out_shape=[
            jax.ShapeDtypeStruct((n_prompt, V_WIDTH), F32),
            jax.ShapeDtypeStruct((n_prompt_seq, N_HEADS, DQK, DV), F32),
            jax.ShapeDtypeStruct((n_prompt_seq, N_HEADS, DQK), F32),
            jax.ShapeDtypeStruct((n_prompt_seq, N_HEADS, LANES), F32),
        ],
        compiler_params=_mixer_params(2),
        name="mlstm_prompt",
    )(qk, v, gcol, tiles(gates[:n_prompt], rows), *arrays)

    rows = SAMPLE_ROWS
    spb = rows // seg_len
    first = n_prompt // rows
    sblk = lambda i: first + i
    m0b = jnp.broadcast_to(m0[:, :, None], m0.shape + (LANES,))
    arrays, specs = const_inputs(rows, seg_len)
    y_s, c_s, n_s, m_s = pl.pallas_call(
        functools.partial(_mlstm_sample_body, seg_len=seg_len),
        grid=(n_sample_seq // spb,),
        in_specs=_mixer_row_specs(rows, sblk) + [pl.BlockSpec((1, 2 * N_HEADS, rows), lambda i: (i, 0, 0))] + specs
        + [_state_spec(spb, (N_HEADS, DQK, DV)), _state_spec(spb, (N_HEADS, DQK)), _state_spec(spb, (N_HEADS, LANES))],
        out_specs=[
            pl.BlockSpec((rows, V_WIDTH), lambda i: (i, 0)),
            _state_spec(spb, (N_HEADS, DQK, DV)), _state_spec(spb, (N_HEADS, DQK)), _state_spec(spb, (N_HEADS, LANES)),
        ],
        out_shape=[
            jax.ShapeDtypeStruct((m_rows - n_prompt, V_WIDTH), F32),
            jax.ShapeDtypeStruct(c0.shape, F32),
            jax.ShapeDtypeStruct(n0.shape, F32),
            jax.ShapeDtypeStruct(m0b.shape, F32),
        ],
        compiler_params=_mixer_params(1),
        name="mlstm_sample",
    )(qk, v, gcol, tiles(gates[n_prompt:], rows), *arrays, c0, n0, m0b)

    return (y_p, y_s), (c_p, n_p, m_p[:, :, 0]), (c_s, n_s, m_s[:, :, 0])


def _gla_mixer(qk, v, glr, w_gate2, b_gate, s0, n_prompt_seq, prompt_len):
    m_rows = qk.shape[0]
    n_prompt = n_prompt_seq * prompt_len
    n_sample_seq = s0.shape[0]
    seg_len = (m_rows - n_prompt) // n_sample_seq
    w2 = jnp.zeros((LANES, QK_WIDTH), BF16).at[:GATE_RANK].set(w_gate2.astype(BF16))
    bg = b_gate.reshape(1, QK_WIDTH).astype(F32)

    def const_inputs(rows, seg):
        cum, level_id, segind = _gla_constants(rows, seg)
        arrays = [w2, bg, cum, level_id, segind]
        return arrays, [_const_spec(a.shape) for a in arrays]

    rows = PROMPT_ROWS
    chunks = prompt_len // rows
    blk = lambda b, ci: b * chunks + ci
    arrays, specs = const_inputs(rows, rows)
    y_p, s_p = pl.pallas_call(
        _gla_prompt_body,
        grid=(n_prompt_seq, chunks),
        in_specs=_mixer_row_specs(rows, blk) + specs,
        out_specs=[
            pl.BlockSpec((rows, V_WIDTH), lambda b, ci: (blk(b, ci), 0)),
            _state_spec(1, (N_HEADS, DQK, DV)),
        ],
        out_shape=[
            jax.ShapeDtypeStruct((n_prompt, V_WIDTH), F32),
            jax.ShapeDtypeStruct((n_prompt_seq, N_HEADS, DQK, DV), F32),
        ],
        compiler_params=_mixer_params(2),
        name="gla_prompt",
    )(qk, v, glr, *arrays)

    rows = SAMPLE_ROWS
    spb = rows // seg_len
    first = n_prompt // rows
    sblk = lambda i: first + i
    arrays, specs = const_inputs(rows, seg_len)
    y_s, s_s = pl.pallas_call(
        functools.partial(_gla_sample_body, seg_len=seg_len),
        grid=(n_sample_seq // spb,),
        in_specs=_mixer_row_specs(rows, sblk) + specs + [_state_spec(spb, (N_HEADS, DQK, DV))],
        out_specs=[
            pl.BlockSpec((rows, V_WIDTH), lambda i: (i, 0)),
            _state_spec(spb, (N_HEADS, DQK, DV)),
        ],
        out_shape=[
            jax.ShapeDtypeStruct((m_rows - n_prompt, V_WIDTH), F32),
            jax.ShapeDtypeStruct(s0.shape, F32),
        ],
        compiler_params=_mixer_params(1),
        name="gla_sample",
    )(qk, v, glr, *arrays, s0)

    return (y_p, y_s), s_p, s_s


def _pad_gate_columns(w):
    return jnp.zeros((D_MODEL, LANES), BF16).at[:, :w.shape[1]].set(w.astype(BF16))


def kernel(x_prompt, x_sample, state_mlstm_C, state_mlstm_n, state_mlstm_m, state_gla_S, g_pre_mix, g_post_mix, g_pre_ffn, g_post_ffn, w_in_mlstm, b_i_mlstm, b_f_mlstm, g_head_mlstm, w_out_mlstm, w_in_gla, w_gate2_gla, b_gate_gla, g_head_gla, w_out_gla, w_ffn_gate, w_ffn_up, w_ffn_down):
    bp, sp, d = x_prompt.shape
    bs, ss, _ = x_sample.shape
    depth = g_pre_mix.shape[0]
    assert d == D_MODEL and sp % PROMPT_ROWS == 0 and SAMPLE_ROWS % ss == 0 and (bs * ss) % SAMPLE_ROWS == 0
    assert (bp * sp) % TM == 0 and (bs * ss) % TM == 0

    xp = x_prompt.reshape(bp * sp, d)
    xs = x_sample.reshape(bs * ss, d)
    vec = lambda g: g.reshape(1, D_MODEL).astype(F32)

    prompt_states = {"C": [], "n": [], "m": [], "S": []}
    sample_states = {"C": [], "n": [], "m": [], "S": []}
    ffn_gate, ffn_up, ffn_down = (w.astype(BF16) for w in (w_ffn_gate, w_ffn_up, w_ffn_down))
    ones = jnp.ones((QK_WIDTH,), F32)
    head_scale = jnp.full((QK_WIDTH,), DQK ** -0.5, F32)
    for layer in range(depth):
        j = layer // 2
        if layer % 2 == 0:
            w_in = w_in_mlstm[j]
            qk, v, gate, gcol = _inproj(xp, xs, vec(g_pre_mix[layer]), w_in[:, :MAIN_WIDTH].astype(BF16),
                                        _pad_gate_columns(w_in[:, MAIN_WIDTH:]),
                                        jnp.concatenate([ones, head_scale]).reshape(1, -1), silu_gate=False)
            h, st_p, st_s = _mlstm_mixer(qk, v, gcol, b_i_mlstm[j], b_f_mlstm[j],
                                         state_mlstm_C[j], state_mlstm_n[j], state_mlstm_m[j], bp, sp)
            for dst, st in ((prompt_states, st_p), (sample_states, st_s)):
                dst["C"].append(st[0]); dst["n"].append(st[1]); dst["m"].append(st[2])
            w_out, g_head = w_out_mlstm[j], g_head_mlstm[j]
        else:
            w_in = w_in_gla[j]
            qk, v, gate, glr = _inproj(xp, xs, vec(g_pre_mix[layer]), w_in[:, :MAIN_WIDTH].astype(BF16),
                                       _pad_gate_columns(w_in[:, MAIN_WIDTH:]),
                                       jnp.concatenate([head_scale, ones]).reshape(1, -1), silu_gate=True)
            h, s_p, s_s = _gla_mixer(qk, v, glr, w_gate2_gla[j], b_gate_gla[j], state_gla_S[j], bp, sp)
            prompt_states["S"].append(s_p)
            sample_states["S"].append(s_s)
            w_out, g_head = w_out_gla[j], g_head_gla[j]
        xp, xs = _block_tail(h[0], h[1], gate, g_head.reshape(1, V_WIDTH).astype(F32), w_out.astype(BF16), xp, xs,
                             vec(g_post_mix[layer]), vec(g_pre_ffn[layer]), vec(g_post_ffn[layer]),
                             ffn_gate, ffn_up, ffn_down, layer)

    stack = lambda xs_: jnp.stack(xs_)
    return (xp.reshape(bp, sp, d), xs.reshape(bs, ss, d),
            stack(prompt_states["C"]), stack(prompt_states["n"]), stack(prompt_states["m"]), stack(prompt_states["S"]),
            stack(sample_states["C"]), stack(sample_states["n"]), stack(sample_states["m"]), stack(sample_states["S"]))
```

```python
import functools

import numpy as np
import jax
import jax.numpy as jnp
from jax import lax
from jax.experimental import pallas as pl
from jax.experimental.pallas import tpu as pltpu

F32 = jnp.float32
BF16 = jnp.bfloat16

D_MODEL = 1024
N_HEADS = 4
DQK = 128
DV = 256
QK_WIDTH = N_HEADS * DQK
V_WIDTH = N_HEADS * DV
MAIN_WIDTH = 2 * QK_WIDTH + 2 * V_WIDTH
GATE_RANK = 16
GATE_TAU = 16.0
EPS = 1e-6
LANES = 128
SUBLANES = 8
MXU_DIM = 256
VMEM_LIMIT = 56 * 1024 * 1024

PROMPT_ROWS = 128
PROMPT_SEQS = 8
SAMPLE_ROWS = 64
LEVELS = (64, 32, 16, 8, 4, 2, 1)
TM = 512
FF_SPLITS = 2
ROW_GROUPS = 2


def _dot(a, b):
    return jnp.dot(a.astype(BF16), b.astype(BF16), preferred_element_type=F32)


def _dot_nt(a, b):
    return lax.dot_general(a.astype(BF16), b.astype(BF16), (((1,), (1,)), ((), ())),
                           preferred_element_type=F32)


def _dot_tn(a, b):
    return lax.dot_general(a.astype(BF16), b.astype(BF16), (((0,), (0,)), ((), ())),
                           preferred_element_type=F32)


def _split_hi_lo(x):
    hi = x.astype(BF16)
    lo = (x - hi.astype(F32)).astype(BF16)
    return hi, lo


def _log_sigmoid(x):
    return jnp.minimum(x, 0.0) - jnp.log(1.0 + jnp.exp(-jnp.abs(x)))


def _sigmoid(x):
    return 1.0 / (1.0 + jnp.exp(-x))


def _rms(x, gain):
    return x * lax.rsqrt(jnp.mean(x * x, axis=-1, keepdims=True) + EPS) * gain


def _rows_per_segment(values, seg_len):
    parts = [jnp.broadcast_to(v, (seg_len, v.shape[1])) for v in values]
    return parts[0] if len(parts) == 1 else jnp.concatenate(parts, axis=0)


def _emit_staggered(chains, lag=1):
    pending, live, step = list(chains), [], 0
    while pending or live:
        while pending and (lag == 0 or step % lag == 0):
            live.append(pending.pop(0))
            if lag:
                break
        step += 1
        for chain in list(live):
            if next(chain, StopIteration) is StopIteration:
                live.remove(chain)


def _const_spec(shape):
    nd = len(shape)
    return pl.BlockSpec(shape, lambda *_: (0,) * nd)


def _two_group_specs(n_prompt_blocks, width, single_buffer_sample=False):
    prompt = pl.BlockSpec((TM, width), lambda i: (jnp.minimum(i, n_prompt_blocks - 1), 0))
    mode = dict(pipeline_mode=pl.Buffered(1)) if single_buffer_sample else {}
    sample = pl.BlockSpec((TM, width), lambda i: (jnp.maximum(i - n_prompt_blocks, 0), 0), **mode)
    return prompt, sample


def _resident_spec(shape, layer=None):
    nd = len(shape)
    if layer is None:
        return pl.BlockSpec(shape, lambda i: (0,) * nd, pipeline_mode=pl.Buffered(1))
    return pl.BlockSpec((None,) + tuple(shape), lambda i: (layer,) + (0,) * nd, pipeline_mode=pl.Buffered(1))


def _inproj_body(xp_ref, xs_ref, gain_ref, w_ref, wg_ref, scale_ref, oqk_ref, ov_ref, ogate_ref, og_ref,
                 *, n_prompt_blocks, silu_gate):
    is_prompt = pl.program_id(0) < n_prompt_blocks
    group_rows = TM // ROW_GROUPS
    groups = [slice(a * group_rows, (a + 1) * group_rows) for a in range(ROW_GROUPS)]
    h = [None] * ROW_GROUPS
    for a in range(ROW_GROUPS + 1):
        if a < ROW_GROUPS:
            x = jnp.where(is_prompt, xp_ref[groups[a], :], xs_ref[groups[a], :])
            h[a] = _rms(x, gain_ref[...]).astype(BF16)
        if a > 0:
            rows, hb = groups[a - 1], h[a - 1]
            og_ref[rows, :] = jnp.dot(hb, wg_ref[...], preferred_element_type=F32)
            qk = jnp.dot(hb, w_ref[:, 0:2 * QK_WIDTH], preferred_element_type=F32)
            oqk_ref[rows, :] = qk * scale_ref[...]
            v = jnp.dot(hb, w_ref[:, 2 * QK_WIDTH:2 * QK_WIDTH + V_WIDTH], preferred_element_type=F32)
            ov_ref[rows, :] = v.astype(BF16)
            gate = jnp.dot(hb, w_ref[:, 2 * QK_WIDTH + V_WIDTH:MAIN_WIDTH], preferred_element_type=F32)
            ogate_ref[rows, :] = gate * _sigmoid(gate) if silu_gate else _sigmoid(gate)


def _inproj(xp, xs, gain, w_main, w_gate, qk_scale, silu_gate):
    npb = xp.shape[0] // TM
    m = xp.shape[0] + xs.shape[0]
    spec_p, spec_s = _two_group_specs(npb, D_MODEL)
    rows_spec = lambda width: pl.BlockSpec((TM, width), lambda i: (i, 0))
    return pl.pallas_call(
        functools.partial(_inproj_body, n_prompt_blocks=npb, silu_gate=silu_gate),
        grid=(m // TM,),
        in_specs=[spec_p, spec_s, _resident_spec((1, D_MODEL)), _resident_spec((D_MODEL, MAIN_WIDTH)),
                  _resident_spec((D_MODEL, LANES)), _resident_spec((1, 2 * QK_WIDTH))],
        out_specs=[rows_spec(2 * QK_WIDTH), rows_spec(V_WIDTH), rows_spec(V_WIDTH), rows_spec(LANES)],
        out_shape=[
            jax.ShapeDtypeStruct((m, 2 * QK_WIDTH), F32),
            jax.ShapeDtypeStruct((m, V_WIDTH), BF16),
            jax.ShapeDtypeStruct((m, V_WIDTH), F32),
            jax.ShapeDtypeStruct((m, LANES), F32),
        ],
        compiler_params=pltpu.CompilerParams(dimension_semantics=("arbitrary",), vmem_limit_bytes=VMEM_LIMIT),
        name="inproj",
    )(xp, xs, gain, w_main, w_gate, qk_scale)


def _tail_body(hp_ref, hs_ref, gate_ref, ghead_ref, wo_ref, xp_ref, xs_ref, gpm_ref, gpf_ref, gqf_ref,
               wg_ref, wu_ref, wd_ref, op_ref, os_ref, out_scr, *, n_prompt_blocks):
    i = pl.program_id(0)
    is_prompt = i < n_prompt_blocks
    d_ff = wg_ref.shape[1]
    n_tiles = d_ff // MXU_DIM
    bounds = [MXU_DIM * ((n_tiles * t + FF_SPLITS - 1) // FF_SPLITS) for t in range(FF_SPLITS + 1)]

    def stages(rows):
        raw = jnp.where(is_prompt, hp_ref[rows, :], hs_ref[rows, :])
        gain = ghead_ref[...]
        y = jnp.concatenate(
            [gate_ref[rows, h * DV:(h + 1) * DV] * _rms(raw[:, h * DV:(h + 1) * DV], gain[:, h * DV:(h + 1) * DV])
             for h in range(N_HEADS)], axis=1).astype(BF16)
        yield
        mix = jnp.dot(y, wo_ref[...], preferred_element_type=F32)
        yield
        x1 = jnp.where(is_prompt, xp_ref[rows, :], xs_ref[rows, :]) + _rms(mix, gpm_ref[...])
        h = _rms(x1, gpf_ref[...]).astype(BF16)
        yield
        ffn = None
        for t in range(FF_SPLITS):
            cols = slice(bounds[t], bounds[t + 1])
            gate = jnp.dot(h, wg_ref[:, cols], preferred_element_type=F32)
            up = jnp.dot(h, wu_ref[:, cols], preferred_element_type=F32)
            yield
            act = (gate * _sigmoid(gate) * up).astype(BF16)
            yield
            part = jnp.dot(act, wd_ref[cols, :], preferred_element_type=F32)
            ffn = part if ffn is None else ffn + part
            yield
        out_scr[rows, :] = x1 + _rms(ffn, gqf_ref[...])
        yield

    group_rows = TM // ROW_GROUPS
    _emit_staggered([stages(slice(a * group_rows, (a + 1) * group_rows)) for a in range(ROW_GROUPS)])

    @pl.when(is_prompt)
    def _():
        op_ref[...] = out_scr[...]

    @pl.when(jnp.logical_not(is_prompt))
    def _():
        os_ref[...] = out_scr[...]


def _block_tail(hp, hs, gate, ghead, w_out, xp, xs, g_post_mix, g_pre_ffn, g_post_ffn, w_gate, w_up, w_down, layer):
    npb = xp.shape[0] // TM
    m = xp.shape[0] + xs.shape[0]
    d_ff = w_gate.shape[2]
    assert d_ff % MXU_DIM == 0
    in_p, in_s = _two_group_specs(npb, D_MODEL, single_buffer_sample=True)
    vec = _resident_spec((1, D_MODEL))
    return pl.pallas_call(
        functools.partial(_tail_body, n_prompt_blocks=npb),
        grid=(m // TM,),
        in_specs=[
            in_p, in_s,
            pl.BlockSpec((TM, V_WIDTH), lambda i: (i, 0)),
            _resident_spec((1, V_WIDTH)),
            _resident_spec((V_WIDTH, D_MODEL)),
            in_p, in_s,
            vec, vec, vec,
            _resident_spec((D_MODEL, d_ff), layer), _resident_spec((D_MODEL, d_ff), layer),
            _resident_spec((d_ff, D_MODEL), layer),
        ],
        out_specs=list(_two_group_specs(npb, D_MODEL)),
        out_shape=[jax.ShapeDtypeStruct(xp.shape, F32), jax.ShapeDtypeStruct(xs.shape, F32)],
        scratch_shapes=[pltpu.VMEM((TM, D_MODEL), F32)],
        compiler_params=pltpu.CompilerParams(dimension_semantics=("arbitrary",), vmem_limit_bytes=VMEM_LIMIT),
        name="block_tail",
    )(hp, hs, gate, ghead, w_out, xp, xs, g_post_mix, g_pre_ffn, g_post_ffn, w_gate, w_up, w_down)


def _segment_structure(rows, seg_len):
    t = np.arange(rows)
    seg = t // seg_len
    same = seg[:, None] == seg[None, :]
    cum = same & (t[None, :] <= t[:, None])
    return t, seg, same, cum


def _mlstm_constants(rows, seg_len):
    _, _, same, cum = _segment_structure(rows, seg_len)
    col = np.concatenate([cum, same], axis=0).astype(np.float32)
    row = cum.T.astype(np.float32)
    return jnp.asarray(col, BF16), jnp.asarray(row, BF16)


def _mlstm_rows(qk_ref, v_ref, gcol_ref, grow_ref, bias_row_ref, bias_col_ref, ccol_ref, crow_ref,
                c_in, n_in, m_in, c_out, n_out, m_out, h_ref, rows, seg_len):
    nseg = rows // seg_len
    r = lax.broadcasted_iota(jnp.int32, (rows, rows), 0)
    c = lax.broadcasted_iota(jnp.int32, (rows, rows), 1)
    same = None if nseg == 1 else (r // seg_len) == (c // seg_len)
    lower = (c <= r) if nseg == 1 else same & (c <= r)
    row_seg = lax.broadcasted_iota(jnp.int32, (rows, DQK), 0) // seg_len

    pre_col = gcol_ref[...] + bias_row_ref[...]
    pre_row = grow_ref[0] + bias_col_ref[...]
    ccol = ccol_ref[...]
    crow = crow_ref[...]
    hi, lo = _split_hi_lo(_log_sigmoid(pre_col))
    sums_col = jnp.dot(ccol, hi, preferred_element_type=F32) + jnp.dot(ccol, lo, preferred_element_type=F32)
    hi, lo = _split_hi_lo(_log_sigmoid(pre_row))
    sums_row = jnp.dot(hi, crow, preferred_element_type=F32) + jnp.dot(lo, crow, preferred_element_type=F32)
    yield

    qk = qk_ref[...]
    c_prev = [[c_in[s, h] for h in range(N_HEADS)] for s in range(nseg)]
    n_state = [[n_in[s, h:h + 1, :] for h in range(N_HEADS)] for s in range(nseg)]
    m_state = [[m_in[s, h:h + 1, 0:1] for h in range(N_HEADS)] for s in range(nseg)]
    writes = []

    heads = range(N_HEADS)
    q = [qk[:, h * DQK:(h + 1) * DQK] for h in heads]
    k = [qk[:, QK_WIDTH + h * DQK:QK_WIDTH + (h + 1) * DQK] for h in heads]
    v_bf = [v_ref[:, h * DV:(h + 1) * DV] for h in heads]
    i_col = [pre_col[:, h:h + 1] for h in heads]
    b_col = [sums_col[0:rows, N_HEADS + h:N_HEADS + h + 1] for h in heads]
    b_last = [sums_col[rows:2 * rows, N_HEADS + h:N_HEADS + h + 1] for h in heads]
    src = [pre_row[h:h + 1, :] - sums_row[N_HEADS + h:N_HEADS + h + 1, :] for h in heads]
    m_prev = [_rows_per_segment([m_state[s][h] for s in range(nseg)], seg_len) for h in heads]
    n_prev = [_rows_per_segment([n_state[s][h] for s in range(nseg)], seg_len) for h in heads]

    dmat = [jnp.where(lower, b_col[h] + src[h], -jnp.inf) for h in heads]
    inter = [b_col[h] + m_prev[h] for h in heads]
    m_t = [jnp.maximum(inter[h], jnp.max(dmat[h], axis=1, keepdims=True)) for h in heads]
    dlast = [b_last[h] + src[h] for h in heads]
    if nseg > 1:
        dlast = [jnp.where(same, d, -jnp.inf) for d in dlast]
    m_new = [jnp.maximum(b_last[h] + m_prev[h], jnp.max(dlast[h], axis=1, keepdims=True)) for h in heads]
    yield
    scores = [_dot_nt(q[h], k[h]) for h in heads]
    q_bf = [q[h].astype(BF16) for h in heads]
    qc = []
    for h in heads:
        parts = []
        for s in range(nseg):
            full = jnp.dot(q_bf[h], c_prev[s][h].astype(BF16), preferred_element_type=F32)
            parts.append(full[s * seg_len:(s + 1) * seg_len] if nseg > 1 else full)
        qc.append(parts[0] if nseg == 1 else jnp.concatenate(parts, axis=0))
    qn = [jnp.sum(q[h] * n_prev[h], axis=1, keepdims=True) for h in heads]
    yield

    w_inter = [jnp.exp(inter[h] - m_t[h]) for h in heads]
    a = [scores[h] * jnp.exp(dmat[h] - m_t[h]) for h in heads]
    w_s = [jnp.exp(b_last[h] - b_col[h] + i_col[h] - m_new[h]) for h in heads]
    decay = [jnp.exp(b_last[h] + m_prev[h] - m_new[h]) for h in heads]
    kw = [k[h] * w_s[h] for h in heads]
    yield

    num = [jnp.dot(a[h].astype(BF16), v_bf[h], preferred_element_type=F32) + w_inter[h] * qc[h] for h in heads]
    den = [jnp.sum(a[h], axis=1, keepdims=True) + w_inter[h] * qn[h] for h in heads]
    for h in heads:
        for s in range(nseg):
            kw_s = kw[h] if nseg == 1 else jnp.where(row_seg == s, kw[h], 0.0)
            dec = decay[h][s * seg_len:s * seg_len + 1, :]
            writes.append((s, h, dec * c_prev[s][h] + _dot_tn(kw_s, v_bf[h]),
                           dec * n_state[s][h] + jnp.sum(kw_s, axis=0, keepdims=True),
                           jnp.broadcast_to(m_new[h][s * seg_len:s * seg_len + 1, :], (1, LANES))))

    yield
    for h in heads:
        h_ref[:, h * DV:(h + 1) * DV] = num[h] * (1.0 / jnp.maximum(jnp.abs(den[h]), jnp.exp(-m_t[h])))

    for s, h, c_new, n_new, m_new in writes:
        c_out[s, h] = c_new
        n_out[s, h:h + 1, :] = n_new
        m_out[s, h:h + 1, :] = m_new
    yield


def _mlstm_prompt_body(*refs):
    per_seq = [refs[4 * a:4 * a + 4] for a in range(PROMPT_SEQS)]
    consts = refs[4 * PROMPT_SEQS:4 * PROMPT_SEQS + 4]
    h_ref, c_ref, n_ref, m_ref = refs[4 * PROMPT_SEQS + 4:]

    @pl.when(pl.program_id(1) == 0)
    def _():
        c_ref[...] = jnp.zeros_like(c_ref)
        n_ref[...] = jnp.zeros_like(n_ref)
        m_ref[...] = jnp.zeros_like(m_ref)

    chains = []
    for a in range(PROMPT_SEQS):
        one = pl.ds(a, 1)
        c, n, m = c_ref.at[one], n_ref.at[one], m_ref.at[one]
        chains.append(_mlstm_rows(*per_seq[a], *consts, c, n, m, c, n, m, h_ref.at[a], PROMPT_ROWS, PROMPT_ROWS))
    _emit_staggered(chains)


def _mlstm_sample_body(qk_ref, v_ref, gcol_ref, grow_ref, brow_ref, bcol_ref, ccol_ref, crow_ref,
                       c_in, n_in, m_in, h_ref, c_out, n_out, m_out, *, seg_len):
    _emit_staggered([_mlstm_rows(qk_ref, v_ref, gcol_ref, grow_ref, brow_ref, bcol_ref, ccol_ref, crow_ref,
                                 c_in, n_in, m_in, c_out, n_out, m_out, h_ref, SAMPLE_ROWS, seg_len)])


def _gla_constants(rows, seg_len):
    t, seg, _, cum = _segment_structure(rows, seg_len)
    level_id = np.full((rows, rows), -1, np.int32)
    level_id[t, t] = 0
    for idx, hb in enumerate(LEVELS):
        if 2 * hb > seg_len:
            continue
        block = t // (2 * hb)
        upper = (t % (2 * hb)) >= hb
        level_id[(block[:, None] == block[None, :]) & upper[:, None] & ~upper[None, :]] = idx + 1
    segind = (seg[:, None] == np.arange(LANES)[None, :]).astype(np.float32)
    return jnp.asarray(cum.astype(np.float32), BF16), jnp.asarray(level_id), jnp.asarray(segind, BF16)


def _row_bcast(x, row, n):
    return jnp.broadcast_to(x[row:row + 1, :], (n, x.shape[1]))


def _gla_rows(qk_ref, v_ref, glr_ref, w2_ref, bg_ref, cum_ref, lvl_ref, segind_ref,
              s_in, s_out, o_ref, rows, seg_len):
    nseg = rows // seg_len
    ntile = rows // SUBLANES
    row_id = lax.broadcasted_iota(jnp.int32, (rows, DQK), 0)
    row_seg = row_id // seg_len
    row_in_tile = row_id % SUBLANES

    pre = _dot(glr_ref[...], w2_ref[...]) + bg_ref[...]
    g_all = _log_sigmoid(pre) * (1.0 / GATE_TAU)
    g_hi, g_lo = _split_hi_lo(g_all)
    cum = cum_ref[...]
    bc_all = jnp.dot(cum, g_hi, preferred_element_type=F32) + jnp.dot(cum, g_lo, preferred_element_type=F32)
    level_tiles = [lvl_ref[t * SUBLANES:(t + 1) * SUBLANES, :] for t in range(ntile)]
    segind = segind_ref[...]
    total_col = (lax.dot_general(g_hi, segind, (((0,), (0,)), ((), ())), preferred_element_type=F32)
                 + lax.dot_general(g_lo, segind, (((0,), (0,)), ((), ())), preferred_element_type=F32))
    yield

    qk = qk_ref[...]
    s_prev = [[s_in[s, h] for h in range(N_HEADS)] for s in range(nseg)]
    writes = []

    heads = range(N_HEADS)
    sl = [slice(h * DQK, (h + 1) * DQK) for h in heads]
    q = [qk[:, sl[h]] for h in heads]
    k = [qk[:, QK_WIDTH + h * DQK:QK_WIDTH + (h + 1) * DQK] for h in heads]
    v_bf = [v_ref[:, h * DV:(h + 1) * DV] for h in heads]
    g = [g_all[:, sl[h]] for h in heads]
    bc = [bc_all[:, sl[h]] for h in heads]
    seg_total = [jnp.concatenate([_row_bcast(bc[h], (s + 1) * seg_len - 1, seg_len) for s in range(nseg)], axis=0)
                 if nseg > 1 else _row_bcast(bc[h], rows - 1, rows) for h in heads]

    qq = [(q[h] * jnp.exp(bc[h])).astype(BF16) for h in heads]
    kk = [k[h] * jnp.exp(seg_total[h] - bc[h]) for h in heads]
    o_inter = []
    for h in heads:
        parts = []
        for s in range(nseg):
            st = s_prev[s][h]
            oi = jnp.dot(qq[h], st.astype(BF16), preferred_element_type=F32)
            parts.append(oi[s * seg_len:(s + 1) * seg_len] if nseg > 1 else oi)
            kk_s = kk[h] if nseg == 1 else jnp.where(row_seg == s, kk[h], 0.0)
            writes.append((s, h, st * jnp.exp(total_col[sl[h], s:s + 1]) + _dot_tn(kk_s, v_bf[h])))
        o_inter.append(parts[0] if nseg == 1 else jnp.concatenate(parts, axis=0))
    yield

    a = [[None] * ntile for h in heads]

    def put(h, t, level, z_rows):
        prev = 0.0 if a[h][t] is None else a[h][t]
        a[h][t] = jnp.where(level_tiles[t] == level, z_rows, prev)

    for h in heads:
        z = _dot_nt(q[h], k[h])
        for t in range(ntile):
            put(h, t, 0, z[t * SUBLANES:(t + 1) * SUBLANES])

    for idx, hb in enumerate(LEVELS):
        if 2 * hb > seg_len:
            continue
        level = idx + 1
        if hb >= SUBLANES:
            starts = range(0, rows, 2 * hb)
            for h in heads:
                e_parts, x_parts = [], []
                for r0 in starts:
                    ref = _row_bcast(bc[h], r0 + hb - 1, hb)
                    e_parts += [ref - bc[h][r0:r0 + hb], bc[h][r0 + hb:r0 + 2 * hb] - ref]
                    x_parts += [k[h][r0:r0 + hb], q[h][r0 + hb:r0 + 2 * hb]]
                u = jnp.concatenate(x_parts, axis=0) * jnp.exp(jnp.concatenate(e_parts, axis=0))
                u_upper = jnp.concatenate([u[r0 + hb:r0 + 2 * hb] for r0 in starts], axis=0)
                z = _dot_nt(u_upper, u)
                upper_tiles = [t for r0 in starts for t in range((r0 + hb) // SUBLANES, (r0 + 2 * hb) // SUBLANES)]
                for i, t in enumerate(upper_tiles):
                    put(h, t, level, z[i * SUBLANES:(i + 1) * SUBLANES])
        else:
            upper = (row_id % (2 * hb)) >= hb
            for h in heads:
                if hb == 1:
                    e_l = jnp.where(upper, g[h], 0.0)
                else:
                    tiles = range(ntile)
                    if hb == 4:
                        ref = jnp.concatenate([_row_bcast(bc[h], t * SUBLANES + 3, SUBLANES) for t in tiles], axis=0)
                    else:
                        lo = jnp.concatenate([_row_bcast(bc[h], t * SUBLANES + 1, SUBLANES) for t in tiles], axis=0)
                        hi = jnp.concatenate([_row_bcast(bc[h], t * SUBLANES + 5, SUBLANES) for t in tiles], axis=0)
                        ref = jnp.where(row_in_tile < 4, lo, hi)
                    d = bc[h] - ref
                    e_l = jnp.where(upper, d, -d)
                u = jnp.where(upper, q[h], k[h]) * jnp.exp(e_l)
                z = _dot_nt(u, u)
                for t in range(ntile):
                    put(h, t, level, z[t * SUBLANES:(t + 1) * SUBLANES])
        yield

    for h in heads:
        o_ref[:, h * DV:(h + 1) * DV] = jnp.dot(jnp.concatenate(a[h], axis=0).astype(BF16), v_bf[h],
                                                preferred_element_type=F32) + o_inter[h]

    for s, h, s_new in writes:
        s_out[s, h] = s_new
    yield


def _gla_prompt_body(*refs):
    per_seq = [refs[3 * a:3 * a + 3] for a in range(PROMPT_SEQS)]
    consts = refs[3 * PROMPT_SEQS:3 * PROMPT_SEQS + 5]
    o_ref, s_ref = refs[3 * PROMPT_SEQS + 5:]

    @pl.when(pl.program_id(1) == 0)
    def _():
        s_ref[...] = jnp.zeros_like(s_ref)

    chains = []
    for a in range(PROMPT_SEQS):
        state = s_ref.at[pl.ds(a, 1)]
        chains.append(_gla_rows(*per_seq[a], *consts, state, state, o_ref.at[a], PROMPT_ROWS, PROMPT_ROWS))
    _emit_staggered(chains)


def _gla_sample_body(qk_ref, v_ref, glr_ref, w2_ref, bg_ref, cum_ref, lvl_ref, segind_ref,
                     s_in, o_ref, s_out, *, seg_len):
    _emit_staggered([_gla_rows(qk_ref, v_ref, glr_ref, w2_ref, bg_ref, cum_ref, lvl_ref, segind_ref,
                               s_in, s_out, o_ref, SAMPLE_ROWS, seg_len)])


def _mixer_params(n_axes):
    return pltpu.CompilerParams(dimension_semantics=("arbitrary",) * n_axes, vmem_limit_bytes=VMEM_LIMIT)


def _mixer_row_specs(rows, row_block):
    return [
        pl.BlockSpec((rows, 2 * QK_WIDTH), lambda *ids: (row_block(*ids), 0)),
        pl.BlockSpec((rows, V_WIDTH), lambda *ids: (row_block(*ids), 0)),
        pl.BlockSpec((rows, LANES), lambda *ids: (row_block(*ids), 0)),
    ]


def _state_spec(n, trailing):
    nd = 1 + len(trailing)
    return pl.BlockSpec((n,) + trailing, lambda *ids: (ids[0],) + (0,) * (nd - 1))


def _mlstm_mixer(qk, v, gcol, bias_i, bias_f, c0, n0, m0, n_prompt_seq, prompt_len):
    m_rows = qk.shape[0]
    n_prompt = n_prompt_seq * prompt_len
    n_sample_seq = c0.shape[0]
    seg_len = (m_rows - n_prompt) // n_sample_seq
    gates = gcol[:, :2 * N_HEADS]
    tiles = lambda a, rows: jnp.transpose(a.reshape(-1, rows, 2 * N_HEADS), (0, 2, 1))
    bias = jnp.concatenate([bias_i, bias_f]).astype(F32)
    bias_row = jnp.zeros((1, LANES), F32).at[0, :2 * N_HEADS].set(bias)
    bias_col = bias.reshape(2 * N_HEADS, 1)

    def const_inputs(rows, seg):
        ccol, crow = _mlstm_constants(rows, seg)
        arrays = [bias_row, bias_col, ccol, crow]
        return arrays, [_const_spec(a.shape) for a in arrays]

    rows = PROMPT_ROWS
    chunks = prompt_len // rows
    blk = lambda b, ci: b * chunks + ci
    arrays, specs = const_inputs(rows, rows)
    grow = tiles(gates[:n_prompt], rows)
    seq_specs, seq_arrays = [], []
    for a in range(PROMPT_SEQS):
        seq_blk = lambda g, ci, a=a: blk(g * PROMPT_SEQS + a, ci)
        seq_specs += _mixer_row_specs(rows, seq_blk) + [
            pl.BlockSpec((1, 2 * N_HEADS, rows), lambda g, ci, f=seq_blk: (f(g, ci), 0, 0))]
        seq_arrays += [qk, v, gcol, grow]
    y_p, c_p, n_p, m_p = pl.pallas_call(
        _mlstm_prompt_body,
        grid=(n_prompt_seq // PROMPT_SEQS, chunks),
        in_specs=seq_specs + specs,
        out_specs=[
            pl.BlockSpec((PROMPT_SEQS, rows, V_WIDTH), lambda g, ci: (g, ci, 0)),
            _state_spec(PROMPT_SEQS, (N_HEADS, DQK, DV)), _state_spec(PROMPT_SEQS, (N_HEADS, DQK)),
            _state_spec(PROMPT_SEQS, (N_HEADS, LANES)),
        ],
        out_shape=[
            jax.ShapeDtypeStruct((n_prompt_seq, prompt_len, V_WIDTH), F32),
            jax.ShapeDtypeStruct((n_prompt_seq, N_HEADS, DQK, DV), F32),
            jax.ShapeDtypeStruct((n_prompt_seq, N_HEADS, DQK), F32),
            jax.ShapeDtypeStruct((n_prompt_seq, N_HEADS, LANES), F32),
        ],
        compiler_params=_mixer_params(2),
        name="mlstm_prompt",
    )(*seq_arrays, *arrays)
    y_p = y_p.reshape(n_prompt, V_WIDTH)

    rows = SAMPLE_ROWS
    spb = rows // seg_len
    first = n_prompt // rows
    sblk = lambda i: first + i
    m0b = jnp.broadcast_to(m0[:, :, None], m0.shape + (LANES,))
    arrays, specs = const_inputs(rows, seg_len)
    y_s, c_s, n_s, m_s = pl.pallas_call(
        functools.partial(_mlstm_sample_body, seg_len=seg_len),
        grid=(n_sample_seq // spb,),
        in_specs=_mixer_row_specs(rows, sblk) + [pl.BlockSpec((1, 2 * N_HEADS, rows), lambda i: (i, 0, 0))] + specs
        + [_state_spec(spb, (N_HEADS, DQK, DV)), _state_spec(spb, (N_HEADS, DQK)), _state_spec(spb, (N_HEADS, LANES))],
        out_specs=[
            pl.BlockSpec((rows, V_WIDTH), lambda i: (i, 0)),
            _state_spec(spb, (N_HEADS, DQK, DV)), _state_spec(spb, (N_HEADS, DQK)), _state_spec(spb, (N_HEADS, LANES)),
        ],
        out_shape=[
            jax.ShapeDtypeStruct((m_rows - n_prompt, V_WIDTH), F32),
            jax.ShapeDtypeStruct(c0.shape, F32),
            jax.ShapeDtypeStruct(n0.shape, F32),
            jax.ShapeDtypeStruct(m0b.shape, F32),
        ],
        compiler_params=_mixer_params(1),
        name="mlstm_sample",
    )(qk, v, gcol, tiles(gates[n_prompt:], rows), *arrays, c0, n0, m0b)

    return (y_p, y_s), (c_p, n_p, m_p[:, :, 0]), (c_s, n_s, m_s[:, :, 0])


def _gla_mixer(qk, v, glr, w_gate2, b_gate, s0, n_prompt_seq, prompt_len):
    m_rows = qk.shape[0]
    n_prompt = n_prompt_seq * prompt_len
    n_sample_seq = s0.shape[0]
    seg_len = (m_rows - n_prompt) // n_sample_seq
    w2 = jnp.zeros((LANES, QK_WIDTH), BF16).at[:GATE_RANK].set(w_gate2.astype(BF16))
    bg = b_gate.reshape(1, QK_WIDTH).astype(F32)

    def const_inputs(rows, seg):
        cum, level_id, segind = _gla_constants(rows, seg)
        arrays = [w2, bg, cum, level_id, segind]
        return arrays, [_const_spec(a.shape) for a in arrays]

    rows = PROMPT_ROWS
    chunks = prompt_len // rows
    blk = lambda b, ci: b * chunks + ci
    arrays, specs = const_inputs(rows, rows)
    seq_specs, seq_arrays = [], []
    for a in range(PROMPT_SEQS):
        seq_specs += _mixer_row_specs(rows, lambda g, ci, a=a: blk(g * PROMPT_SEQS + a, ci))
        seq_arrays += [qk, v, glr]
    y_p, s_p = pl.pallas_call(
        _gla_prompt_body,
        grid=(n_prompt_seq // PROMPT_SEQS, chunks),
        in_specs=seq_specs + specs,
        out_specs=[
            pl.BlockSpec((PROMPT_SEQS, rows, V_WIDTH), lambda g, ci: (g, ci, 0)),
            _state_spec(PROMPT_SEQS, (N_HEADS, DQK, DV)),
        ],
        out_shape=[
            jax.ShapeDtypeStruct((n_prompt_seq, prompt_len, V_WIDTH), F32),
            jax.ShapeDtypeStruct((n_prompt_seq, N_HEADS, DQK, DV), F32),
        ],
        compiler_params=_mixer_params(2),
        name="gla_prompt",
    )(*seq_arrays, *arrays)
    y_p = y_p.reshape(n_prompt, V_WIDTH)

    rows = SAMPLE_ROWS
    spb = rows // seg_len
    first = n_prompt // rows
    sblk = lambda i: first + i
    arrays, specs = const_inputs(rows, seg_len)
    y_s, s_s = pl.pallas_call(
        functools.partial(_gla_sample_body, seg_len=seg_len),
        grid=(n_sample_seq // spb,),
        in_specs=_mixer_row_specs(rows, sblk) + specs + [_state_spec(spb, (N_HEADS, DQK, DV))],
        out_specs=[
            pl.BlockSpec((rows, V_WIDTH), lambda i: (i, 0)),
            _state_spec(spb, (N_HEADS, DQK, DV)),
        ],
        out_shape=[
            jax.ShapeDtypeStruct((m_rows - n_prompt, V_WIDTH), F32),
            jax.ShapeDtypeStruct(s0.shape, F32),
        ],
        compiler_params=_mixer_params(1),
        name="gla_sample",
    )(qk, v, glr, *arrays, s0)

    return (y_p, y_s), s_p, s_s


def _pad_gate_columns(w):
    return jnp.zeros((D_MODEL, LANES), BF16).at[:, :w.shape[1]].set(w.astype(BF16))


def kernel(x_prompt, x_sample, state_mlstm_C, state_mlstm_n, state_mlstm_m, state_gla_S, g_pre_mix, g_post_mix, g_pre_ffn, g_post_ffn, w_in_mlstm, b_i_mlstm, b_f_mlstm, g_head_mlstm, w_out_mlstm, w_in_gla, w_gate2_gla, b_gate_gla, g_head_gla, w_out_gla, w_ffn_gate, w_ffn_up, w_ffn_down):
    bp, sp, d = x_prompt.shape
    bs, ss, _ = x_sample.shape
    depth = g_pre_mix.shape[0]
    assert d == D_MODEL and sp % PROMPT_ROWS == 0 and SAMPLE_ROWS % ss == 0 and (bs * ss) % SAMPLE_ROWS == 0
    assert (bp * sp) % TM == 0 and (bs * ss) % TM == 0 and bp % PROMPT_SEQS == 0

    xp = x_prompt.reshape(bp * sp, d)
    xs = x_sample.reshape(bs * ss, d)
    vec = lambda g: g.reshape(1, D_MODEL).astype(F32)

    prompt_states = {"C": [], "n": [], "m": [], "S": []}
    sample_states = {"C": [], "n": [], "m": [], "S": []}
    ffn_gate, ffn_up, ffn_down = (w.astype(BF16) for w in (w_ffn_gate, w_ffn_up, w_ffn_down))
    ones = jnp.ones((QK_WIDTH,), F32)
    head_scale = jnp.full((QK_WIDTH,), DQK ** -0.5, F32)
    for layer in range(depth):
        j = layer // 2
        if layer % 2 == 0:
            w_in = w_in_mlstm[j]
            qk, v, gate, gcol = _inproj(xp, xs, vec(g_pre_mix[layer]), w_in[:, :MAIN_WIDTH].astype(BF16),
                                        _pad_gate_columns(w_in[:, MAIN_WIDTH:]),
                                        jnp.concatenate([ones, head_scale]).reshape(1, -1), silu_gate=False)
            h, st_p, st_s = _mlstm_mixer(qk, v, gcol, b_i_mlstm[j], b_f_mlstm[j],
                                         state_mlstm_C[j], state_mlstm_n[j], state_mlstm_m[j], bp, sp)
            for dst, st in ((prompt_states, st_p), (sample_states, st_s)):
                dst["C"].append(st[0]); dst["n"].append(st[1]); dst["m"].append(st[2])
            w_out, g_head = w_out_mlstm[j], g_head_mlstm[j]
        else:
            w_in = w_in_gla[j]
            qk, v, gate, glr = _inproj(xp, xs, vec(g_pre_mix[layer]), w_in[:, :MAIN_WIDTH].astype(BF16),
                                       _pad_gate_columns(w_in[:, MAIN_WIDTH:]),
                                       jnp.concatenate([head_scale, ones]).reshape(1, -1), silu_gate=True)
            h, s_p, s_s = _gla_mixer(qk, v, glr, w_gate2_gla[j], b_gate_gla[j], state_gla_S[j], bp, sp)
            prompt_states["S"].append(s_p)
            sample_states["S"].append(s_s)
            w_out, g_head = w_out_gla[j], g_head_gla[j]
        xp, xs = _block_tail(h[0], h[1], gate, g_head.reshape(1, V_WIDTH).astype(F32), w_out.astype(BF16), xp, xs,
                             vec(g_post_mix[layer]), vec(g_pre_ffn[layer]), vec(g_post_ffn[layer]),
                             ffn_gate, ffn_up, ffn_down, layer)

    stack = lambda xs_: jnp.stack(xs_)
    return (xp.reshape(bp, sp, d), xs.reshape(bs, ss, d),
            stack(prompt_states["C"]), stack(prompt_states["n"]), stack(prompt_states["m"]), stack(prompt_states["S"]),
            stack(sample_states["C"]), stack(sample_states["n"]), stack(sample_states["m"]), stack(sample_states["S"]))
```

```python
import functools

import numpy as np
import jax
import jax.numpy as jnp
from jax import lax
from jax.experimental import pallas as pl
from jax.experimental.pallas import tpu as pltpu

F32 = jnp.float32
BF16 = jnp.bfloat16

D_MODEL = 1024
N_HEADS = 4
DQK = 128
DV = 256
QK_WIDTH = N_HEADS * DQK
V_WIDTH = N_HEADS * DV
MAIN_WIDTH = 2 * QK_WIDTH + 2 * V_WIDTH
GATE_RANK = 16
GATE_TAU = 16.0
EPS = 1e-6
LANES = 128
SUBLANES = 8
MXU_DIM = 256
VMEM_LIMIT = 56 * 1024 * 1024

PROMPT_ROWS = 128
PROMPT_SEQS = 8
SAMPLE_ROWS = 128
LEVELS = (64, 32, 16, 8, 4, 2, 1)
TM = 512
FF_SPLITS = 2
ROW_GROUPS = 2


def _dot(a, b):
    return jnp.dot(a.astype(BF16), b.astype(BF16), preferred_element_type=F32)


def _dot_nt(a, b):
    return lax.dot_general(a.astype(BF16), b.astype(BF16), (((1,), (1,)), ((), ())),
                           preferred_element_type=F32)


def _dot_tn(a, b):
    return lax.dot_general(a.astype(BF16), b.astype(BF16), (((0,), (0,)), ((), ())),
                           preferred_element_type=F32)


def _split_hi_lo(x):
    hi = x.astype(BF16)
    lo = (x - hi.astype(F32)).astype(BF16)
    return hi, lo


def _log_sigmoid(x):
    return jnp.minimum(x, 0.0) - jnp.log(1.0 + jnp.exp(-jnp.abs(x)))


def _sigmoid(x):
    return 1.0 / (1.0 + jnp.exp(-x))


def _rms(x, gain):
    return x * lax.rsqrt(jnp.mean(x * x, axis=-1, keepdims=True) + EPS) * gain


def _rows_per_segment(values, seg_len):
    parts = [jnp.broadcast_to(v, (seg_len, v.shape[1])) for v in values]
    return parts[0] if len(parts) == 1 else jnp.concatenate(parts, axis=0)


def _emit_staggered(chains, lag=1):
    pending, live, step = list(chains), [], 0
    while pending or live:
        while pending and (lag == 0 or step % lag == 0):
            live.append(pending.pop(0))
            if lag:
                break
        step += 1
        for chain in list(live):
            if next(chain, StopIteration) is StopIteration:
                live.remove(chain)


def _const_spec(shape):
    nd = len(shape)
    return pl.BlockSpec(shape, lambda *_: (0,) * nd)


def _two_group_specs(n_prompt_blocks, width, single_buffer_sample=False):
    prompt = pl.BlockSpec((TM, width), lambda i: (jnp.minimum(i, n_prompt_blocks - 1), 0))
    mode = dict(pipeline_mode=pl.Buffered(1)) if single_buffer_sample else {}
    sample = pl.BlockSpec((TM, width), lambda i: (jnp.maximum(i - n_prompt_blocks, 0), 0), **mode)
    return prompt, sample


def _resident_spec(shape, layer=None):
    nd = len(shape)
    if layer is None:
        return pl.BlockSpec(shape, lambda i: (0,) * nd, pipeline_mode=pl.Buffered(1))
    return pl.BlockSpec((None,) + tuple(shape), lambda i: (layer,) + (0,) * nd, pipeline_mode=pl.Buffered(1))


def _inproj_body(xp_ref, xs_ref, gain_ref, w_ref, wg_ref, scale_ref, oqk_ref, ov_ref, ogate_ref, og_ref,
                 *maybe_grow_ref, n_prompt_blocks, silu_gate):
    is_prompt = pl.program_id(0) < n_prompt_blocks
    group_rows = TM // ROW_GROUPS
    groups = [slice(a * group_rows, (a + 1) * group_rows) for a in range(ROW_GROUPS)]
    h = [None] * ROW_GROUPS
    for a in range(ROW_GROUPS + 1):
        if a < ROW_GROUPS:
            x = jnp.where(is_prompt, xp_ref[groups[a], :], xs_ref[groups[a], :])
            h[a] = _rms(x, gain_ref[...]).astype(BF16)
        if a > 0:
            rows, hb = groups[a - 1], h[a - 1]
            narrow = jnp.dot(hb, wg_ref[...], preferred_element_type=F32)
            og_ref[rows, :] = narrow
            for grow_ref in maybe_grow_ref:
                grow_ref[:, rows] = narrow.T[0:SUBLANES, :]
            qk = jnp.dot(hb, w_ref[:, 0:2 * QK_WIDTH], preferred_element_type=F32)
            oqk_ref[rows, :] = qk * scale_ref[...]
            v = jnp.dot(hb, w_ref[:, 2 * QK_WIDTH:2 * QK_WIDTH + V_WIDTH], preferred_element_type=F32)
            ov_ref[rows, :] = v.astype(BF16)
            gate = jnp.dot(hb, w_ref[:, 2 * QK_WIDTH + V_WIDTH:MAIN_WIDTH], preferred_element_type=F32)
            ogate_ref[rows, :] = gate * _sigmoid(gate) if silu_gate else _sigmoid(gate)


def _inproj(xp, xs, gain, w_in, layer, w_gate, qk_scale, silu_gate, gate_rows):
    npb = xp.shape[0] // TM
    m = xp.shape[0] + xs.shape[0]
    spec_p, spec_s = _two_group_specs(npb, D_MODEL)
    rows_spec = lambda width: pl.BlockSpec((TM, width), lambda i: (i, 0))
    out_specs = [rows_spec(2 * QK_WIDTH), rows_spec(V_WIDTH), rows_spec(V_WIDTH), rows_spec(LANES)]
    out_shape = [
        jax.ShapeDtypeStruct((m, 2 * QK_WIDTH), F32),
        jax.ShapeDtypeStruct((m, V_WIDTH), BF16),
        jax.ShapeDtypeStruct((m, V_WIDTH), F32),
        jax.ShapeDtypeStruct((m, LANES), F32),
    ]
    if gate_rows:
        out_specs.append(pl.BlockSpec((SUBLANES, TM), lambda i: (0, i)))
        out_shape.append(jax.ShapeDtypeStruct((SUBLANES, m), F32))
    return pl.pallas_call(
        functools.partial(_inproj_body, n_prompt_blocks=npb, silu_gate=silu_gate),
        grid=(m // TM,),
        in_specs=[spec_p, spec_s, _resident_spec((1, D_MODEL)), _resident_spec(w_in.shape[1:], layer),
                  _resident_spec((D_MODEL, LANES)), _resident_spec((1, 2 * QK_WIDTH))],
        out_specs=out_specs,
        out_shape=out_shape,
        compiler_params=pltpu.CompilerParams(dimension_semantics=("arbitrary",), vmem_limit_bytes=VMEM_LIMIT),
        name="inproj",
    )(xp, xs, gain, w_in, w_gate, qk_scale)


def _tail_body(hp_ref, hs_ref, gate_ref, ghead_ref, wo_ref, xp_ref, xs_ref, gpm_ref, gpf_ref, gqf_ref,
               wg_ref, wu_ref, wd_ref, op_ref, os_ref, out_scr, *, n_prompt_blocks):
    i = pl.program_id(0)
    is_prompt = i < n_prompt_blocks
    d_ff = wg_ref.shape[1]
    n_tiles = d_ff // MXU_DIM
    bounds = [MXU_DIM * ((n_tiles * t + FF_SPLITS - 1) // FF_SPLITS) for t in range(FF_SPLITS + 1)]

    def stages(rows):
        raw = jnp.where(is_prompt, hp_ref[rows, :], hs_ref[rows, :])
        gain = ghead_ref[...]
        y = jnp.concatenate(
            [gate_ref[rows, h * DV:(h + 1) * DV] * _rms(raw[:, h * DV:(h + 1) * DV], gain[:, h * DV:(h + 1) * DV])
             for h in range(N_HEADS)], axis=1).astype(BF16)
        yield
        mix = jnp.dot(y, wo_ref[...], preferred_element_type=F32)
        yield
        x1 = jnp.where(is_prompt, xp_ref[rows, :], xs_ref[rows, :]) + _rms(mix, gpm_ref[...])
        h = _rms(x1, gpf_ref[...]).astype(BF16)
        yield
        ffn = None
        for t in range(FF_SPLITS):
            cols = slice(bounds[t], bounds[t + 1])
            gate = jnp.dot(h, wg_ref[:, cols], preferred_element_type=F32)
            up = jnp.dot(h, wu_ref[:, cols], preferred_element_type=F32)
            yield
            act = (gate * _sigmoid(gate) * up).astype(BF16)
            yield
            part = jnp.dot(act, wd_ref[cols, :], preferred_element_type=F32)
            ffn = part if ffn is None else ffn + part
            yield
        out_scr[rows, :] = x1 + _rms(ffn, gqf_ref[...])
        yield

    group_rows = TM // ROW_GROUPS
    _emit_staggered([stages(slice(a * group_rows, (a + 1) * group_rows)) for a in range(ROW_GROUPS)])

    @pl.when(is_prompt)
    def _():
        op_ref[...] = out_scr[...]

    @pl.when(jnp.logical_not(is_prompt))
    def _():
        os_ref[...] = out_scr[...]


def _block_tail(hp, hs, gate, ghead, w_out, xp, xs, g_post_mix, g_pre_ffn, g_post_ffn, w_gate, w_up, w_down, layer):
    npb = xp.shape[0] // TM
    m = xp.shape[0] + xs.shape[0]
    d_ff = w_gate.shape[2]
    assert d_ff % MXU_DIM == 0
    in_p, in_s = _two_group_specs(npb, D_MODEL, single_buffer_sample=True)
    vec = _resident_spec((1, D_MODEL))
    return pl.pallas_call(
        functools.partial(_tail_body, n_prompt_blocks=npb),
        grid=(m // TM,),
        in_specs=[
            in_p, in_s,
            pl.BlockSpec((TM, V_WIDTH), lambda i: (i, 0)),
            _resident_spec((1, V_WIDTH)),
            _resident_spec((V_WIDTH, D_MODEL)),
            in_p, in_s,
            vec, vec, vec,
            _resident_spec((D_MODEL, d_ff), layer), _resident_spec((D_MODEL, d_ff), layer),
            _resident_spec((d_ff, D_MODEL), layer),
        ],
        out_specs=list(_two_group_specs(npb, D_MODEL)),
        out_shape=[jax.ShapeDtypeStruct(xp.shape, F32), jax.ShapeDtypeStruct(xs.shape, F32)],
        scratch_shapes=[pltpu.VMEM((TM, D_MODEL), F32)],
        compiler_params=pltpu.CompilerParams(dimension_semantics=("arbitrary",), vmem_limit_bytes=VMEM_LIMIT),
        name="block_tail",
    )(hp, hs, gate, ghead, w_out, xp, xs, g_post_mix, g_pre_ffn, g_post_ffn, w_gate, w_up, w_down)


def _segment_structure(rows, seg_len):
    t = np.arange(rows)
    seg = t // seg_len
    same = seg[:, None] == seg[None, :]
    cum = same & (t[None, :] <= t[:, None])
    return t, seg, same, cum


def _mlstm_constants(rows, seg_len):
    _, _, same, cum = _segment_structure(rows, seg_len)
    col = np.concatenate([cum, same], axis=0).astype(np.float32)
    row = cum.T.astype(np.float32)
    return jnp.asarray(col, BF16), jnp.asarray(row, BF16)


def _mlstm_rows(qk_ref, v_ref, gcol_ref, grow_ref, bias_row_ref, bias_col_ref, ccol_ref, crow_ref,
                c_in, n_in, m_in, c_out, n_out, m_out, h_ref, rows, seg_len):
    nseg = rows // seg_len
    r = lax.broadcasted_iota(jnp.int32, (rows, rows), 0)
    c = lax.broadcasted_iota(jnp.int32, (rows, rows), 1)
    same = None if nseg == 1 else (r // seg_len) == (c // seg_len)
    lower = (c <= r) if nseg == 1 else same & (c <= r)
    row_seg = lax.broadcasted_iota(jnp.int32, (rows, DQK), 0) // seg_len

    pre_col = gcol_ref[...] + bias_row_ref[...]
    grow = grow_ref[0] if len(grow_ref.shape) == 3 else grow_ref[...]
    pre_row = grow + bias_col_ref[...]
    ccol = ccol_ref[...]
    crow = crow_ref[...]
    hi, lo = _split_hi_lo(_log_sigmoid(pre_col))
    sums_col = jnp.dot(ccol, hi, preferred_element_type=F32) + jnp.dot(ccol, lo, preferred_element_type=F32)
    hi, lo = _split_hi_lo(_log_sigmoid(pre_row))
    sums_row = jnp.dot(hi, crow, preferred_element_type=F32) + jnp.dot(lo, crow, preferred_element_type=F32)
    yield

    qk = qk_ref[...]
    c_prev = [[c_in[s, h] for h in range(N_HEADS)] for s in range(nseg)]
    n_state = [[n_in[s, h:h + 1, :] for h in range(N_HEADS)] for s in range(nseg)]
    m_state = [[m_in[s, h:h + 1, 0:1] for h in range(N_HEADS)] for s in range(nseg)]
    writes = []

    heads = range(N_HEADS)
    q = [qk[:, h * DQK:(h + 1) * DQK] for h in heads]
    k = [qk[:, QK_WIDTH + h * DQK:QK_WIDTH + (h + 1) * DQK] for h in heads]
    v_bf = [v_ref[:, h * DV:(h + 1) * DV] for h in heads]
    i_col = [pre_col[:, h:h + 1] for h in heads]
    b_col = [sums_col[0:rows, N_HEADS + h:N_HEADS + h + 1] for h in heads]
    b_last = [sums_col[rows:2 * rows, N_HEADS + h:N_HEADS + h + 1] for h in heads]
    src = [pre_row[h:h + 1, :] - sums_row[N_HEADS + h:N_HEADS + h + 1, :] for h in heads]
    m_prev = [_rows_per_segment([m_state[s][h] for s in range(nseg)], seg_len) for h in heads]
    n_prev = [_rows_per_segment([n_state[s][h] for s in range(nseg)], seg_len) for h in heads]

    dmat = [jnp.where(lower, b_col[h] + src[h], -jnp.inf) for h in heads]
    inter = [b_col[h] + m_prev[h] for h in heads]
    m_t = [jnp.maximum(inter[h], jnp.max(dmat[h], axis=1, keepdims=True)) for h in heads]
    dlast = [b_last[h] + src[h] for h in heads]
    if nseg > 1:
        dlast = [jnp.where(same, d, -jnp.inf) for d in dlast]
    m_new = [jnp.maximum(b_last[h] + m_prev[h], jnp.max(dlast[h], axis=1, keepdims=True)) for h in heads]
    yield
    scores = [_dot_nt(q[h], k[h]) for h in heads]
    q_bf = [q[h].astype(BF16) for h in heads]
    qc = []
    for h in heads:
        parts = []
        for s in range(nseg):
            full = jnp.dot(q_bf[h], c_prev[s][h].astype(BF16), preferred_element_type=F32)
            parts.append(full[s * seg_len:(s + 1) * seg_len] if nseg > 1 else full)
        qc.append(parts[0] if nseg == 1 else jnp.concatenate(parts, axis=0))
    qn = [jnp.sum(q[h] * n_prev[h], axis=1, keepdims=True) for h in heads]
    yield

    w_inter = [jnp.exp(inter[h] - m_t[h]) for h in heads]
    a = [scores[h] * jnp.exp(dmat[h] - m_t[h]) for h in heads]
    w_s = [jnp.exp(b_last[h] - b_col[h] + i_col[h] - m_new[h]) for h in heads]
    decay = [jnp.exp(b_last[h] + m_prev[h] - m_new[h]) for h in heads]
    kw = [k[h] * w_s[h] for h in heads]
    yield

    num = [jnp.dot(a[h].astype(BF16), v_bf[h], preferred_element_type=F32) + w_inter[h] * qc[h] for h in heads]
    den = [jnp.sum(a[h], axis=1, keepdims=True) + w_inter[h] * qn[h] for h in heads]
    for h in heads:
        for s in range(nseg):
            kw_s = kw[h] if nseg == 1 else jnp.where(row_seg == s, kw[h], 0.0)
            dec = decay[h][s * seg_len:s * seg_len + 1, :]
            writes.append((s, h, dec * c_prev[s][h] + _dot_tn(kw_s, v_bf[h]),
                           dec * n_state[s][h] + jnp.sum(kw_s, axis=0, keepdims=True),
                           jnp.broadcast_to(m_new[h][s * seg_len:s * seg_len + 1, :], (1, LANES))))

    yield
    for h in heads:
        h_ref[:, h * DV:(h + 1) * DV] = num[h] * (1.0 / jnp.maximum(jnp.abs(den[h]), jnp.exp(-m_t[h])))

    for s, h, c_new, n_new, m_new in writes:
        c_out[s, h] = c_new
        n_out[s, h:h + 1, :] = n_new
        m_out[s, h:h + 1, :] = m_new
    yield


def _mlstm_prompt_body(*refs):
    per_seq = [refs[4 * a:4 * a + 4] for a in range(PROMPT_SEQS)]
    consts = refs[4 * PROMPT_SEQS:4 * PROMPT_SEQS + 4]
    h_ref, c_ref, n_ref, m_ref = refs[4 * PROMPT_SEQS + 4:]

    @pl.when(pl.program_id(1) == 0)
    def _():
        c_ref[...] = jnp.zeros_like(c_ref)
        n_ref[...] = jnp.zeros_like(n_ref)
        m_ref[...] = jnp.zeros_like(m_ref)

    chains = []
    for a in range(PROMPT_SEQS):
        one = pl.ds(a, 1)
        c, n, m = c_ref.at[one], n_ref.at[one], m_ref.at[one]
        chains.append(_mlstm_rows(*per_seq[a], *consts, c, n, m, c, n, m, h_ref.at[a], PROMPT_ROWS, PROMPT_ROWS))
    _emit_staggered(chains)


def _mlstm_sample_body(qk_ref, v_ref, gcol_ref, grow_ref, brow_ref, bcol_ref, ccol_ref, crow_ref,
                       c_in, n_in, m_in, h_ref, c_out, n_out, m_out, *, seg_len):
    _emit_staggered([_mlstm_rows(qk_ref, v_ref, gcol_ref, grow_ref, brow_ref, bcol_ref, ccol_ref, crow_ref,
                                 c_in, n_in, m_in, c_out, n_out, m_out, h_ref, SAMPLE_ROWS, seg_len)])


def _gla_constants(rows, seg_len):
    t, seg, _, cum = _segment_structure(rows, seg_len)
    level_id = np.full((rows, rows), -1, np.int32)
    level_id[t, t] = 0
    for idx, hb in enumerate(LEVELS):
        if 2 * hb > seg_len:
            continue
        block = t // (2 * hb)
        upper = (t % (2 * hb)) >= hb
        level_id[(block[:, None] == block[None, :]) & upper[:, None] & ~upper[None, :]] = idx + 1
    segind = (seg[:, None] == np.arange(LANES)[None, :]).astype(np.float32)
    return jnp.asarray(cum.astype(np.float32), BF16), jnp.asarray(level_id), jnp.asarray(segind, BF16)


def _row_bcast(x, row, n):
    return jnp.broadcast_to(x[row:row + 1, :], (n, x.shape[1]))


def _gla_rows(qk_ref, v_ref, glr_ref, w2_ref, bg_ref, cum_ref, lvl_ref, segind_ref,
              s_in, s_out, o_ref, rows, seg_len):
    nseg = rows // seg_len
    ntile = rows // SUBLANES
    row_id = lax.broadcasted_iota(jnp.int32, (rows, DQK), 0)
    row_seg = row_id // seg_len
    row_in_tile = row_id % SUBLANES

    pre = _dot(glr_ref[...], w2_ref[...]) + bg_ref[...]
    g_all = _log_sigmoid(pre) * (1.0 / GATE_TAU)
    g_hi, g_lo = _split_hi_lo(g_all)
    cum = cum_ref[...]
    bc_all = jnp.dot(cum, g_hi, preferred_element_type=F32) + jnp.dot(cum, g_lo, preferred_element_type=F32)
    level_tiles = [lvl_ref[t * SUBLANES:(t + 1) * SUBLANES, :] for t in range(ntile)]
    segind = segind_ref[...]
    total_col = (lax.dot_general(g_hi, segind, (((0,), (0,)), ((), ())), preferred_element_type=F32)
                 + lax.dot_general(g_lo, segind, (((0,), (0,)), ((), ())), preferred_element_type=F32))
    yield

    qk = qk_ref[...]
    s_prev = [[s_in[s, h] for h in range(N_HEADS)] for s in range(nseg)]
    writes = []

    heads = range(N_HEADS)
    sl = [slice(h * DQK, (h + 1) * DQK) for h in heads]
    q = [qk[:, sl[h]] for h in heads]
    k = [qk[:, QK_WIDTH + h * DQK:QK_WIDTH + (h + 1) * DQK] for h in heads]
    v_bf = [v_ref[:, h * DV:(h + 1) * DV] for h in heads]
    g = [g_all[:, sl[h]] for h in heads]
    bc = [bc_all[:, sl[h]] for h in heads]
    seg_total = [jnp.concatenate([_row_bcast(bc[h], (s + 1) * seg_len - 1, seg_len) for s in range(nseg)], axis=0)
                 if nseg > 1 else _row_bcast(bc[h], rows - 1, rows) for h in heads]

    qq = [(q[h] * jnp.exp(bc[h])).astype(BF16) for h in heads]
    kk = [k[h] * jnp.exp(seg_total[h] - bc[h]) for h in heads]
    o_inter = []
    for h in heads:
        parts = []
        for s in range(nseg):
            st = s_prev[s][h]
            oi = jnp.dot(qq[h], st.astype(BF16), preferred_element_type=F32)
            parts.append(oi[s * seg_len:(s + 1) * seg_len] if nseg > 1 else oi)
            kk_s = kk[h] if nseg == 1 else jnp.where(row_seg == s, kk[h], 0.0)
            writes.append((s, h, st * jnp.exp(total_col[sl[h], s:s + 1]) + _dot_tn(kk_s, v_bf[h])))
        o_inter.append(parts[0] if nseg == 1 else jnp.concatenate(parts, axis=0))
    yield

    a = [[None] * ntile for h in heads]

    def put(h, t, level, z_rows):
        prev = 0.0 if a[h][t] is None else a[h][t]
        a[h][t] = jnp.where(level_tiles[t] == level, z_rows, prev)

    for h in heads:
        z = _dot_nt(q[h], k[h])
        for t in range(ntile):
            put(h, t, 0, z[t * SUBLANES:(t + 1) * SUBLANES])

    for idx, hb in enumerate(LEVELS):
        if 2 * hb > seg_len:
            continue
        level = idx + 1
        if hb >= SUBLANES:
            starts = range(0, rows, 2 * hb)
            for h in heads:
                e_parts, x_parts = [], []
                for r0 in starts:
                    ref = _row_bcast(bc[h], r0 + hb - 1, hb)
                    e_parts += [ref - bc[h][r0:r0 + hb], bc[h][r0 + hb:r0 + 2 * hb] - ref]
                    x_parts += [k[h][r0:r0 + hb], q[h][r0 + hb:r0 + 2 * hb]]
                u = jnp.concatenate(x_parts, axis=0) * jnp.exp(jnp.concatenate(e_parts, axis=0))
                u_upper = jnp.concatenate([u[r0 + hb:r0 + 2 * hb] for r0 in starts], axis=0)
                z = _dot_nt(u_upper, u)
                upper_tiles = [t for r0 in starts for t in range((r0 + hb) // SUBLANES, (r0 + 2 * hb) // SUBLANES)]
                for i, t in enumerate(upper_tiles):
                    put(h, t, level, z[i * SUBLANES:(i + 1) * SUBLANES])
        else:
            upper = (row_id % (2 * hb)) >= hb
            for h in heads:
                if hb == 1:
                    e_l = jnp.where(upper, g[h], 0.0)
                else:
                    tiles = range(ntile)
                    if hb == 4:
                        ref = jnp.concatenate([_row_bcast(bc[h], t * SUBLANES + 3, SUBLANES) for t in tiles], axis=0)
                    else:
                        lo = jnp.concatenate([_row_bcast(bc[h], t * SUBLANES + 1, SUBLANES) for t in tiles], axis=0)
                        hi = jnp.concatenate([_row_bcast(bc[h], t * SUBLANES + 5, SUBLANES) for t in tiles], axis=0)
                        ref = jnp.where(row_in_tile < 4, lo, hi)
                    d = bc[h] - ref
                    e_l = jnp.where(upper, d, -d)
                u = jnp.where(upper, q[h], k[h]) * jnp.exp(e_l)
                z = _dot_nt(u, u)
                for t in range(ntile):
                    put(h, t, level, z[t * SUBLANES:(t + 1) * SUBLANES])
        yield

    for h in heads:
        o_ref[:, h * DV:(h + 1) * DV] = jnp.dot(jnp.concatenate(a[h], axis=0).astype(BF16), v_bf[h],
                                                preferred_element_type=F32) + o_inter[h]

    for s, h, s_new in writes:
        s_out[s, h] = s_new
    yield


def _gla_prompt_body(*refs):
    per_seq = [refs[3 * a:3 * a + 3] for a in range(PROMPT_SEQS)]
    consts = refs[3 * PROMPT_SEQS:3 * PROMPT_SEQS + 5]
    o_ref, s_ref = refs[3 * PROMPT_SEQS + 5:]

    @pl.when(pl.program_id(1) == 0)
    def _():
        s_ref[...] = jnp.zeros_like(s_ref)

    chains = []
    for a in range(PROMPT_SEQS):
        state = s_ref.at[pl.ds(a, 1)]
        chains.append(_gla_rows(*per_seq[a], *consts, state, state, o_ref.at[a], PROMPT_ROWS, PROMPT_ROWS))
    _emit_staggered(chains)


def _gla_sample_body(qk_ref, v_ref, glr_ref, w2_ref, bg_ref, cum_ref, lvl_ref, segind_ref,
                     s_in, o_ref, s_out, *, seg_len):
    _emit_staggered([_gla_rows(qk_ref, v_ref, glr_ref, w2_ref, bg_ref, cum_ref, lvl_ref, segind_ref,
                               s_in, s_out, o_ref, SAMPLE_ROWS, seg_len)])


def _mixer_params(n_axes):
    return pltpu.CompilerParams(dimension_semantics=("arbitrary",) * n_axes, vmem_limit_bytes=VMEM_LIMIT)


def _mixer_row_specs(rows, row_block):
    return [
        pl.BlockSpec((rows, 2 * QK_WIDTH), lambda *ids: (row_block(*ids), 0)),
        pl.BlockSpec((rows, V_WIDTH), lambda *ids: (row_block(*ids), 0)),
        pl.BlockSpec((rows, LANES), lambda *ids: (row_block(*ids), 0)),
    ]


def _state_spec(n, trailing):
    nd = 1 + len(trailing)
    return pl.BlockSpec((n,) + trailing, lambda *ids: (ids[0],) + (0,) * (nd - 1))


def _mlstm_mixer(qk, v, gcol, grow, bias_i, bias_f, c0, n0, m0, n_prompt_seq, prompt_len):
    m_rows = qk.shape[0]
    n_prompt = n_prompt_seq * prompt_len
    n_sample_seq = c0.shape[0]
    seg_len = (m_rows - n_prompt) // n_sample_seq
    bias = jnp.concatenate([bias_i, bias_f]).astype(F32)
    bias_row = jnp.zeros((1, LANES), F32).at[0, :2 * N_HEADS].set(bias)
    bias_col = bias.reshape(2 * N_HEADS, 1)

    def const_inputs(rows, seg):
        ccol, crow = _mlstm_constants(rows, seg)
        arrays = [bias_row, bias_col, ccol, crow]
        return arrays, [_const_spec(a.shape) for a in arrays]

    rows = PROMPT_ROWS
    chunks = prompt_len // rows
    blk = lambda b, ci: b * chunks + ci
    arrays, specs = const_inputs(rows, rows)
    seq_specs, seq_arrays = [], []
    for a in range(PROMPT_SEQS):
        seq_blk = lambda g, ci, a=a: blk(g * PROMPT_SEQS + a, ci)
        seq_specs += _mixer_row_specs(rows, seq_blk) + [
            pl.BlockSpec((2 * N_HEADS, rows), lambda g, ci, f=seq_blk: (0, f(g, ci)))]
        seq_arrays += [qk, v, gcol, grow]
    y_p, c_p, n_p, m_p = pl.pallas_call(
        _mlstm_prompt_body,
        grid=(n_prompt_seq // PROMPT_SEQS, chunks),
        in_specs=seq_specs + specs,
        out_specs=[
            pl.BlockSpec((PROMPT_SEQS, rows, V_WIDTH), lambda g, ci: (g, ci, 0)),
            _state_spec(PROMPT_SEQS, (N_HEADS, DQK, DV)), _state_spec(PROMPT_SEQS, (N_HEADS, DQK)),
            _state_spec(PROMPT_SEQS, (N_HEADS, LANES)),
        ],
        out_shape=[
            jax.ShapeDtypeStruct((n_prompt_seq, prompt_len, V_WIDTH), F32),
            jax.ShapeDtypeStruct((n_prompt_seq, N_HEADS, DQK, DV), F32),
            jax.ShapeDtypeStruct((n_prompt_seq, N_HEADS, DQK), F32),
            jax.ShapeDtypeStruct((n_prompt_seq, N_HEADS, LANES), F32),
        ],
        compiler_params=_mixer_params(2),
        name="mlstm_prompt",
    )(*seq_arrays, *arrays)
    y_p = y_p.reshape(n_prompt, V_WIDTH)

    rows = SAMPLE_ROWS
    spb = rows // seg_len
    first = n_prompt // rows
    sblk = lambda i: first + i
    m0b = jnp.broadcast_to(m0[:, :, None], m0.shape + (LANES,))
    arrays, specs = const_inputs(rows, seg_len)
    sample_tiles = jnp.transpose(grow[:, n_prompt:].reshape(2 * N_HEADS, -1, rows), (1, 0, 2))
    y_s, c_s, n_s, m_s = pl.pallas_call(
        functools.partial(_mlstm_sample_body, seg_len=seg_len),
        grid=(n_sample_seq // spb,),
        in_specs=_mixer_row_specs(rows, sblk) + [pl.BlockSpec((1, 2 * N_HEADS, rows), lambda i: (i, 0, 0))] + specs
        + [_state_spec(spb, (N_HEADS, DQK, DV)), _state_spec(spb, (N_HEADS, DQK)), _state_spec(spb, (N_HEADS, LANES))],
        out_specs=[
            pl.BlockSpec((rows, V_WIDTH), lambda i: (i, 0)),
            _state_spec(spb, (N_HEADS, DQK, DV)), _state_spec(spb, (N_HEADS, DQK)), _state_spec(spb, (N_HEADS, LANES)),
        ],
        out_shape=[
            jax.ShapeDtypeStruct((m_rows - n_prompt, V_WIDTH), F32),
            jax.ShapeDtypeStruct(c0.shape, F32),
            jax.ShapeDtypeStruct(n0.shape, F32),
            jax.ShapeDtypeStruct(m0b.shape, F32),
        ],
        compiler_params=_mixer_params(1),
        name="mlstm_sample",
    )(qk, v, gcol, sample_tiles, *arrays, c0, n0, m0b)

    return (y_p, y_s), (c_p, n_p, m_p[:, :, 0]), (c_s, n_s, m_s[:, :, 0])


def _gla_mixer(qk, v, glr, w_gate2, b_gate, s0, n_prompt_seq, prompt_len):
    m_rows = qk.shape[0]
    n_prompt = n_prompt_seq * prompt_len
    n_sample_seq = s0.shape[0]
    seg_len = (m_rows - n_prompt) // n_sample_seq
    w2 = jnp.zeros((LANES, QK_WIDTH), BF16).at[:GATE_RANK].set(w_gate2.astype(BF16))
    bg = b_gate.reshape(1, QK_WIDTH).astype(F32)

    def const_inputs(rows, seg):
        cum, level_id, segind = _gla_constants(rows, seg)
        arrays = [w2, bg, cum, level_id, segind]
        return arrays, [_const_spec(a.shape) for a in arrays]

    rows = PROMPT_ROWS
    chunks = prompt_len // rows
    blk = lambda b, ci: b * chunks + ci
    arrays, specs = const_inputs(rows, rows)
    seq_specs, seq_arrays = [], []
    for a in range(PROMPT_SEQS):
        seq_specs += _mixer_row_specs(rows, lambda g, ci, a=a: blk(g * PROMPT_SEQS + a, ci))
        seq_arrays += [qk, v, glr]
    y_p, s_p = pl.pallas_call(
        _gla_prompt_body,
        grid=(n_prompt_seq // PROMPT_SEQS, chunks),
        in_specs=seq_specs + specs,
        out_specs=[
            pl.BlockSpec((PROMPT_SEQS, rows, V_WIDTH), lambda g, ci: (g, ci, 0)),
            _state_spec(PROMPT_SEQS, (N_HEADS, DQK, DV)),
        ],
        out_shape=[
            jax.ShapeDtypeStruct((n_prompt_seq, prompt_len, V_WIDTH), F32),
            jax.ShapeDtypeStruct((n_prompt_seq, N_HEADS, DQK, DV), F32),
        ],
        compiler_params=_mixer_params(2),
        name="gla_prompt",
    )(*seq_arrays, *arrays)
    y_p = y_p.reshape(n_prompt, V_WIDTH)

    rows = SAMPLE_ROWS
    spb = rows // seg_len
    first = n_prompt // rows
    sblk = lambda i: first + i
    arrays, specs = const_inputs(rows, seg_len)
    y_s, s_s = pl.pallas_call(
        functools.partial(_gla_sample_body, seg_len=seg_len),
        grid=(n_sample_seq // spb,),
        in_specs=_mixer_row_specs(rows, sblk) + specs + [_state_spec(spb, (N_HEADS, DQK, DV))],
        out_specs=[
            pl.BlockSpec((rows, V_WIDTH), lambda i: (i, 0)),
            _state_spec(spb, (N_HEADS, DQK, DV)),
        ],
        out_shape=[
            jax.ShapeDtypeStruct((m_rows - n_prompt, V_WIDTH), F32),
            jax.ShapeDtypeStruct(s0.shape, F32),
        ],
        compiler_params=_mixer_params(1),
        name="gla_sample",
    )(qk, v, glr, *arrays, s0)

    return (y_p, y_s), s_p, s_s


def _pad_gate_columns(w):
    return jnp.zeros((D_MODEL, LANES), BF16).at[:, :w.shape[1]].set(w.astype(BF16))


def kernel(x_prompt, x_sample, state_mlstm_C, state_mlstm_n, state_mlstm_m, state_gla_S, g_pre_mix, g_post_mix, g_pre_ffn, g_post_ffn, w_in_mlstm, b_i_mlstm, b_f_mlstm, g_head_mlstm, w_out_mlstm, w_in_gla, w_gate2_gla, b_gate_gla, g_head_gla, w_out_gla, w_ffn_gate, w_ffn_up, w_ffn_down):
    bp, sp, d = x_prompt.shape
    bs, ss, _ = x_sample.shape
    depth = g_pre_mix.shape[0]
    assert d == D_MODEL and sp % PROMPT_ROWS == 0 and SAMPLE_ROWS % ss == 0 and (bs * ss) % SAMPLE_ROWS == 0
    assert (bp * sp) % TM == 0 and (bs * ss) % TM == 0 and bp % PROMPT_SEQS == 0

    xp = x_prompt.reshape(bp * sp, d)
    xs = x_sample.reshape(bs * ss, d)
    vec = lambda g: g.reshape(1, D_MODEL).astype(F32)

    prompt_states = {"C": [], "n": [], "m": [], "S": []}
    sample_states = {"C": [], "n": [], "m": [], "S": []}
    ffn_gate, ffn_up, ffn_down = (w.astype(BF16) for w in (w_ffn_gate, w_ffn_up, w_ffn_down))
    in_mlstm, in_gla = w_in_mlstm.astype(BF16), w_in_gla.astype(BF16)
    ones = jnp.ones((QK_WIDTH,), F32)
    head_scale = jnp.full((QK_WIDTH,), DQK ** -0.5, F32)
    for layer in range(depth):
        j = layer // 2
        if layer % 2 == 0:
            qk, v, gate, gcol, grow = _inproj(xp, xs, vec(g_pre_mix[layer]), in_mlstm, j,
                                              _pad_gate_columns(w_in_mlstm[j][:, MAIN_WIDTH:]),
                                              jnp.concatenate([ones, head_scale]).reshape(1, -1),
                                              silu_gate=False, gate_rows=True)
            h, st_p, st_s = _mlstm_mixer(qk, v, gcol, grow, b_i_mlstm[j], b_f_mlstm[j],
                                         state_mlstm_C[j], state_mlstm_n[j], state_mlstm_m[j], bp, sp)
            for dst, st in ((prompt_states, st_p), (sample_states, st_s)):
                dst["C"].append(st[0]); dst["n"].append(st[1]); dst["m"].append(st[2])
            w_out, g_head = w_out_mlstm[j], g_head_mlstm[j]
        else:
            qk, v, gate, glr = _inproj(xp, xs, vec(g_pre_mix[layer]), in_gla, j,
                                       _pad_gate_columns(w_in_gla[j][:, MAIN_WIDTH:]),
                                       jnp.concatenate([head_scale, ones]).reshape(1, -1),
                                       silu_gate=True, gate_rows=False)
            h, s_p, s_s = _gla_mixer(qk, v, glr, w_gate2_gla[j], b_gate_gla[j], state_gla_S[j], bp, sp)
            prompt_states["S"].append(s_p)
            sample_states["S"].append(s_s)
            w_out, g_head = w_out_gla[j], g_head_gla[j]
        xp, xs = _block_tail(h[0], h[1], gate, g_head.reshape(1, V_WIDTH).astype(F32), w_out.astype(BF16), xp, xs,
                             vec(g_post_mix[layer]), vec(g_pre_ffn[layer]), vec(g_post_ffn[layer]),
                             ffn_gate, ffn_up, ffn_down, layer)

    stack = lambda xs_: jnp.stack(xs_)
    return (xp.reshape(bp, sp, d), xs.reshape(bs, ss, d),
            stack(prompt_states["C"]), stack(prompt_states["n"]), stack(prompt_states["m"]), stack(prompt_states["S"]),
            stack(sample_states["C"]), stack(sample_states["n"]), stack(sample_states["m"]), stack(sample_states["S"]))
```

```python
import functools

import numpy as np
import jax
import jax.numpy as jnp
from jax import lax
from jax.experimental import pallas as pl
from jax.experimental.pallas import tpu as pltpu

F32 = jnp.float32
BF16 = jnp.bfloat16

D_MODEL = 1024
N_HEADS = 4
DQK = 128
DV = 256
QK_WIDTH = N_HEADS * DQK
V_WIDTH = N_HEADS * DV
MAIN_WIDTH = 2 * QK_WIDTH + 2 * V_WIDTH
GATE_RANK = 16
GATE_TAU = 16.0
EPS = 1e-6
LANES = 128
SUBLANES = 8
MXU_DIM = 256
VMEM_LIMIT = 56 * 1024 * 1024
MIXER_VMEM_LIMIT = 62 * 1024 * 1024

PROMPT_ROWS = 128
SAMPLE_ROWS = 64
LEVELS = (64, 32, 16, 8, 4, 2, 1)
TM = 512
FF_SPLITS = 2
ROW_GROUPS = 2


def _dot(a, b):
    return jnp.dot(a.astype(BF16), b.astype(BF16), preferred_element_type=F32)


def _dot_nt(a, b):
    return lax.dot_general(a.astype(BF16), b.astype(BF16), (((1,), (1,)), ((), ())),
                           preferred_element_type=F32)


def _dot_tn(a, b):
    return lax.dot_general(a.astype(BF16), b.astype(BF16), (((0,), (0,)), ((), ())),
                           preferred_element_type=F32)


def _split_hi_lo(x):
    hi = x.astype(BF16)
    lo = (x - hi.astype(F32)).astype(BF16)
    return hi, lo


def _log_sigmoid(x):
    return jnp.minimum(x, 0.0) - jnp.log(1.0 + jnp.exp(-jnp.abs(x)))


def _sigmoid(x):
    return 1.0 / (1.0 + jnp.exp(-x))


def _rms(x, gain):
    return x * lax.rsqrt(jnp.mean(x * x, axis=-1, keepdims=True) + EPS) * gain


def _rows_per_segment(values, seg_len):
    parts = [jnp.broadcast_to(v, (seg_len, v.shape[1])) for v in values]
    return parts[0] if len(parts) == 1 else jnp.concatenate(parts, axis=0)


def _emit_staggered(chains, lag=1):
    pending, live, step = list(chains), [], 0
    while pending or live:
        while pending and (lag == 0 or step % lag == 0):
            live.append(pending.pop(0))
            if lag:
                break
        step += 1
        for chain in list(live):
            if next(chain, StopIteration) is StopIteration:
                live.remove(chain)


def _const_spec(shape):
    nd = len(shape)
    return pl.BlockSpec(shape, lambda *_: (0,) * nd)


def _two_group_specs(n_prompt_blocks, width, single_buffer_sample=False):
    prompt = pl.BlockSpec((TM, width), lambda i: (jnp.minimum(i, n_prompt_blocks - 1), 0))
    mode = dict(pipeline_mode=pl.Buffered(1)) if single_buffer_sample else {}
    sample = pl.BlockSpec((TM, width), lambda i: (jnp.maximum(i - n_prompt_blocks, 0), 0), **mode)
    return prompt, sample


def _resident_spec(shape, layer=None):
    nd = len(shape)
    if layer is None:
        return pl.BlockSpec(shape, lambda i: (0,) * nd, pipeline_mode=pl.Buffered(1))
    return pl.BlockSpec((None,) + tuple(shape), lambda i: (layer,) + (0,) * nd, pipeline_mode=pl.Buffered(1))


def _inproj_body(x_ref, gain_ref, w_ref, wg_ref, scale_ref, oqk_ref, ov_ref, ogate_ref, og_ref,
                 *maybe_grow_ref, silu_gate):
    group_rows = TM // ROW_GROUPS
    groups = [slice(a * group_rows, (a + 1) * group_rows) for a in range(ROW_GROUPS)]
    h = [None] * ROW_GROUPS
    for a in range(ROW_GROUPS + 1):
        if a < ROW_GROUPS:
            h[a] = _rms(x_ref[groups[a], :], gain_ref[...]).astype(BF16)
        if a > 0:
            rows, hb = groups[a - 1], h[a - 1]
            narrow = jnp.dot(hb, wg_ref[...], preferred_element_type=F32)
            og_ref[rows, :] = narrow
            for grow_ref in maybe_grow_ref:
                grow_ref[:, rows] = narrow.T[0:SUBLANES, :]
            qk = jnp.dot(hb, w_ref[:, 0:2 * QK_WIDTH], preferred_element_type=F32)
            oqk_ref[rows, :] = qk * scale_ref[...]
            v = jnp.dot(hb, w_ref[:, 2 * QK_WIDTH:2 * QK_WIDTH + V_WIDTH], preferred_element_type=F32)
            ov_ref[rows, :] = v.astype(BF16)
            gate = jnp.dot(hb, w_ref[:, 2 * QK_WIDTH + V_WIDTH:MAIN_WIDTH], preferred_element_type=F32)
            ogate_ref[rows, :] = gate * _sigmoid(gate) if silu_gate else _sigmoid(gate)


def _inproj(x, gain, w_in, layer, w_gate, qk_scale, silu_gate, gate_rows):
    m = x.shape[0]
    rows_spec = lambda width: pl.BlockSpec((TM, width), lambda i: (i, 0))
    out_specs = [rows_spec(2 * QK_WIDTH), rows_spec(V_WIDTH), rows_spec(V_WIDTH), rows_spec(LANES)]
    out_shape = [
        jax.ShapeDtypeStruct((m, 2 * QK_WIDTH), F32),
        jax.ShapeDtypeStruct((m, V_WIDTH), BF16),
        jax.ShapeDtypeStruct((m, V_WIDTH), F32),
        jax.ShapeDtypeStruct((m, LANES), F32),
    ]
    if gate_rows:
        out_specs.append(pl.BlockSpec((SUBLANES, TM), lambda i: (0, i)))
        out_shape.append(jax.ShapeDtypeStruct((SUBLANES, m), F32))
    return pl.pallas_call(
        functools.partial(_inproj_body, silu_gate=silu_gate),
        grid=(m // TM,),
        in_specs=[rows_spec(D_MODEL), _resident_spec((1, D_MODEL)), _resident_spec(w_in.shape[1:], layer),
                  _resident_spec((D_MODEL, LANES)), _resident_spec((1, 2 * QK_WIDTH))],
        out_specs=out_specs,
        out_shape=out_shape,
        compiler_params=pltpu.CompilerParams(dimension_semantics=("arbitrary",), vmem_limit_bytes=VMEM_LIMIT),
        name="inproj",
    )(x, gain, w_in, w_gate, qk_scale)


def _tail_body(hp_ref, hs_ref, gp_ref, gs_ref, ghead_ref, wo_ref, xp_ref, xs_ref, gpm_ref, gpf_ref, gqf_ref,
               wg_ref, wu_ref, wd_ref, op_ref, os_ref, out_scr, *, n_prompt_blocks):
    i = pl.program_id(0)
    is_prompt = i < n_prompt_blocks
    d_ff = wg_ref.shape[1]
    n_tiles = d_ff // MXU_DIM
    bounds = [MXU_DIM * ((n_tiles * t + FF_SPLITS - 1) // FF_SPLITS) for t in range(FF_SPLITS + 1)]

    def stages(rows):
        raw = jnp.where(is_prompt, hp_ref[rows, :], hs_ref[rows, :])
        gate = jnp.where(is_prompt, gp_ref[rows, :], gs_ref[rows, :])
        gain = ghead_ref[...]
        y = jnp.concatenate(
            [gate[:, h * DV:(h + 1) * DV] * _rms(raw[:, h * DV:(h + 1) * DV], gain[:, h * DV:(h + 1) * DV])
             for h in range(N_HEADS)], axis=1).astype(BF16)
        yield
        mix = jnp.dot(y, wo_ref[...], preferred_element_type=F32)
        yield
        x1 = jnp.where(is_prompt, xp_ref[rows, :], xs_ref[rows, :]) + _rms(mix, gpm_ref[...])
        h = _rms(x1, gpf_ref[...]).astype(BF16)
        yield
        ffn = None
        for t in range(FF_SPLITS):
            cols = slice(bounds[t], bounds[t + 1])
            gate = jnp.dot(h, wg_ref[:, cols], preferred_element_type=F32)
            up = jnp.dot(h, wu_ref[:, cols], preferred_element_type=F32)
            yield
            act = (gate * _sigmoid(gate) * up).astype(BF16)
            yield
            part = jnp.dot(act, wd_ref[cols, :], preferred_element_type=F32)
            ffn = part if ffn is None else ffn + part
            yield
        out_scr[rows, :] = x1 + _rms(ffn, gqf_ref[...])
        yield

    group_rows = TM // ROW_GROUPS
    _emit_staggered([stages(slice(a * group_rows, (a + 1) * group_rows)) for a in range(ROW_GROUPS)])

    @pl.when(is_prompt)
    def _():
        op_ref[...] = out_scr[...]

    @pl.when(jnp.logical_not(is_prompt))
    def _():
        os_ref[...] = out_scr[...]


def _block_tail(hp, hs, gate_p, gate_s, ghead, w_out, xp, xs, g_post_mix, g_pre_ffn, g_post_ffn,
                w_gate, w_up, w_down, layer):
    npb = xp.shape[0] // TM
    m = xp.shape[0] + xs.shape[0]
    d_ff = w_gate.shape[2]
    assert d_ff % MXU_DIM == 0
    in_p, in_s = _two_group_specs(npb, D_MODEL, single_buffer_sample=True)
    vec = _resident_spec((1, D_MODEL))
    return pl.pallas_call(
        functools.partial(_tail_body, n_prompt_blocks=npb),
        grid=(m // TM,),
        in_specs=[
            in_p, in_s,
            in_p, in_s,
            _resident_spec((1, V_WIDTH)),
            _resident_spec((V_WIDTH, D_MODEL)),
            in_p, in_s,
            vec, vec, vec,
            _resident_spec((D_MODEL, d_ff), layer), _resident_spec((D_MODEL, d_ff), layer),
            _resident_spec((d_ff, D_MODEL), layer),
        ],
        out_specs=list(_two_group_specs(npb, D_MODEL)),
        out_shape=[jax.ShapeDtypeStruct(xp.shape, F32), jax.ShapeDtypeStruct(xs.shape, F32)],
        scratch_shapes=[pltpu.VMEM((TM, D_MODEL), F32)],
        compiler_params=pltpu.CompilerParams(dimension_semantics=("arbitrary",), vmem_limit_bytes=VMEM_LIMIT),
        name="block_tail",
    )(hp, hs, gate_p, gate_s, ghead, w_out, xp, xs, g_post_mix, g_pre_ffn, g_post_ffn, w_gate, w_up, w_down)


def _segment_structure(rows, seg_len):
    t = np.arange(rows)
    seg = t // seg_len
    same = seg[:, None] == seg[None, :]
    cum = same & (t[None, :] <= t[:, None])
    return t, seg, same, cum


def _mlstm_constants(rows, seg_len):
    _, _, same, cum = _segment_structure(rows, seg_len)
    col = np.concatenate([cum, same], axis=0).astype(np.float32)
    row = cum.T.astype(np.float32)
    return jnp.asarray(col, BF16), jnp.asarray(row, BF16)


def _mlstm_rows(qk_ref, v_ref, gcol_ref, grow_ref, bias_row_ref, bias_col_ref, ccol_ref, crow_ref,
                c_in, n_in, m_in, c_out, n_out, m_out, h_ref, rows, seg_len):
    nseg = rows // seg_len
    r = lax.broadcasted_iota(jnp.int32, (rows, rows), 0)
    c = lax.broadcasted_iota(jnp.int32, (rows, rows), 1)
    same = None if nseg == 1 else (r // seg_len) == (c // seg_len)
    lower = (c <= r) if nseg == 1 else same & (c <= r)
    row_seg = lax.broadcasted_iota(jnp.int32, (rows, DQK), 0) // seg_len

    pre_col = gcol_ref[...] + bias_row_ref[...]
    grow = grow_ref[0] if len(grow_ref.shape) == 3 else grow_ref[...]
    pre_row = grow + bias_col_ref[...]
    ccol = ccol_ref[...]
    crow = crow_ref[...]
    hi, lo = _split_hi_lo(_log_sigmoid(pre_col))
    sums_col = jnp.dot(ccol, hi, preferred_element_type=F32) + jnp.dot(ccol, lo, preferred_element_type=F32)
    hi, lo = _split_hi_lo(_log_sigmoid(pre_row))
    sums_row = jnp.dot(hi, crow, preferred_element_type=F32) + jnp.dot(lo, crow, preferred_element_type=F32)
    yield

    qk = qk_ref[...]
    c_prev = [[c_in[s, h] for h in range(N_HEADS)] for s in range(nseg)]
    n_state = [[n_in[s, h:h + 1, :] for h in range(N_HEADS)] for s in range(nseg)]
    m_state = [[m_in[s, h:h + 1, 0:1] for h in range(N_HEADS)] for s in range(nseg)]
    writes = []

    heads = range(N_HEADS)
    q = [qk[:, h * DQK:(h + 1) * DQK] for h in heads]
    k = [qk[:, QK_WIDTH + h * DQK:QK_WIDTH + (h + 1) * DQK] for h in heads]
    v_bf = [v_ref[:, h * DV:(h + 1) * DV] for h in heads]
    i_col = [pre_col[:, h:h + 1] for h in heads]
    b_col = [sums_col[0:rows, N_HEADS + h:N_HEADS + h + 1] for h in heads]
    b_last = [sums_col[rows:2 * rows, N_HEADS + h:N_HEADS + h + 1] for h in heads]
    src = [pre_row[h:h + 1, :] - sums_row[N_HEADS + h:N_HEADS + h + 1, :] for h in heads]
    m_prev = [_rows_per_segment([m_state[s][h] for s in range(nseg)], seg_len) for h in heads]
    n_prev = [_rows_per_segment([n_state[s][h] for s in range(nseg)], seg_len) for h in heads]

    dmat = [jnp.where(lower, b_col[h] + src[h], -jnp.inf) for h in heads]
    inter = [b_col[h] + m_prev[h] for h in heads]
    m_t = [jnp.maximum(inter[h], jnp.max(dmat[h], axis=1, keepdims=True)) for h in heads]
    dlast = [b_last[h] + src[h] for h in heads]
    if nseg > 1:
        dlast = [jnp.where(same, d, -jnp.inf) for d in dlast]
    m_new = [jnp.maximum(b_last[h] + m_prev[h], jnp.max(dlast[h], axis=1, keepdims=True)) for h in heads]
    yield
    scores = [_dot_nt(q[h], k[h]) for h in heads]
    q_bf = [q[h].astype(BF16) for h in heads]
    qc = []
    for h in heads:
        parts = []
        for s in range(nseg):
            full = jnp.dot(q_bf[h], c_prev[s][h].astype(BF16), preferred_element_type=F32)
            parts.append(full[s * seg_len:(s + 1) * seg_len] if nseg > 1 else full)
        qc.append(parts[0] if nseg == 1 else jnp.concatenate(parts, axis=0))
    qn = [jnp.sum(q[h] * n_prev[h], axis=1, keepdims=True) for h in heads]
    yield

    w_inter = [jnp.exp(inter[h] - m_t[h]) for h in heads]
    a = [scores[h] * jnp.exp(dmat[h] - m_t[h]) for h in heads]
    w_s = [jnp.exp(b_last[h] - b_col[h] + i_col[h] - m_new[h]) for h in heads]
    decay = [jnp.exp(b_last[h] + m_prev[h] - m_new[h]) for h in heads]
    kw = [k[h] * w_s[h] for h in heads]
    yield

    num = [jnp.dot(a[h].astype(BF16), v_bf[h], preferred_element_type=F32) + w_inter[h] * qc[h] for h in heads]
    den = [jnp.sum(a[h], axis=1, keepdims=True) + w_inter[h] * qn[h] for h in heads]
    for h in heads:
        for s in range(nseg):
            kw_s = kw[h] if nseg == 1 else jnp.where(row_seg == s, kw[h], 0.0)
            dec = decay[h][s * seg_len:s * seg_len + 1, :]
            writes.append((s, h, dec * c_prev[s][h] + _dot_tn(kw_s, v_bf[h]),
                           dec * n_state[s][h] + jnp.sum(kw_s, axis=0, keepdims=True),
                           jnp.broadcast_to(m_new[h][s * seg_len:s * seg_len + 1, :], (1, LANES))))

    yield
    for h in heads:
        h_ref[:, h * DV:(h + 1) * DV] = num[h] * (1.0 / jnp.maximum(jnp.abs(den[h]), jnp.exp(-m_t[h])))

    for s, h, c_new, n_new, m_new in writes:
        c_out[s, h] = c_new
        n_out[s, h:h + 1, :] = n_new
        m_out[s, h:h + 1, :] = m_new
    yield


def _mlstm_body(qk_p, v_p, gcol_p, grow_p, qk_s, v_s, gcol_s, grow_s, brow_ref, bcol_ref,
                ccol_p, crow_p, ccol_s, crow_s, c_in, n_in, m_in,
                h_p, c_p, n_p, m_p, h_s, c_out, n_out, m_out, *, n_prompt_seq, seg_len):
    @pl.when(pl.program_id(0) == 0)
    def _():
        c_p[...] = jnp.zeros_like(c_p)
        n_p[...] = jnp.zeros_like(n_p)
        m_p[...] = jnp.zeros_like(m_p)

    chains = []
    for a in range(n_prompt_seq):
        one = pl.ds(a, 1)
        c, n, m = c_p.at[one], n_p.at[one], m_p.at[one]
        chains.append(_mlstm_rows(qk_p.at[a], v_p.at[a], gcol_p.at[a], grow_p.at[a], brow_ref, bcol_ref, ccol_p, crow_p,
                                  c, n, m, c, n, m, h_p.at[a], PROMPT_ROWS, PROMPT_ROWS))
    chains.append(_mlstm_rows(qk_s, v_s, gcol_s, grow_s, brow_ref, bcol_ref, ccol_s, crow_s,
                              c_in, n_in, m_in, c_out, n_out, m_out, h_s, SAMPLE_ROWS, seg_len))
    _emit_staggered(chains)


def _gla_constants(rows, seg_len):
    t, seg, _, cum = _segment_structure(rows, seg_len)
    level_id = np.full((rows, rows), -1, np.int32)
    level_id[t, t] = 0
    for idx, hb in enumerate(LEVELS):
        if 2 * hb > seg_len:
            continue
        block = t // (2 * hb)
        upper = (t % (2 * hb)) >= hb
        level_id[(block[:, None] == block[None, :]) & upper[:, None] & ~upper[None, :]] = idx + 1
    segind = (seg[:, None] == np.arange(LANES)[None, :]).astype(np.float32)
    return jnp.asarray(cum.astype(np.float32), BF16), jnp.asarray(level_id), jnp.asarray(segind, BF16)


def _row_bcast(x, row, n):
    return jnp.broadcast_to(x[row:row + 1, :], (n, x.shape[1]))


def _gla_rows(qk_ref, v_ref, glr_ref, w2_ref, bg_ref, cum_ref, lvl_ref, segind_ref,
              s_in, s_out, o_ref, rows, seg_len):
    nseg = rows // seg_len
    ntile = rows // SUBLANES
    row_id = lax.broadcasted_iota(jnp.int32, (rows, DQK), 0)
    row_seg = row_id // seg_len
    row_in_tile = row_id % SUBLANES

    pre = _dot(glr_ref[...], w2_ref[...]) + bg_ref[...]
    g_all = _log_sigmoid(pre) * (1.0 / GATE_TAU)
    g_hi, g_lo = _split_hi_lo(g_all)
    cum = cum_ref[...]
    bc_all = jnp.dot(cum, g_hi, preferred_element_type=F32) + jnp.dot(cum, g_lo, preferred_element_type=F32)
    level_tiles = [lvl_ref[t * SUBLANES:(t + 1) * SUBLANES, :] for t in range(ntile)]
    segind = segind_ref[...]
    total_col = (lax.dot_general(g_hi, segind, (((0,), (0,)), ((), ())), preferred_element_type=F32)
                 + lax.dot_general(g_lo, segind, (((0,), (0,)), ((), ())), preferred_element_type=F32))
    yield

    qk = qk_ref[...]
    s_prev = [[s_in[s, h] for h in range(N_HEADS)] for s in range(nseg)]
    writes = []

    heads = range(N_HEADS)
    sl = [slice(h * DQK, (h + 1) * DQK) for h in heads]
    q = [qk[:, sl[h]] for h in heads]
    k = [qk[:, QK_WIDTH + h * DQK:QK_WIDTH + (h + 1) * DQK] for h in heads]
    v_bf = [v_ref[:, h * DV:(h + 1) * DV] for h in heads]
    g = [g_all[:, sl[h]] for h in heads]
    bc = [bc_all[:, sl[h]] for h in heads]
    seg_total = [jnp.concatenate([_row_bcast(bc[h], (s + 1) * seg_len - 1, seg_len) for s in range(nseg)], axis=0)
                 if nseg > 1 else _row_bcast(bc[h], rows - 1, rows) for h in heads]

    qq = [(q[h] * jnp.exp(bc[h])).astype(BF16) for h in heads]
    kk = [k[h] * jnp.exp(seg_total[h] - bc[h]) for h in heads]
    o_inter = []
    for h in heads:
        parts = []
        for s in range(nseg):
            st = s_prev[s][h]
            oi = jnp.dot(qq[h], st.astype(BF16), preferred_element_type=F32)
            parts.append(oi[s * seg_len:(s + 1) * seg_len] if nseg > 1 else oi)
            kk_s = kk[h] if nseg == 1 else jnp.where(row_seg == s, kk[h], 0.0)
            writes.append((s, h, st * jnp.exp(total_col[sl[h], s:s + 1]) + _dot_tn(kk_s, v_bf[h])))
        o_inter.append(parts[0] if nseg == 1 else jnp.concatenate(parts, axis=0))
    yield

    a = [[None] * ntile for h in heads]

    def put(h, t, level, z_rows):
        prev = 0.0 if a[h][t] is None else a[h][t]
        a[h][t] = jnp.where(level_tiles[t] == level, z_rows, prev)

    for h in heads:
        z = _dot_nt(q[h], k[h])
        for t in range(ntile):
            put(h, t, 0, z[t * SUBLANES:(t + 1) * SUBLANES])

    for idx, hb in enumerate(LEVELS):
        if 2 * hb > seg_len:
            continue
        level = idx + 1
        if hb >= SUBLANES:
            starts = range(0, rows, 2 * hb)
            for h in heads:
                e_parts, x_parts = [], []
                for r0 in starts:
                    ref = _row_bcast(bc[h], r0 + hb - 1, hb)
                    e_parts += [ref - bc[h][r0:r0 + hb], bc[h][r0 + hb:r0 + 2 * hb] - ref]
                    x_parts += [k[h][r0:r0 + hb], q[h][r0 + hb:r0 + 2 * hb]]
                u = jnp.concatenate(x_parts, axis=0) * jnp.exp(jnp.concatenate(e_parts, axis=0))
                u_upper = jnp.concatenate([u[r0 + hb:r0 + 2 * hb] for r0 in starts], axis=0)
                z = _dot_nt(u_upper, u)
                upper_tiles = [t for r0 in starts for t in range((r0 + hb) // SUBLANES, (r0 + 2 * hb) // SUBLANES)]
                for i, t in enumerate(upper_tiles):
                    put(h, t, level, z[i * SUBLANES:(i + 1) * SUBLANES])
        else:
            upper = (row_id % (2 * hb)) >= hb
            for h in heads:
                if hb == 1:
                    e_l = jnp.where(upper, g[h], 0.0)
                else:
                    tiles = range(ntile)
                    if hb == 4:
                        ref = jnp.concatenate([_row_bcast(bc[h], t * SUBLANES + 3, SUBLANES) for t in tiles], axis=0)
                    else:
                        lo = jnp.concatenate([_row_bcast(bc[h], t * SUBLANES + 1, SUBLANES) for t in tiles], axis=0)
                        hi = jnp.concatenate([_row_bcast(bc[h], t * SUBLANES + 5, SUBLANES) for t in tiles], axis=0)
                        ref = jnp.where(row_in_tile < 4, lo, hi)
                    d = bc[h] - ref
                    e_l = jnp.where(upper, d, -d)
                u = jnp.where(upper, q[h], k[h]) * jnp.exp(e_l)
                z = _dot_nt(u, u)
                for t in range(ntile):
                    put(h, t, level, z[t * SUBLANES:(t + 1) * SUBLANES])
        yield

    for h in heads:
        o_ref[:, h * DV:(h + 1) * DV] = jnp.dot(jnp.concatenate(a[h], axis=0).astype(BF16), v_bf[h],
                                                preferred_element_type=F32) + o_inter[h]

    for s, h, s_new in writes:
        s_out[s, h] = s_new
    yield


def _gla_body(qk_p, v_p, glr_p, qk_s, v_s, glr_s, w2_ref, bg_ref, cum_p, lvl_p, segind_p, cum_s, lvl_s, segind_s,
              s_in, o_p, s_p, o_s, s_out, *, n_prompt_seq, seg_len):
    @pl.when(pl.program_id(0) == 0)
    def _():
        s_p[...] = jnp.zeros_like(s_p)

    chains = []
    for a in range(n_prompt_seq):
        state = s_p.at[pl.ds(a, 1)]
        chains.append(_gla_rows(qk_p.at[a], v_p.at[a], glr_p.at[a], w2_ref, bg_ref, cum_p, lvl_p, segind_p,
                                state, state, o_p.at[a], PROMPT_ROWS, PROMPT_ROWS))
    chains.append(_gla_rows(qk_s, v_s, glr_s, w2_ref, bg_ref, cum_s, lvl_s, segind_s,
                            s_in, s_out, o_s, SAMPLE_ROWS, seg_len))
    _emit_staggered(chains)


def _mixer_params(n_axes):
    return pltpu.CompilerParams(dimension_semantics=("arbitrary",) * n_axes, vmem_limit_bytes=MIXER_VMEM_LIMIT)


def _state_spec(n, trailing):
    nd = 1 + len(trailing)
    return pl.BlockSpec((n,) + trailing, lambda *ids: (ids[0],) + (0,) * (nd - 1))


def _mlstm_mixer(prompt, sample, bias_i, bias_f, c0, n0, m0, n_prompt_seq, prompt_len):
    qk_p, v_p, gcol_p, grow_p = prompt
    qk_s, v_s, gcol_s, grow_s = sample
    n_sample_seq = c0.shape[0]
    seg_len = qk_s.shape[0] // n_sample_seq
    bias = jnp.concatenate([bias_i, bias_f]).astype(F32)
    bias_row = jnp.zeros((1, LANES), F32).at[0, :2 * N_HEADS].set(bias)
    bias_col = bias.reshape(2 * N_HEADS, 1)

    nb, plen = n_prompt_seq, prompt_len
    steps = plen // PROMPT_ROWS
    spb = SAMPLE_ROWS // seg_len
    assert n_sample_seq == steps * spb, "one block of sample sequences per prompt chunk step"
    ccol_p, crow_p = _mlstm_constants(PROMPT_ROWS, PROMPT_ROWS)
    ccol_s, crow_s = _mlstm_constants(SAMPLE_ROWS, seg_len)
    consts = [bias_row, bias_col, ccol_p, crow_p, ccol_s, crow_s]
    m0b = jnp.broadcast_to(m0[:, :, None], m0.shape + (LANES,))
    grow_p = jnp.transpose(grow_p.reshape(2 * N_HEADS, nb, plen), (1, 0, 2))
    grow_s = jnp.transpose(grow_s.reshape(2 * N_HEADS, steps, SAMPLE_ROWS), (1, 0, 2))
    seq_rows = lambda width: pl.BlockSpec((nb, PROMPT_ROWS, width), lambda i: (0, i, 0))
    blk_rows = lambda width: pl.BlockSpec((SAMPLE_ROWS, width), lambda i: (i, 0))
    whole = lambda shape: pl.BlockSpec(shape, lambda i: (0,) * len(shape))
    state_specs = [_state_spec(spb, (N_HEADS, DQK, DV)), _state_spec(spb, (N_HEADS, DQK)),
                   _state_spec(spb, (N_HEADS, LANES))]
    h_p, c_p, n_p, m_p, h_s, c_s, n_s, m_s = pl.pallas_call(
        functools.partial(_mlstm_body, n_prompt_seq=nb, seg_len=seg_len),
        grid=(steps,),
        in_specs=[seq_rows(2 * QK_WIDTH), seq_rows(V_WIDTH), seq_rows(LANES),
                  pl.BlockSpec((nb, 2 * N_HEADS, PROMPT_ROWS), lambda i: (0, 0, i)),
                  blk_rows(2 * QK_WIDTH), blk_rows(V_WIDTH), blk_rows(LANES),
                  pl.BlockSpec((1, 2 * N_HEADS, SAMPLE_ROWS), lambda i: (i, 0, 0))]
        + [_const_spec(a.shape) for a in consts] + state_specs,
        out_specs=[seq_rows(V_WIDTH), whole((nb, N_HEADS, DQK, DV)), whole((nb, N_HEADS, DQK)),
                   whole((nb, N_HEADS, LANES)), blk_rows(V_WIDTH)] + state_specs,
        out_shape=[
            jax.ShapeDtypeStruct((nb, plen, V_WIDTH), F32),
            jax.ShapeDtypeStruct((nb, N_HEADS, DQK, DV), F32),
            jax.ShapeDtypeStruct((nb, N_HEADS, DQK), F32),
            jax.ShapeDtypeStruct((nb, N_HEADS, LANES), F32),
            jax.ShapeDtypeStruct((n_sample_seq * seg_len, V_WIDTH), F32),
            jax.ShapeDtypeStruct(c0.shape, F32),
            jax.ShapeDtypeStruct(n0.shape, F32),
            jax.ShapeDtypeStruct(m0b.shape, F32),
        ],
        compiler_params=_mixer_params(1),
        name="mlstm_mixer",
    )(qk_p.reshape(nb, plen, -1), v_p.reshape(nb, plen, -1), gcol_p.reshape(nb, plen, -1), grow_p,
      qk_s, v_s, gcol_s, grow_s, *consts, c0, n0, m0b)

    return (h_p.reshape(nb * plen, V_WIDTH), h_s), (c_p, n_p, m_p[:, :, 0]), (c_s, n_s, m_s[:, :, 0])


def _gla_mixer(prompt, sample, w_gate2, b_gate, s0, n_prompt_seq, prompt_len):
    qk_p, v_p, glr_p = prompt
    qk_s, v_s, glr_s = sample
    n_sample_seq = s0.shape[0]
    seg_len = qk_s.shape[0] // n_sample_seq
    nb, plen = n_prompt_seq, prompt_len
    steps = plen // PROMPT_ROWS
    spb = SAMPLE_ROWS // seg_len
    assert n_sample_seq == steps * spb, "one block of sample sequences per prompt chunk step"
    w2 = jnp.zeros((LANES, QK_WIDTH), BF16).at[:GATE_RANK].set(w_gate2.astype(BF16))
    bg = b_gate.reshape(1, QK_WIDTH).astype(F32)
    consts = [w2, bg, *_gla_constants(PROMPT_ROWS, PROMPT_ROWS), *_gla_constants(SAMPLE_ROWS, seg_len)]
    seq_rows = lambda width: pl.BlockSpec((nb, PROMPT_ROWS, width), lambda i: (0, i, 0))
    blk_rows = lambda width: pl.BlockSpec((SAMPLE_ROWS, width), lambda i: (i, 0))
    state_spec = _state_spec(spb, (N_HEADS, DQK, DV))
    o_p, s_p, o_s, s_s = pl.pallas_call(
        functools.partial(_gla_body, n_prompt_seq=nb, seg_len=seg_len),
        grid=(steps,),
        in_specs=[seq_rows(2 * QK_WIDTH), seq_rows(V_WIDTH), seq_rows(LANES),
                  blk_rows(2 * QK_WIDTH), blk_rows(V_WIDTH), blk_rows(LANES)]
        + [_const_spec(a.shape) for a in consts] + [state_spec],
        out_specs=[seq_rows(V_WIDTH), pl.BlockSpec((nb, N_HEADS, DQK, DV), lambda i: (0, 0, 0, 0)),
                   blk_rows(V_WIDTH), state_spec],
        out_shape=[
            jax.ShapeDtypeStruct((nb, plen, V_WIDTH), F32),
            jax.ShapeDtypeStruct((nb, N_HEADS, DQK, DV), F32),
            jax.ShapeDtypeStruct((n_sample_seq * seg_len, V_WIDTH), F32),
            jax.ShapeDtypeStruct(s0.shape, F32),
        ],
        compiler_params=_mixer_params(1),
        name="gla_mixer",
    )(qk_p.reshape(nb, plen, -1), v_p.reshape(nb, plen, -1), glr_p.reshape(nb, plen, -1),
      qk_s, v_s, glr_s, *consts, s0)

    return (o_p.reshape(nb * plen, V_WIDTH), o_s), s_p, s_s


def _pad_gate_columns(w):
    return jnp.zeros((D_MODEL, LANES), BF16).at[:, :w.shape[1]].set(w.astype(BF16))


def kernel(x_prompt, x_sample, state_mlstm_C, state_mlstm_n, state_mlstm_m, state_gla_S, g_pre_mix, g_post_mix, g_pre_ffn, g_post_ffn, w_in_mlstm, b_i_mlstm, b_f_mlstm, g_head_mlstm, w_out_mlstm, w_in_gla, w_gate2_gla, b_gate_gla, g_head_gla, w_out_gla, w_ffn_gate, w_ffn_up, w_ffn_down):
    bp, sp, d = x_prompt.shape
    bs, ss, _ = x_sample.shape
    depth = g_pre_mix.shape[0]
    assert d == D_MODEL and sp % PROMPT_ROWS == 0 and SAMPLE_ROWS % ss == 0 and (bs * ss) % SAMPLE_ROWS == 0
    assert (bp * sp) % TM == 0 and (bs * ss) % TM == 0

    xp = x_prompt.reshape(bp * sp, d)
    xs = x_sample.reshape(bs * ss, d)
    vec = lambda g: g.reshape(1, D_MODEL).astype(F32)

    prompt_states = {"C": [], "n": [], "m": [], "S": []}
    sample_states = {"C": [], "n": [], "m": [], "S": []}
    ffn_gate, ffn_up, ffn_down = (w.astype(BF16) for w in (w_ffn_gate, w_ffn_up, w_ffn_down))
    in_mlstm, in_gla = w_in_mlstm.astype(BF16), w_in_gla.astype(BF16)
    ones = jnp.ones((QK_WIDTH,), F32)
    head_scale = jnp.full((QK_WIDTH,), DQK ** -0.5, F32)
    for layer in range(depth):
        j = layer // 2
        if layer % 2 == 0:
            project = lambda x: _inproj(x, vec(g_pre_mix[layer]), in_mlstm, j,
                                        _pad_gate_columns(w_in_mlstm[j][:, MAIN_WIDTH:]),
                                        jnp.concatenate([ones, head_scale]).reshape(1, -1),
                                        silu_gate=False, gate_rows=True)
            (qk_p, v_p, gate_p, gcol_p, grow_p), (qk_s, v_s, gate_s, gcol_s, grow_s) = project(xp), project(xs)
            h, st_p, st_s = _mlstm_mixer((qk_p, v_p, gcol_p, grow_p), (qk_s, v_s, gcol_s, grow_s),
                                         b_i_mlstm[j], b_f_mlstm[j],
                                         state_mlstm_C[j], state_mlstm_n[j], state_mlstm_m[j], bp, sp)
            for dst, st in ((prompt_states, st_p), (sample_states, st_s)):
                dst["C"].append(st[0]); dst["n"].append(st[1]); dst["m"].append(st[2])
            w_out, g_head = w_out_mlstm[j], g_head_mlstm[j]
        else:
            project = lambda x: _inproj(x, vec(g_pre_mix[layer]), in_gla, j,
                                        _pad_gate_columns(w_in_gla[j][:, MAIN_WIDTH:]),
                                        jnp.concatenate([head_scale, ones]).reshape(1, -1),
                                        silu_gate=True, gate_rows=False)
            (qk_p, v_p, gate_p, glr_p), (qk_s, v_s, gate_s, glr_s) = project(xp), project(xs)
            h, s_p, s_s = _gla_mixer((qk_p, v_p, glr_p), (qk_s, v_s, glr_s), w_gate2_gla[j], b_gate_gla[j],
                                     state_gla_S[j], bp, sp)
            prompt_states["S"].append(s_p)
            sample_states["S"].append(s_s)
            w_out, g_head = w_out_gla[j], g_head_gla[j]
        xp, xs = _block_tail(h[0], h[1], gate_p, gate_s, g_head.reshape(1, V_WIDTH).astype(F32), w_out.astype(BF16),
                             xp, xs,
                             vec(g_post_mix[layer]), vec(g_pre_ffn[layer]), vec(g_post_ffn[layer]),
                             ffn_gate, ffn_up, ffn_down, layer)

    stack = lambda xs_: jnp.stack(xs_)
    return (xp.reshape(bp, sp, d), xs.reshape(bs, ss, d),
            stack(prompt_states["C"]), stack(prompt_states["n"]), stack(prompt_states["m"]), stack(prompt_states["S"]),
            stack(sample_states["C"]), stack(sample_states["n"]), stack(sample_states["m"]), stack(sample_states["S"]))
```

```python
import functools

import numpy as np
import jax
import jax.numpy as jnp
from jax import lax
from jax.experimental import pallas as pl
from jax.experimental.pallas import tpu as pltpu

F32 = jnp.float32
BF16 = jnp.bfloat16

D_MODEL = 1024
N_HEADS = 4
DQK = 128
DV = 256
QK_WIDTH = N_HEADS * DQK
V_WIDTH = N_HEADS * DV
MAIN_WIDTH = 2 * QK_WIDTH + 2 * V_WIDTH
GATE_RANK = 16
GATE_TAU = 16.0
EPS = 1e-6
LANES = 128
SUBLANES = 8
MXU_DIM = 256
VMEM_LIMIT = 56 * 1024 * 1024
MIXER_VMEM_LIMIT = 62 * 1024 * 1024

PROMPT_ROWS = 128
SAMPLE_ROWS = 64
LEVELS = (64, 32, 16, 8, 4, 2, 1)
TM = 512
TM_IN = 1024
GROUP_ROWS = 256
FF_SPLITS = 2


def _dot(a, b):
    return jnp.dot(a.astype(BF16), b.astype(BF16), preferred_element_type=F32)


def _dot_nt(a, b):
    return lax.dot_general(a.astype(BF16), b.astype(BF16), (((1,), (1,)), ((), ())),
                           preferred_element_type=F32)


def _dot_tn(a, b):
    return lax.dot_general(a.astype(BF16), b.astype(BF16), (((0,), (0,)), ((), ())),
                           preferred_element_type=F32)


def _split_hi_lo(x):
    hi = x.astype(BF16)
    lo = (x - hi.astype(F32)).astype(BF16)
    return hi, lo


def _log_sigmoid(x):
    return jnp.minimum(x, 0.0) - jnp.log(1.0 + jnp.exp(-jnp.abs(x)))


def _sigmoid(x):
    return 1.0 / (1.0 + jnp.exp(-x))


def _rms(x, gain):
    return x * lax.rsqrt(jnp.mean(x * x, axis=-1, keepdims=True) + EPS) * gain


def _rows_per_segment(values, seg_len):
    parts = [jnp.broadcast_to(v, (seg_len, v.shape[1])) for v in values]
    return parts[0] if len(parts) == 1 else jnp.concatenate(parts, axis=0)


def _emit_staggered(chains, lag=1):
    pending, live, step = list(chains), [], 0
    while pending or live:
        while pending and (lag == 0 or step % lag == 0):
            live.append(pending.pop(0))
            if lag:
                break
        step += 1
        for chain in list(live):
            if next(chain, StopIteration) is StopIteration:
                live.remove(chain)


def _const_spec(shape):
    nd = len(shape)
    return pl.BlockSpec(shape, lambda *_: (0,) * nd)


def _two_group_specs(n_prompt_blocks, width, single_buffer_sample=False):
    prompt = pl.BlockSpec((TM, width), lambda i: (jnp.minimum(i, n_prompt_blocks - 1), 0))
    mode = dict(pipeline_mode=pl.Buffered(1)) if single_buffer_sample else {}
    sample = pl.BlockSpec((TM, width), lambda i: (jnp.maximum(i - n_prompt_blocks, 0), 0), **mode)
    return prompt, sample


def _resident_spec(shape, layer=None):
    nd = len(shape)
    if layer is None:
        return pl.BlockSpec(shape, lambda i: (0,) * nd, pipeline_mode=pl.Buffered(1))
    return pl.BlockSpec((None,) + tuple(shape), lambda i: (layer,) + (0,) * nd, pipeline_mode=pl.Buffered(1))


def _inproj_body(x_ref, gain_ref, w_ref, wg_ref, scale_ref, oqk_ref, ov_ref, ogate_ref, og_ref,
                 *maybe_grow_ref, silu_gate):
    n_groups = x_ref.shape[0] // GROUP_ROWS
    groups = [slice(a * GROUP_ROWS, (a + 1) * GROUP_ROWS) for a in range(n_groups)]
    h = [None] * n_groups
    for a in range(n_groups + 1):
        if a < n_groups:
            h[a] = _rms(x_ref[groups[a], :], gain_ref[...]).astype(BF16)
        if a > 0:
            rows, hb = groups[a - 1], h[a - 1]
            narrow = jnp.dot(hb, wg_ref[...], preferred_element_type=F32)
            og_ref[rows, :] = narrow
            for grow_ref in maybe_grow_ref:
                grow_ref[:, rows] = narrow.T[0:SUBLANES, :]
            qk = jnp.dot(hb, w_ref[:, 0:2 * QK_WIDTH], preferred_element_type=F32)
            oqk_ref[rows, :] = qk * scale_ref[...]
            v = jnp.dot(hb, w_ref[:, 2 * QK_WIDTH:2 * QK_WIDTH + V_WIDTH], preferred_element_type=F32)
            ov_ref[rows, :] = v.astype(BF16)
            gate = jnp.dot(hb, w_ref[:, 2 * QK_WIDTH + V_WIDTH:MAIN_WIDTH], preferred_element_type=F32)
            ogate_ref[rows, :] = gate * _sigmoid(gate) if silu_gate else _sigmoid(gate)


def _inproj(x, gain, w_in, layer, w_gate, qk_scale, silu_gate, gate_rows):
    m = x.shape[0]
    assert m % TM_IN == 0
    rows_spec = lambda width: pl.BlockSpec((TM_IN, width), lambda i: (i, 0))
    out_specs = [rows_spec(2 * QK_WIDTH), rows_spec(V_WIDTH), rows_spec(V_WIDTH), rows_spec(LANES)]
    out_shape = [
        jax.ShapeDtypeStruct((m, 2 * QK_WIDTH), F32),
        jax.ShapeDtypeStruct((m, V_WIDTH), BF16),
        jax.ShapeDtypeStruct((m, V_WIDTH), F32),
        jax.ShapeDtypeStruct((m, LANES), F32),
    ]
    if gate_rows:
        out_specs.append(pl.BlockSpec((SUBLANES, TM_IN), lambda i: (0, i)))
        out_shape.append(jax.ShapeDtypeStruct((SUBLANES, m), F32))
    return pl.pallas_call(
        functools.partial(_inproj_body, silu_gate=silu_gate),
        grid=(m // TM_IN,),
        in_specs=[rows_spec(D_MODEL), _resident_spec((1, D_MODEL)), _resident_spec(w_in.shape[1:], layer),
                  _resident_spec((D_MODEL, LANES)), _resident_spec((1, 2 * QK_WIDTH))],
        out_specs=out_specs,
        out_shape=out_shape,
        compiler_params=pltpu.CompilerParams(dimension_semantics=("arbitrary",), vmem_limit_bytes=VMEM_LIMIT),
        name="inproj",
    )(x, gain, w_in, w_gate, qk_scale)


def _tail_body(hp_ref, hs_ref, gp_ref, gs_ref, ghead_ref, wo_ref, xp_ref, xs_ref, gpm_ref, gpf_ref, gqf_ref,
               wg_ref, wu_ref, wd_ref, op_ref, os_ref, out_scr, *, n_prompt_blocks):
    i = pl.program_id(0)
    is_prompt = i < n_prompt_blocks
    d_ff = wg_ref.shape[1]
    n_tiles = d_ff // MXU_DIM
    bounds = [MXU_DIM * ((n_tiles * t + FF_SPLITS - 1) // FF_SPLITS) for t in range(FF_SPLITS + 1)]

    def stages(rows):
        raw = jnp.where(is_prompt, hp_ref[rows, :], hs_ref[rows, :])
        gate = jnp.where(is_prompt, gp_ref[rows, :], gs_ref[rows, :])
        gain = ghead_ref[...]
        y = jnp.concatenate(
            [gate[:, h * DV:(h + 1) * DV] * _rms(raw[:, h * DV:(h + 1) * DV], gain[:, h * DV:(h + 1) * DV])
             for h in range(N_HEADS)], axis=1).astype(BF16)
        yield
        mix = jnp.dot(y, wo_ref[...], preferred_element_type=F32)
        yield
        x1 = jnp.where(is_prompt, xp_ref[rows, :], xs_ref[rows, :]) + _rms(mix, gpm_ref[...])
        h = _rms(x1, gpf_ref[...]).astype(BF16)
        yield
        ffn = None
        for t in range(FF_SPLITS):
            cols = slice(bounds[t], bounds[t + 1])
            gate = jnp.dot(h, wg_ref[:, cols], preferred_element_type=F32)
            up = jnp.dot(h, wu_ref[:, cols], preferred_element_type=F32)
            yield
            act = (gate * _sigmoid(gate) * up).astype(BF16)
            yield
            part = jnp.dot(act, wd_ref[cols, :], preferred_element_type=F32)
            ffn = part if ffn is None else ffn + part
            yield
        out_scr[rows, :] = x1 + _rms(ffn, gqf_ref[...])
        yield

    _emit_staggered([stages(slice(a * GROUP_ROWS, (a + 1) * GROUP_ROWS)) for a in range(TM // GROUP_ROWS)])

    @pl.when(is_prompt)
    def _():
        op_ref[...] = out_scr[...]

    @pl.when(jnp.logical_not(is_prompt))
    def _():
        os_ref[...] = out_scr[...]


def _block_tail(hp, hs, gate_p, gate_s, ghead, w_out, xp, xs, g_post_mix, g_pre_ffn, g_post_ffn,
                w_gate, w_up, w_down, layer):
    npb = xp.shape[0] // TM
    m = xp.shape[0] + xs.shape[0]
    d_ff = w_gate.shape[2]
    assert d_ff % MXU_DIM == 0
    in_p, in_s = _two_group_specs(npb, D_MODEL, single_buffer_sample=True)
    vec = _resident_spec((1, D_MODEL))
    return pl.pallas_call(
        functools.partial(_tail_body, n_prompt_blocks=npb),
        grid=(m // TM,),
        in_specs=[
            in_p, in_s,
            in_p, in_s,
            _resident_spec((1, V_WIDTH)),
            _resident_spec((V_WIDTH, D_MODEL)),
            in_p, in_s,
            vec, vec, vec,
            _resident_spec((D_MODEL, d_ff), layer), _resident_spec((D_MODEL, d_ff), layer),
            _resident_spec((d_ff, D_MODEL), layer),
        ],
        out_specs=list(_two_group_specs(npb, D_MODEL)),
        out_shape=[jax.ShapeDtypeStruct(xp.shape, F32), jax.ShapeDtypeStruct(xs.shape, F32)],
        scratch_shapes=[pltpu.VMEM((TM, D_MODEL), F32)],
        compiler_params=pltpu.CompilerParams(dimension_semantics=("arbitrary",), vmem_limit_bytes=VMEM_LIMIT),
        name="block_tail",
    )(hp, hs, gate_p, gate_s, ghead, w_out, xp, xs, g_post_mix, g_pre_ffn, g_post_ffn, w_gate, w_up, w_down)


def _segment_structure(rows, seg_len):
    t = np.arange(rows)
    seg = t // seg_len
    same = seg[:, None] == seg[None, :]
    cum = same & (t[None, :] <= t[:, None])
    return t, seg, same, cum


def _mlstm_constants(rows, seg_len):
    _, _, same, cum = _segment_structure(rows, seg_len)
    col = np.concatenate([cum, same], axis=0).astype(np.float32)
    row = cum.T.astype(np.float32)
    return jnp.asarray(col, BF16), jnp.asarray(row, BF16)


def _mlstm_rows(qk_ref, v_ref, gcol_ref, grow_ref, bias_row_ref, bias_col_ref, ccol_ref, crow_ref,
                c_in, n_in, m_in, c_out, n_out, m_out, h_ref, rows, seg_len):
    nseg = rows // seg_len
    r = lax.broadcasted_iota(jnp.int32, (rows, rows), 0)
    c = lax.broadcasted_iota(jnp.int32, (rows, rows), 1)
    same = None if nseg == 1 else (r // seg_len) == (c // seg_len)
    lower = (c <= r) if nseg == 1 else same & (c <= r)
    row_seg = lax.broadcasted_iota(jnp.int32, (rows, DQK), 0) // seg_len

    pre_col = gcol_ref[...] + bias_row_ref[...]
    grow = grow_ref[0] if len(grow_ref.shape) == 3 else grow_ref[...]
    pre_row = grow + bias_col_ref[...]
    ccol = ccol_ref[...]
    crow = crow_ref[...]
    hi, lo = _split_hi_lo(_log_sigmoid(pre_col))
    sums_col = jnp.dot(ccol, hi, preferred_element_type=F32) + jnp.dot(ccol, lo, preferred_element_type=F32)
    hi, lo = _split_hi_lo(_log_sigmoid(pre_row))
    sums_row = jnp.dot(hi, crow, preferred_element_type=F32) + jnp.dot(lo, crow, preferred_element_type=F32)
    yield

    qk = qk_ref[...]
    c_prev = [[c_in[s, h] for h in range(N_HEADS)] for s in range(nseg)]
    n_state = [[n_in[s, h:h + 1, :] for h in range(N_HEADS)] for s in range(nseg)]
    m_state = [[m_in[s, h:h + 1, 0:1] for h in range(N_HEADS)] for s in range(nseg)]
    writes = []

    heads = range(N_HEADS)
    q = [qk[:, h * DQK:(h + 1) * DQK] for h in heads]
    k = [qk[:, QK_WIDTH + h * DQK:QK_WIDTH + (h + 1) * DQK] for h in heads]
    v_bf = [v_ref[:, h * DV:(h + 1) * DV] for h in heads]
    i_col = [pre_col[:, h:h + 1] for h in heads]
    b_col = [sums_col[0:rows, N_HEADS + h:N_HEADS + h + 1] for h in heads]
    b_last = [sums_col[rows:2 * rows, N_HEADS + h:N_HEADS + h + 1] for h in heads]
    src = [pre_row[h:h + 1, :] - sums_row[N_HEADS + h:N_HEADS + h + 1, :] for h in heads]
    m_prev = [_rows_per_segment([m_state[s][h] for s in range(nseg)], seg_len) for h in heads]
    n_prev = [_rows_per_segment([n_state[s][h] for s in range(nseg)], seg_len) for h in heads]

    dmat = [jnp.where(lower, b_col[h] + src[h], -jnp.inf) for h in heads]
    inter = [b_col[h] + m_prev[h] for h in heads]
    m_t = [jnp.maximum(inter[h], jnp.max(dmat[h], axis=1, keepdims=True)) for h in heads]
    dlast = [b_last[h] + src[h] for h in heads]
    if nseg > 1:
        dlast = [jnp.where(same, d, -jnp.inf) for d in dlast]
    m_new = [jnp.maximum(b_last[h] + m_prev[h], jnp.max(dlast[h], axis=1, keepdims=True)) for h in heads]
    yield
    scores = [_dot_nt(q[h], k[h]) for h in heads]
    q_bf = [q[h].astype(BF16) for h in heads]
    qc = []
    for h in heads:
        parts = []
        for s in range(nseg):
            full = jnp.dot(q_bf[h], c_prev[s][h].astype(BF16), preferred_element_type=F32)
            parts.append(full[s * seg_len:(s + 1) * seg_len] if nseg > 1 else full)
        qc.append(parts[0] if nseg == 1 else jnp.concatenate(parts, axis=0))
    qn = [jnp.sum(q[h] * n_prev[h], axis=1, keepdims=True) for h in heads]
    yield

    w_inter = [jnp.exp(inter[h] - m_t[h]) for h in heads]
    a = [scores[h] * jnp.exp(dmat[h] - m_t[h]) for h in heads]
    w_s = [jnp.exp(b_last[h] - b_col[h] + i_col[h] - m_new[h]) for h in heads]
    decay = [jnp.exp(b_last[h] + m_prev[h] - m_new[h]) for h in heads]
    kw = [k[h] * w_s[h] for h in heads]
    yield

    num = [jnp.dot(a[h].astype(BF16), v_bf[h], preferred_element_type=F32) + w_inter[h] * qc[h] for h in heads]
    den = [jnp.sum(a[h], axis=1, keepdims=True) + w_inter[h] * qn[h] for h in heads]
    for h in heads:
        for s in range(nseg):
            kw_s = kw[h] if nseg == 1 else jnp.where(row_seg == s, kw[h], 0.0)
            dec = decay[h][s * seg_len:s * seg_len + 1, :]
            writes.append((s, h, dec * c_prev[s][h] + _dot_tn(kw_s, v_bf[h]),
                           dec * n_state[s][h] + jnp.sum(kw_s, axis=0, keepdims=True),
                           jnp.broadcast_to(m_new[h][s * seg_len:s * seg_len + 1, :], (1, LANES))))

    yield
    for h in heads:
        h_ref[:, h * DV:(h + 1) * DV] = num[h] * (1.0 / jnp.maximum(jnp.abs(den[h]), jnp.exp(-m_t[h])))

    for s, h, c_new, n_new, m_new in writes:
        c_out[s, h] = c_new
        n_out[s, h:h + 1, :] = n_new
        m_out[s, h:h + 1, :] = m_new
    yield


def _mlstm_body(qk_p, v_p, gcol_p, grow_p, qk_s, v_s, gcol_s, grow_s, brow_ref, bcol_ref,
                ccol_p, crow_p, ccol_s, crow_s, c_in, n_in, m_in,
                h_p, c_p, n_p, m_p, h_s, c_out, n_out, m_out, *, n_prompt_seq, seg_len):
    @pl.when(pl.program_id(0) == 0)
    def _():
        c_p[...] = jnp.zeros_like(c_p)
        n_p[...] = jnp.zeros_like(n_p)
        m_p[...] = jnp.zeros_like(m_p)

    chains = []
    for a in range(n_prompt_seq):
        one = pl.ds(a, 1)
        c, n, m = c_p.at[one], n_p.at[one], m_p.at[one]
        chains.append(_mlstm_rows(qk_p.at[a], v_p.at[a], gcol_p.at[a], grow_p.at[a], brow_ref, bcol_ref, ccol_p, crow_p,
                                  c, n, m, c, n, m, h_p.at[a], PROMPT_ROWS, PROMPT_ROWS))
    chains.append(_mlstm_rows(qk_s, v_s, gcol_s, grow_s, brow_ref, bcol_ref, ccol_s, crow_s,
                              c_in, n_in, m_in, c_out, n_out, m_out, h_s, SAMPLE_ROWS, seg_len))
    _emit_staggered(chains)


def _gla_constants(rows, seg_len):
    t, seg, _, cum = _segment_structure(rows, seg_len)
    level_id = np.full((rows, rows), -1, np.int32)
    level_id[t, t] = 0
    for idx, hb in enumerate(LEVELS):
        if 2 * hb > seg_len:
            continue
        block = t // (2 * hb)
        upper = (t % (2 * hb)) >= hb
        level_id[(block[:, None] == block[None, :]) & upper[:, None] & ~upper[None, :]] = idx + 1
    segind = (seg[:, None] == np.arange(LANES)[None, :]).astype(np.float32)
    return jnp.asarray(cum.astype(np.float32), BF16), jnp.asarray(level_id), jnp.asarray(segind, BF16)


def _row_bcast(x, row, n):
    return jnp.broadcast_to(x[row:row + 1, :], (n, x.shape[1]))


def _gla_rows(qk_ref, v_ref, glr_ref, w2_ref, bg_ref, cum_ref, lvl_ref, segind_ref,
              s_in, s_out, o_ref, rows, seg_len):
    nseg = rows // seg_len
    ntile = rows // SUBLANES
    row_id = lax.broadcasted_iota(jnp.int32, (rows, DQK), 0)
    row_seg = row_id // seg_len
    row_in_tile = row_id % SUBLANES

    pre = _dot(glr_ref[...], w2_ref[...]) + bg_ref[...]
    g_all = _log_sigmoid(pre) * (1.0 / GATE_TAU)
    g_hi, g_lo = _split_hi_lo(g_all)
    cum = cum_ref[...]
    bc_all = jnp.dot(cum, g_hi, preferred_element_type=F32) + jnp.dot(cum, g_lo, preferred_element_type=F32)
    level_tiles = [lvl_ref[t * SUBLANES:(t + 1) * SUBLANES, :] for t in range(ntile)]
    segind = segind_ref[...]
    total_col = (lax.dot_general(g_hi, segind, (((0,), (0,)), ((), ())), preferred_element_type=F32)
                 + lax.dot_general(g_lo, segind, (((0,), (0,)), ((), ())), preferred_element_type=F32))
    yield

    qk = qk_ref[...]
    s_prev = [[s_in[s, h] for h in range(N_HEADS)] for s in range(nseg)]
    writes = []

    heads = range(N_HEADS)
    sl = [slice(h * DQK, (h + 1) * DQK) for h in heads]
    q = [qk[:, sl[h]] for h in heads]
    k = [qk[:, QK_WIDTH + h * DQK:QK_WIDTH + (h + 1) * DQK] for h in heads]
    v_bf = [v_ref[:, h * DV:(h + 1) * DV] for h in heads]
    g = [g_all[:, sl[h]] for h in heads]
    bc = [bc_all[:, sl[h]] for h in heads]
    seg_total = [jnp.concatenate([_row_bcast(bc[h], (s + 1) * seg_len - 1, seg_len) for s in range(nseg)], axis=0)
                 if nseg > 1 else _row_bcast(bc[h], rows - 1, rows) for h in heads]

    qq = [(q[h] * jnp.exp(bc[h])).astype(BF16) for h in heads]
    kk = [k[h] * jnp.exp(seg_total[h] - bc[h]) for h in heads]
    o_inter = []
    for h in heads:
        parts = []
        for s in range(nseg):
            st = s_prev[s][h]
            oi = jnp.dot(qq[h], st.astype(BF16), preferred_element_type=F32)
            parts.append(oi[s * seg_len:(s + 1) * seg_len] if nseg > 1 else oi)
            kk_s = kk[h] if nseg == 1 else jnp.where(row_seg == s, kk[h], 0.0)
            writes.append((s, h, st * jnp.exp(total_col[sl[h], s:s + 1]) + _dot_tn(kk_s, v_bf[h])))
        o_inter.append(parts[0] if nseg == 1 else jnp.concatenate(parts, axis=0))
    yield

    a = [[None] * ntile for h in heads]

    def put(h, t, level, z_rows):
        prev = 0.0 if a[h][t] is None else a[h][t]
        a[h][t] = jnp.where(level_tiles[t] == level, z_rows, prev)

    for h in heads:
        z = _dot_nt(q[h], k[h])
        for t in range(ntile):
            put(h, t, 0, z[t * SUBLANES:(t + 1) * SUBLANES])

    for idx, hb in enumerate(LEVELS):
        if 2 * hb > seg_len:
            continue
        level = idx + 1
        if hb >= SUBLANES:
            starts = range(0, rows, 2 * hb)
            for h in heads:
                e_parts, x_parts = [], []
                for r0 in starts:
                    ref = _row_bcast(bc[h], r0 + hb - 1, hb)
                    e_parts += [ref - bc[h][r0:r0 + hb], bc[h][r0 + hb:r0 + 2 * hb] - ref]
                    x_parts += [k[h][r0:r0 + hb], q[h][r0 + hb:r0 + 2 * hb]]
                u = jnp.concatenate(x_parts, axis=0) * jnp.exp(jnp.concatenate(e_parts, axis=0))
                u_upper = jnp.concatenate([u[r0 + hb:r0 + 2 * hb] for r0 in starts], axis=0)
                z = _dot_nt(u_upper, u)
                upper_tiles = [t for r0 in starts for t in range((r0 + hb) // SUBLANES, (r0 + 2 * hb) // SUBLANES)]
                for i, t in enumerate(upper_tiles):
                    put(h, t, level, z[i * SUBLANES:(i + 1) * SUBLANES])
        else:
            upper = (row_id % (2 * hb)) >= hb
            for h in heads:
                if hb == 1:
                    e_l = jnp.where(upper, g[h], 0.0)
                else:
                    tiles = range(ntile)
                    if hb == 4:
                        ref = jnp.concatenate([_row_bcast(bc[h], t * SUBLANES + 3, SUBLANES) for t in tiles], axis=0)
                    else:
                        lo = jnp.concatenate([_row_bcast(bc[h], t * SUBLANES + 1, SUBLANES) for t in tiles], axis=0)
                        hi = jnp.concatenate([_row_bcast(bc[h], t * SUBLANES + 5, SUBLANES) for t in tiles], axis=0)
                        ref = jnp.where(row_in_tile < 4, lo, hi)
                    d = bc[h] - ref
                    e_l = jnp.where(upper, d, -d)
                u = jnp.where(upper, q[h], k[h]) * jnp.exp(e_l)
                z = _dot_nt(u, u)
                for t in range(ntile):
                    put(h, t, level, z[t * SUBLANES:(t + 1) * SUBLANES])
        yield

    for h in heads:
        o_ref[:, h * DV:(h + 1) * DV] = jnp.dot(jnp.concatenate(a[h], axis=0).astype(BF16), v_bf[h],
                                                preferred_element_type=F32) + o_inter[h]

    for s, h, s_new in writes:
        s_out[s, h] = s_new
    yield


def _gla_body(qk_p, v_p, glr_p, qk_s, v_s, glr_s, w2_ref, bg_ref, cum_p, lvl_p, segind_p, cum_s, lvl_s, segind_s,
              s_in, o_p, s_p, o_s, s_out, *, n_prompt_seq, seg_len):
    @pl.when(pl.program_id(0) == 0)
    def _():
        s_p[...] = jnp.zeros_like(s_p)

    chains = []
    for a in range(n_prompt_seq):
        state = s_p.at[pl.ds(a, 1)]
        chains.append(_gla_rows(qk_p.at[a], v_p.at[a], glr_p.at[a], w2_ref, bg_ref, cum_p, lvl_p, segind_p,
                                state, state, o_p.at[a], PROMPT_ROWS, PROMPT_ROWS))
    chains.append(_gla_rows(qk_s, v_s, glr_s, w2_ref, bg_ref, cum_s, lvl_s, segind_s,
                            s_in, s_out, o_s, SAMPLE_ROWS, seg_len))
    _emit_staggered(chains)


def _mixer_params(n_axes):
    return pltpu.CompilerParams(dimension_semantics=("arbitrary",) * n_axes, vmem_limit_bytes=MIXER_VMEM_LIMIT)


def _state_spec(n, trailing):
    nd = 1 + len(trailing)
    return pl.BlockSpec((n,) + trailing, lambda *ids: (ids[0],) + (0,) * (nd - 1))


def _mlstm_mixer(prompt, sample, bias_i, bias_f, c0, n0, m0, n_prompt_seq, prompt_len):
    qk_p, v_p, gcol_p, grow_p = prompt
    qk_s, v_s, gcol_s, grow_s = sample
    n_sample_seq = c0.shape[0]
    seg_len = qk_s.shape[0] // n_sample_seq
    bias = jnp.concatenate([bias_i, bias_f]).astype(F32)
    bias_row = jnp.zeros((1, LANES), F32).at[0, :2 * N_HEADS].set(bias)
    bias_col = bias.reshape(2 * N_HEADS, 1)

    nb, plen = n_prompt_seq, prompt_len
    steps = plen // PROMPT_ROWS
    spb = SAMPLE_ROWS // seg_len
    assert n_sample_seq == steps * spb, "one block of sample sequences per prompt chunk step"
    ccol_p, crow_p = _mlstm_constants(PROMPT_ROWS, PROMPT_ROWS)
    ccol_s, crow_s = _mlstm_constants(SAMPLE_ROWS, seg_len)
    consts = [bias_row, bias_col, ccol_p, crow_p, ccol_s, crow_s]
    m0b = jnp.broadcast_to(m0[:, :, None], m0.shape + (LANES,))
    grow_p = jnp.transpose(grow_p.reshape(2 * N_HEADS, nb, plen), (1, 0, 2))
    grow_s = jnp.transpose(grow_s.reshape(2 * N_HEADS, steps, SAMPLE_ROWS), (1, 0, 2))
    seq_rows = lambda width: pl.BlockSpec((nb, PROMPT_ROWS, width), lambda i: (0, i, 0))
    blk_rows = lambda width: pl.BlockSpec((SAMPLE_ROWS, width), lambda i: (i, 0))
    whole = lambda shape: pl.BlockSpec(shape, lambda i: (0,) * len(shape))
    state_specs = [_state_spec(spb, (N_HEADS, DQK, DV)), _state_spec(spb, (N_HEADS, DQK)),
                   _state_spec(spb, (N_HEADS, LANES))]
    h_p, c_p, n_p, m_p, h_s, c_s, n_s, m_s = pl.pallas_call(
        functools.partial(_mlstm_body, n_prompt_seq=nb, seg_len=seg_len),
        grid=(steps,),
        in_specs=[seq_rows(2 * QK_WIDTH), seq_rows(V_WIDTH), seq_rows(LANES),
                  pl.BlockSpec((nb, 2 * N_HEADS, PROMPT_ROWS), lambda i: (0, 0, i)),
                  blk_rows(2 * QK_WIDTH), blk_rows(V_WIDTH), blk_rows(LANES),
                  pl.BlockSpec((1, 2 * N_HEADS, SAMPLE_ROWS), lambda i: (i, 0, 0))]
        + [_const_spec(a.shape) for a in consts] + state_specs,
        out_specs=[seq_rows(V_WIDTH), whole((nb, N_HEADS, DQK, DV)), whole((nb, N_HEADS, DQK)),
                   whole((nb, N_HEADS, LANES)), blk_rows(V_WIDTH)] + state_specs,
        out_shape=[
            jax.ShapeDtypeStruct((nb, plen, V_WIDTH), F32),
            jax.ShapeDtypeStruct((nb, N_HEADS, DQK, DV), F32),
            jax.ShapeDtypeStruct((nb, N_HEADS, DQK), F32),
            jax.ShapeDtypeStruct((nb, N_HEADS, LANES), F32),
            jax.ShapeDtypeStruct((n_sample_seq * seg_len, V_WIDTH), F32),
            jax.ShapeDtypeStruct(c0.shape, F32),
            jax.ShapeDtypeStruct(n0.shape, F32),
            jax.ShapeDtypeStruct(m0b.shape, F32),
        ],
        compiler_params=_mixer_params(1),
        name="mlstm_mixer",
    )(qk_p.reshape(nb, plen, -1), v_p.reshape(nb, plen, -1), gcol_p.reshape(nb, plen, -1), grow_p,
      qk_s, v_s, gcol_s, grow_s, *consts, c0, n0, m0b)

    return (h_p.reshape(nb * plen, V_WIDTH), h_s), (c_p, n_p, m_p[:, :, 0]), (c_s, n_s, m_s[:, :, 0])


def _gla_mixer(prompt, sample, w_gate2, b_gate, s0, n_prompt_seq, prompt_len):
    qk_p, v_p, glr_p = prompt
    qk_s, v_s, glr_s = sample
    n_sample_seq = s0.shape[0]
    seg_len = qk_s.shape[0] // n_sample_seq
    nb, plen = n_prompt_seq, prompt_len
    steps = plen // PROMPT_ROWS
    spb = SAMPLE_ROWS // seg_len
    assert n_sample_seq == steps * spb, "one block of sample sequences per prompt chunk step"
    w2 = jnp.zeros((LANES, QK_WIDTH), BF16).at[:GATE_RANK].set(w_gate2.astype(BF16))
    bg = b_gate.reshape(1, QK_WIDTH).astype(F32)
    consts = [w2, bg, *_gla_constants(PROMPT_ROWS, PROMPT_ROWS), *_gla_constants(SAMPLE_ROWS, seg_len)]
    seq_rows = lambda width: pl.BlockSpec((nb, PROMPT_ROWS, width), lambda i: (0, i, 0))
    blk_rows = lambda width: pl.BlockSpec((SAMPLE_ROWS, width), lambda i: (i, 0))
    state_spec = _state_spec(spb, (N_HEADS, DQK, DV))
    o_p, s_p, o_s, s_s = pl.pallas_call(
        functools.partial(_gla_body, n_prompt_seq=nb, seg_len=seg_len),
        grid=(steps,),
        in_specs=[seq_rows(2 * QK_WIDTH), seq_rows(V_WIDTH), seq_rows(LANES),
                  blk_rows(2 * QK_WIDTH), blk_rows(V_WIDTH), blk_rows(LANES)]
        + [_const_spec(a.shape) for a in consts] + [state_spec],
        out_specs=[seq_rows(V_WIDTH), pl.BlockSpec((nb, N_HEADS, DQK, DV), lambda i: (0, 0, 0, 0)),
                   blk_rows(V_WIDTH), state_spec],
        out_shape=[
            jax.ShapeDtypeStruct((nb, plen, V_WIDTH), F32),
            jax.ShapeDtypeStruct((nb, N_HEADS, DQK, DV), F32),
            jax.ShapeDtypeStruct((n_sample_seq * seg_len, V_WIDTH), F32),
            jax.ShapeDtypeStruct(s0.shape, F32),
        ],
        compiler_params=_mixer_params(1),
        name="gla_mixer",
    )(qk_p.reshape(nb, plen, -1), v_p.reshape(nb, plen, -1), glr_p.reshape(nb, plen, -1),
      qk_s, v_s, glr_s, *consts, s0)

    return (o_p.reshape(nb * plen, V_WIDTH), o_s), s_p, s_s


def _pad_gate_columns(w):
    return jnp.zeros((D_MODEL, LANES), BF16).at[:, :w.shape[1]].set(w.astype(BF16))


def kernel(x_prompt, x_sample, state_mlstm_C, state_mlstm_n, state_mlstm_m, state_gla_S, g_pre_mix, g_post_mix, g_pre_ffn, g_post_ffn, w_in_mlstm, b_i_mlstm, b_f_mlstm, g_head_mlstm, w_out_mlstm, w_in_gla, w_gate2_gla, b_gate_gla, g_head_gla, w_out_gla, w_ffn_gate, w_ffn_up, w_ffn_down):
    bp, sp, d = x_prompt.shape
    bs, ss, _ = x_sample.shape
    depth = g_pre_mix.shape[0]
    assert d == D_MODEL and sp % PROMPT_ROWS == 0 and SAMPLE_ROWS % ss == 0 and (bs * ss) % SAMPLE_ROWS == 0
    assert (bp * sp) % TM == 0 and (bs * ss) % TM == 0

    xp = x_prompt.reshape(bp * sp, d)
    xs = x_sample.reshape(bs * ss, d)
    vec = lambda g: g.reshape(1, D_MODEL).astype(F32)

    prompt_states = {"C": [], "n": [], "m": [], "S": []}
    sample_states = {"C": [], "n": [], "m": [], "S": []}
    ffn_gate, ffn_up, ffn_down = (w.astype(BF16) for w in (w_ffn_gate, w_ffn_up, w_ffn_down))
    in_mlstm, in_gla = w_in_mlstm.astype(BF16), w_in_gla.astype(BF16)
    ones = jnp.ones((QK_WIDTH,), F32)
    head_scale = jnp.full((QK_WIDTH,), DQK ** -0.5, F32)
    for layer in range(depth):
        j = layer // 2
        if layer % 2 == 0:
            project = lambda x: _inproj(x, vec(g_pre_mix[layer]), in_mlstm, j,
                                        _pad_gate_columns(w_in_mlstm[j][:, MAIN_WIDTH:]),
                                        jnp.concatenate([ones, head_scale]).reshape(1, -1),
                                        silu_gate=False, gate_rows=True)
            (qk_p, v_p, gate_p, gcol_p, grow_p), (qk_s, v_s, gate_s, gcol_s, grow_s) = project(xp), project(xs)
            h, st_p, st_s = _mlstm_mixer((qk_p, v_p, gcol_p, grow_p), (qk_s, v_s, gcol_s, grow_s),
                                         b_i_mlstm[j], b_f_mlstm[j],
                                         state_mlstm_C[j], state_mlstm_n[j], state_mlstm_m[j], bp, sp)
            for dst, st in ((prompt_states, st_p), (sample_states, st_s)):
                dst["C"].append(st[0]); dst["n"].append(st[1]); dst["m"].append(st[2])
            w_out, g_head = w_out_mlstm[j], g_head_mlstm[j]
        else:
            project = lambda x: _inproj(x, vec(g_pre_mix[layer]), in_gla, j,
                                        _pad_gate_columns(w_in_gla[j][:, MAIN_WIDTH:]),
                                        jnp.concatenate([head_scale, ones]).reshape(1, -1),
                                        silu_gate=True, gate_rows=False)
            (qk_p, v_p, gate_p, glr_p), (qk_s, v_s, gate_s, glr_s) = project(xp), project(xs)
            h, s_p, s_s = _gla_mixer((qk_p, v_p, glr_p), (qk_s, v_s, glr_s), w_gate2_gla[j], b_gate_gla[j],
                                     state_gla_S[j], bp, sp)
            prompt_states["S"].append(s_p)
            sample_states["S"].append(s_s)
            w_out, g_head = w_out_gla[j], g_head_gla[j]
        xp, xs = _block_tail(h[0], h[1], gate_p, gate_s, g_head.reshape(1, V_WIDTH).astype(F32), w_out.astype(BF16),
                             xp, xs,
                             vec(g_post_mix[layer]), vec(g_pre_ffn[layer]), vec(g_post_ffn[layer]),
                             ffn_gate, ffn_up, ffn_down, layer)

    stack = lambda xs_: jnp.stack(xs_)
    return (xp.reshape(bp, sp, d), xs.reshape(bs, ss, d),
            stack(prompt_states["C"]), stack(prompt_states["n"]), stack(prompt_states["m"]), stack(prompt_states["S"]),
            stack(sample_states["C"]), stack(sample_states["n"]), stack(sample_states["m"]), stack(sample_states["S"]))
```

```python
import functools

import numpy as np
import jax
import jax.numpy as jnp
from jax import lax
from jax.experimental import pallas as pl
from jax.experimental.pallas import tpu as pltpu

F32 = jnp.float32
BF16 = jnp.bfloat16

D_MODEL = 1024
N_HEADS = 4
DQK = 128
DV = 256
QK_WIDTH = N_HEADS * DQK
V_WIDTH = N_HEADS * DV
MAIN_WIDTH = 2 * QK_WIDTH + 2 * V_WIDTH
GATE_RANK = 16
GATE_TAU = 16.0
EPS = 1e-6
LANES = 128
SUBLANES = 8
MXU_DIM = 256
VMEM_LIMIT = 56 * 1024 * 1024
MIXER_VMEM_LIMIT = 62 * 1024 * 1024

PROMPT_ROWS = 128
SAMPLE_ROWS = 64
LEVELS = (64, 32, 16, 8, 4, 2, 1)
TM = 512
TM_IN = 1024
GROUP_ROWS = 256
FF_SPLITS = 2


def _dot(a, b):
    return jnp.dot(a.astype(BF16), b.astype(BF16), preferred_element_type=F32)


def _dot_nt(a, b):
    return lax.dot_general(a.astype(BF16), b.astype(BF16), (((1,), (1,)), ((), ())),
                           preferred_element_type=F32)


def _dot_tn(a, b):
    return lax.dot_general(a.astype(BF16), b.astype(BF16), (((0,), (0,)), ((), ())),
                           preferred_element_type=F32)


def _split_hi_lo(x):
    hi = x.astype(BF16)
    lo = (x - hi.astype(F32)).astype(BF16)
    return hi, lo


def _log_sigmoid(x):
    return jnp.minimum(x, 0.0) - jnp.log(1.0 + jnp.exp(-jnp.abs(x)))


def _sigmoid(x):
    return 1.0 / (1.0 + jnp.exp(-x))


def _rms(x, gain):
    return x * lax.rsqrt(jnp.mean(x * x, axis=-1, keepdims=True) + EPS) * gain


def _rows_per_segment(values, seg_len):
    parts = [jnp.broadcast_to(v, (seg_len, v.shape[1])) for v in values]
    return parts[0] if len(parts) == 1 else jnp.concatenate(parts, axis=0)


def _emit_staggered(chains, lag=1):
    pending, live, step = list(chains), [], 0
    while pending or live:
        while pending and (lag == 0 or step % lag == 0):
            live.append(pending.pop(0))
            if lag:
                break
        step += 1
        for chain in list(live):
            if next(chain, StopIteration) is StopIteration:
                live.remove(chain)


def _const_spec(shape):
    nd = len(shape)
    return pl.BlockSpec(shape, lambda *_: (0,) * nd)


def _two_group_specs(n_prompt_blocks, width, single_buffer_sample=False):
    prompt = pl.BlockSpec((TM, width), lambda i: (jnp.minimum(i, n_prompt_blocks - 1), 0))
    mode = dict(pipeline_mode=pl.Buffered(1)) if single_buffer_sample else {}
    sample = pl.BlockSpec((TM, width), lambda i: (jnp.maximum(i - n_prompt_blocks, 0), 0), **mode)
    return prompt, sample


def _resident_spec(shape, layer=None):
    nd = len(shape)
    if layer is None:
        return pl.BlockSpec(shape, lambda i: (0,) * nd, pipeline_mode=pl.Buffered(1))
    return pl.BlockSpec((None,) + tuple(shape), lambda i: (layer,) + (0,) * nd, pipeline_mode=pl.Buffered(1))


def _inproj_body(x_ref, gain_ref, w_ref, wg_ref, scale_ref, oqk_ref, ov_ref, ogate_ref, og_ref,
                 *maybe_grow_ref, silu_gate):
    n_groups = x_ref.shape[0] // GROUP_ROWS
    groups = [slice(a * GROUP_ROWS, (a + 1) * GROUP_ROWS) for a in range(n_groups)]
    h = [None] * n_groups
    for a in range(n_groups + 1):
        if a < n_groups:
            h[a] = _rms(x_ref[groups[a], :], gain_ref[...]).astype(BF16)
        if a > 0:
            rows, hb = groups[a - 1], h[a - 1]
            narrow = jnp.dot(hb, wg_ref[...], preferred_element_type=F32)
            og_ref[rows, :] = narrow
            for grow_ref in maybe_grow_ref:
                grow_ref[:, rows] = narrow.T[0:SUBLANES, :]
            qk = jnp.dot(hb, w_ref[:, 0:2 * QK_WIDTH], preferred_element_type=F32)
            oqk_ref[rows, :] = qk * scale_ref[...]
            v = jnp.dot(hb, w_ref[:, 2 * QK_WIDTH:2 * QK_WIDTH + V_WIDTH], preferred_element_type=F32)
            ov_ref[rows, :] = v.astype(BF16)
            gate = jnp.dot(hb, w_ref[:, 2 * QK_WIDTH + V_WIDTH:MAIN_WIDTH], preferred_element_type=F32)
            ogate_ref[rows, :] = gate * _sigmoid(gate) if silu_gate else _sigmoid(gate)


def _inproj(x, gain, w_in, layer, w_gate, qk_scale, silu_gate, gate_rows):
    m = x.shape[0]
    assert m % TM_IN == 0
    rows_spec = lambda width: pl.BlockSpec((TM_IN, width), lambda i: (i, 0))
    out_specs = [rows_spec(2 * QK_WIDTH), rows_spec(V_WIDTH), rows_spec(V_WIDTH), rows_spec(LANES)]
    out_shape = [
        jax.ShapeDtypeStruct((m, 2 * QK_WIDTH), F32),
        jax.ShapeDtypeStruct((m, V_WIDTH), BF16),
        jax.ShapeDtypeStruct((m, V_WIDTH), F32),
        jax.ShapeDtypeStruct((m, LANES), F32),
    ]
    if gate_rows:
        out_specs.append(pl.BlockSpec((SUBLANES, TM_IN), lambda i: (0, i)))
        out_shape.append(jax.ShapeDtypeStruct((SUBLANES, m), F32))
    return pl.pallas_call(
        functools.partial(_inproj_body, silu_gate=silu_gate),
        grid=(m // TM_IN,),
        in_specs=[rows_spec(D_MODEL), _resident_spec((1, D_MODEL)), _resident_spec(w_in.shape[1:], layer),
                  _resident_spec((D_MODEL, LANES)), _resident_spec((1, 2 * QK_WIDTH))],
        out_specs=out_specs,
        out_shape=out_shape,
        compiler_params=pltpu.CompilerParams(dimension_semantics=("arbitrary",), vmem_limit_bytes=VMEM_LIMIT),
        name="inproj",
    )(x, gain, w_in, w_gate, qk_scale)


def _tail_body(hp_ref, hs_ref, gp_ref, gs_ref, ghead_ref, wo_ref, xp_ref, xs_ref, gpm_ref, gpf_ref, gqf_ref,
               wg_ref, wu_ref, wd_ref, op_ref, os_ref, out_scr, *, n_prompt_blocks):
    i = pl.program_id(0)
    is_prompt = i < n_prompt_blocks
    d_ff = wg_ref.shape[1]
    n_tiles = d_ff // MXU_DIM
    bounds = [MXU_DIM * ((n_tiles * t + FF_SPLITS - 1) // FF_SPLITS) for t in range(FF_SPLITS + 1)]

    def stages(rows):
        raw = jnp.where(is_prompt, hp_ref[rows, :], hs_ref[rows, :])
        gate = jnp.where(is_prompt, gp_ref[rows, :], gs_ref[rows, :])
        gain = ghead_ref[...]
        y = jnp.concatenate(
            [gate[:, h * DV:(h + 1) * DV] * _rms(raw[:, h * DV:(h + 1) * DV], gain[:, h * DV:(h + 1) * DV])
             for h in range(N_HEADS)], axis=1).astype(BF16)
        yield
        mix = jnp.dot(y, wo_ref[...], preferred_element_type=F32)
        yield
        x1 = jnp.where(is_prompt, xp_ref[rows, :], xs_ref[rows, :]) + _rms(mix, gpm_ref[...])
        h = _rms(x1, gpf_ref[...]).astype(BF16)
        yield
        ffn = None
        for t in range(FF_SPLITS):
            cols = slice(bounds[t], bounds[t + 1])
            gate = jnp.dot(h, wg_ref[:, cols], preferred_element_type=F32)
            up = jnp.dot(h, wu_ref[:, cols], preferred_element_type=F32)
            yield
            act = (gate * _sigmoid(gate) * up).astype(BF16)
            yield
            part = jnp.dot(act, wd_ref[cols, :], preferred_element_type=F32)
            ffn = part if ffn is None else ffn + part
            yield
        out_scr[rows, :] = x1 + _rms(ffn, gqf_ref[...])
        yield

    _emit_staggered([stages(slice(a * GROUP_ROWS, (a + 1) * GROUP_ROWS)) for a in range(TM // GROUP_ROWS)])

    @pl.when(is_prompt)
    def _():
        op_ref[...] = out_scr[...]

    @pl.when(jnp.logical_not(is_prompt))
    def _():
        os_ref[...] = out_scr[...]


def _block_tail(hp, hs, gate_p, gate_s, ghead, w_out, xp, xs, g_post_mix, g_pre_ffn, g_post_ffn,
                w_gate, w_up, w_down, layer):
    npb = xp.shape[0] // TM
    m = xp.shape[0] + xs.shape[0]
    d_ff = w_gate.shape[2]
    assert d_ff % MXU_DIM == 0
    in_p, in_s = _two_group_specs(npb, D_MODEL, single_buffer_sample=True)
    vec = _resident_spec((1, D_MODEL))
    return pl.pallas_call(
        functools.partial(_tail_body, n_prompt_blocks=npb),
        grid=(m // TM,),
        in_specs=[
            in_p, in_s,
            in_p, in_s,
            _resident_spec((1, V_WIDTH)),
            _resident_spec((V_WIDTH, D_MODEL)),
            in_p, in_s,
            vec, vec, vec,
            _resident_spec((D_MODEL, d_ff), layer), _resident_spec((D_MODEL, d_ff), layer),
            _resident_spec((d_ff, D_MODEL), layer),
        ],
        out_specs=list(_two_group_specs(npb, D_MODEL)),
        out_shape=[jax.ShapeDtypeStruct(xp.shape, F32), jax.ShapeDtypeStruct(xs.shape, F32)],
        scratch_shapes=[pltpu.VMEM((TM, D_MODEL), F32)],
        compiler_params=pltpu.CompilerParams(dimension_semantics=("arbitrary",), vmem_limit_bytes=VMEM_LIMIT),
        name="block_tail",
    )(hp, hs, gate_p, gate_s, ghead, w_out, xp, xs, g_post_mix, g_pre_ffn, g_post_ffn, w_gate, w_up, w_down)


def _segment_structure(rows, seg_len):
    t = np.arange(rows)
    seg = t // seg_len
    same = seg[:, None] == seg[None, :]
    cum = same & (t[None, :] <= t[:, None])
    return t, seg, same, cum


def _mlstm_constants(rows, seg_len):
    _, _, same, cum = _segment_structure(rows, seg_len)
    col = np.concatenate([cum, same], axis=0).astype(np.float32)
    row = cum.T.astype(np.float32)
    return jnp.asarray(col, BF16), jnp.asarray(row, BF16)


def _mlstm_rows(qk_ref, v_ref, gcol_ref, grow_ref, bias_row_ref, bias_col_ref, ccol_ref, crow_ref,
                c_in, n_in, m_in, c_out, n_out, m_out, h_ref, rows, seg_len):
    nseg = rows // seg_len
    r = lax.broadcasted_iota(jnp.int32, (rows, rows), 0)
    c = lax.broadcasted_iota(jnp.int32, (rows, rows), 1)
    same = None if nseg == 1 else (r // seg_len) == (c // seg_len)
    lower = (c <= r) if nseg == 1 else same & (c <= r)
    row_seg = lax.broadcasted_iota(jnp.int32, (rows, DQK), 0) // seg_len

    pre_col = gcol_ref[...] + bias_row_ref[...]
    grow = grow_ref[0] if len(grow_ref.shape) == 3 else grow_ref[...]
    pre_row = grow + bias_col_ref[...]
    ccol = ccol_ref[...]
    crow = crow_ref[...]
    hi, lo = _split_hi_lo(_log_sigmoid(pre_col))
    sums_col = jnp.dot(ccol, hi, preferred_element_type=F32) + jnp.dot(ccol, lo, preferred_element_type=F32)
    hi, lo = _split_hi_lo(_log_sigmoid(pre_row))
    sums_row = jnp.dot(hi, crow, preferred_element_type=F32) + jnp.dot(lo, crow, preferred_element_type=F32)
    yield

    qk = qk_ref[...]
    c_prev = [[c_in[s, h] for h in range(N_HEADS)] for s in range(nseg)]
    n_state = [[n_in[s, h:h + 1, :] for h in range(N_HEADS)] for s in range(nseg)]
    m_state = [[m_in[s, h:h + 1, 0:1] for h in range(N_HEADS)] for s in range(nseg)]
    writes = []

    heads = range(N_HEADS)
    q = [qk[:, h * DQK:(h + 1) * DQK] for h in heads]
    k = [qk[:, QK_WIDTH + h * DQK:QK_WIDTH + (h + 1) * DQK] for h in heads]
    v_bf = [v_ref[:, h * DV:(h + 1) * DV] for h in heads]
    i_col = [pre_col[:, h:h + 1] for h in heads]
    b_col = [sums_col[0:rows, N_HEADS + h:N_HEADS + h + 1] for h in heads]
    b_last = [sums_col[rows:2 * rows, N_HEADS + h:N_HEADS + h + 1] for h in heads]
    src = [pre_row[h:h + 1, :] - sums_row[N_HEADS + h:N_HEADS + h + 1, :] for h in heads]
    m_prev = [_rows_per_segment([m_state[s][h] for s in range(nseg)], seg_len) for h in heads]
    n_prev = [_rows_per_segment([n_state[s][h] for s in range(nseg)], seg_len) for h in heads]

    dmat = [jnp.where(lower, b_col[h] + src[h], -jnp.inf) for h in heads]
    inter = [b_col[h] + m_prev[h] for h in heads]
    m_t = [jnp.maximum(inter[h], jnp.max(dmat[h], axis=1, keepdims=True)) for h in heads]
    if nseg == 1:
        m_new = [jnp.broadcast_to(m_t[h][rows - 1:rows, :], (rows, 1)) for h in heads]
    else:
        dlast = [jnp.where(same, b_last[h] + src[h], -jnp.inf) for h in heads]
        m_new = [jnp.maximum(b_last[h] + m_prev[h], jnp.max(dlast[h], axis=1, keepdims=True)) for h in heads]
    yield
    scores = [_dot_nt(q[h], k[h]) for h in heads]
    q_bf = [q[h].astype(BF16) for h in heads]
    qc = []
    for h in heads:
        parts = []
        for s in range(nseg):
            full = jnp.dot(q_bf[h], c_prev[s][h].astype(BF16), preferred_element_type=F32)
            parts.append(full[s * seg_len:(s + 1) * seg_len] if nseg > 1 else full)
        qc.append(parts[0] if nseg == 1 else jnp.concatenate(parts, axis=0))
    qn = [jnp.sum(q[h] * n_prev[h], axis=1, keepdims=True) for h in heads]
    yield

    w_inter = [jnp.exp(inter[h] - m_t[h]) for h in heads]
    a = [scores[h] * jnp.exp(dmat[h] - m_t[h]) for h in heads]
    w_s = [jnp.exp(b_last[h] - b_col[h] + i_col[h] - m_new[h]) for h in heads]
    decay = [jnp.exp(b_last[h] + m_prev[h] - m_new[h]) for h in heads]
    kw = [k[h] * w_s[h] for h in heads]
    yield

    num = [jnp.dot(a[h].astype(BF16), v_bf[h], preferred_element_type=F32) + w_inter[h] * qc[h] for h in heads]
    den = [jnp.sum(a[h], axis=1, keepdims=True) + w_inter[h] * qn[h] for h in heads]
    for h in heads:
        for s in range(nseg):
            kw_s = kw[h] if nseg == 1 else jnp.where(row_seg == s, kw[h], 0.0)
            dec = decay[h][s * seg_len:s * seg_len + 1, :]
            writes.append((s, h, dec * c_prev[s][h] + _dot_tn(kw_s, v_bf[h]),
                           dec * n_state[s][h] + jnp.sum(kw_s, axis=0, keepdims=True),
                           jnp.broadcast_to(m_new[h][s * seg_len:s * seg_len + 1, :], (1, LANES))))

    yield
    for h in heads:
        h_ref[:, h * DV:(h + 1) * DV] = num[h] * (1.0 / jnp.maximum(jnp.abs(den[h]), jnp.exp(-m_t[h])))

    for s, h, c_new, n_new, m_new in writes:
        c_out[s, h] = c_new
        n_out[s, h:h + 1, :] = n_new
        m_out[s, h:h + 1, :] = m_new
    yield


def _mlstm_body(qk_p, v_p, gcol_p, grow_p, qk_s, v_s, gcol_s, grow_s, brow_ref, bcol_ref,
                ccol_p, crow_p, ccol_s, crow_s, c_in, n_in, m_in,
                h_p, c_p, n_p, m_p, h_s, c_out, n_out, m_out, *, n_prompt_seq, seg_len):
    @pl.when(pl.program_id(0) == 0)
    def _():
        c_p[...] = jnp.zeros_like(c_p)
        n_p[...] = jnp.zeros_like(n_p)
        m_p[...] = jnp.zeros_like(m_p)

    chains = []
    for a in range(n_prompt_seq):
        one = pl.ds(a, 1)
        c, n, m = c_p.at[one], n_p.at[one], m_p.at[one]
        chains.append(_mlstm_rows(qk_p.at[a], v_p.at[a], gcol_p.at[a], grow_p.at[a], brow_ref, bcol_ref, ccol_p, crow_p,
                                  c, n, m, c, n, m, h_p.at[a], PROMPT_ROWS, PROMPT_ROWS))
    chains.append(_mlstm_rows(qk_s, v_s, gcol_s, grow_s, brow_ref, bcol_ref, ccol_s, crow_s,
                              c_in, n_in, m_in, c_out, n_out, m_out, h_s, SAMPLE_ROWS, seg_len))
    _emit_staggered(chains)


def _gla_constants(rows, seg_len):
    t, seg, _, cum = _segment_structure(rows, seg_len)
    level_id = np.full((rows, rows), -1, np.int32)
    level_id[t, t] = 0
    for idx, hb in enumerate(LEVELS):
        if 2 * hb > seg_len:
            continue
        block = t // (2 * hb)
        upper = (t % (2 * hb)) >= hb
        level_id[(block[:, None] == block[None, :]) & upper[:, None] & ~upper[None, :]] = idx + 1
    segind = (seg[:, None] == np.arange(LANES)[None, :]).astype(np.float32)
    return jnp.asarray(cum.astype(np.float32), BF16), jnp.asarray(level_id), jnp.asarray(segind, BF16)


def _row_bcast(x, row, n):
    return jnp.broadcast_to(x[row:row + 1, :], (n, x.shape[1]))


def _gla_rows(qk_ref, v_ref, glr_ref, w2_ref, bg_ref, cum_ref, lvl_ref, segind_ref,
              s_in, s_out, o_ref, rows, seg_len):
    nseg = rows // seg_len
    ntile = rows // SUBLANES
    row_id = lax.broadcasted_iota(jnp.int32, (rows, DQK), 0)
    row_seg = row_id // seg_len
    row_in_tile = row_id % SUBLANES

    pre = _dot(glr_ref[...], w2_ref[...]) + bg_ref[...]
    g_all = _log_sigmoid(pre) * (1.0 / GATE_TAU)
    g_hi, g_lo = _split_hi_lo(g_all)
    cum = cum_ref[...]
    bc_all = jnp.dot(cum, g_hi, preferred_element_type=F32) + jnp.dot(cum, g_lo, preferred_element_type=F32)
    level_tiles = [lvl_ref[t * SUBLANES:(t + 1) * SUBLANES, :] for t in range(ntile)]
    segind = segind_ref[...]
    total_col = (lax.dot_general(g_hi, segind, (((0,), (0,)), ((), ())), preferred_element_type=F32)
                 + lax.dot_general(g_lo, segind, (((0,), (0,)), ((), ())), preferred_element_type=F32))
    yield

    qk = qk_ref[...]
    s_prev = [[s_in[s, h] for h in range(N_HEADS)] for s in range(nseg)]
    writes = []

    heads = range(N_HEADS)
    sl = [slice(h * DQK, (h + 1) * DQK) for h in heads]
    q = [qk[:, sl[h]] for h in heads]
    k = [qk[:, QK_WIDTH + h * DQK:QK_WIDTH + (h + 1) * DQK] for h in heads]
    v_bf = [v_ref[:, h * DV:(h + 1) * DV] for h in heads]
    g = [g_all[:, sl[h]] for h in heads]
    bc = [bc_all[:, sl[h]] for h in heads]
    seg_total = [jnp.concatenate([_row_bcast(bc[h], (s + 1) * seg_len - 1, seg_len) for s in range(nseg)], axis=0)
                 if nseg > 1 else _row_bcast(bc[h], rows - 1, rows) for h in heads]

    qq = [(q[h] * jnp.exp(bc[h])).astype(BF16) for h in heads]
    kk = [k[h] * jnp.exp(seg_total[h] - bc[h]) for h in heads]
    o_inter = []
    for h in heads:
        parts = []
        for s in range(nseg):
            st = s_prev[s][h]
            oi = jnp.dot(qq[h], st.astype(BF16), preferred_element_type=F32)
            parts.append(oi[s * seg_len:(s + 1) * seg_len] if nseg > 1 else oi)
            kk_s = kk[h] if nseg == 1 else jnp.where(row_seg == s, kk[h], 0.0)
            writes.append((s, h, st * jnp.exp(total_col[sl[h], s:s + 1]) + _dot_tn(kk_s, v_bf[h])))
        o_inter.append(parts[0] if nseg == 1 else jnp.concatenate(parts, axis=0))
    yield

    a = [[None] * ntile for h in heads]

    def put(h, t, level, z_rows):
        prev = 0.0 if a[h][t] is None else a[h][t]
        a[h][t] = jnp.where(level_tiles[t] == level, z_rows, prev)

    for h in heads:
        z = _dot_nt(q[h], k[h])
        for t in range(ntile):
            put(h, t, 0, z[t * SUBLANES:(t + 1) * SUBLANES])

    for idx, hb in enumerate(LEVELS):
        if 2 * hb > seg_len:
            continue
        level = idx + 1
        if hb >= SUBLANES:
            starts = range(0, rows, 2 * hb)
            for h in heads:
                e_parts, x_parts = [], []
                for r0 in starts:
                    ref = _row_bcast(bc[h], r0 + hb - 1, hb)
                    e_parts += [ref - bc[h][r0:r0 + hb], bc[h][r0 + hb:r0 + 2 * hb] - ref]
                    x_parts += [k[h][r0:r0 + hb], q[h][r0 + hb:r0 + 2 * hb]]
                u = jnp.concatenate(x_parts, axis=0) * jnp.exp(jnp.concatenate(e_parts, axis=0))
                u_upper = jnp.concatenate([u[r0 + hb:r0 + 2 * hb] for r0 in starts], axis=0)
                z = _dot_nt(u_upper, u)
                upper_tiles = [t for r0 in starts for t in range((r0 + hb) // SUBLANES, (r0 + 2 * hb) // SUBLANES)]
                for i, t in enumerate(upper_tiles):
                    put(h, t, level, z[i * SUBLANES:(i + 1) * SUBLANES])
        else:
            upper = (row_id % (2 * hb)) >= hb
            for h in heads:
                if hb == 1:
                    e_l = jnp.where(upper, g[h], 0.0)
                else:
                    tiles = range(ntile)
                    if hb == 4:
                        ref = jnp.concatenate([_row_bcast(bc[h], t * SUBLANES + 3, SUBLANES) for t in tiles], axis=0)
                    else:
                        lo = jnp.concatenate([_row_bcast(bc[h], t * SUBLANES + 1, SUBLANES) for t in tiles], axis=0)
                        hi = jnp.concatenate([_row_bcast(bc[h], t * SUBLANES + 5, SUBLANES) for t in tiles], axis=0)
                        ref = jnp.where(row_in_tile < 4, lo, hi)
                    d = bc[h] - ref
                    e_l = jnp.where(upper, d, -d)
                u = jnp.where(upper, q[h], k[h]) * jnp.exp(e_l)
                z = _dot_nt(u, u)
                for t in range(ntile):
                    put(h, t, level, z[t * SUBLANES:(t + 1) * SUBLANES])
        yield

    for h in heads:
        o_ref[:, h * DV:(h + 1) * DV] = jnp.dot(jnp.concatenate(a[h], axis=0).astype(BF16), v_bf[h],
                                                preferred_element_type=F32) + o_inter[h]

    for s, h, s_new in writes:
        s_out[s, h] = s_new
    yield


def _gla_body(qk_p, v_p, glr_p, qk_s, v_s, glr_s, w2_ref, bg_ref, cum_p, lvl_p, segind_p, cum_s, lvl_s, segind_s,
              s_in, o_p, s_p, o_s, s_out, *, n_prompt_seq, seg_len):
    @pl.when(pl.program_id(0) == 0)
    def _():
        s_p[...] = jnp.zeros_like(s_p)

    chains = []
    for a in range(n_prompt_seq):
        state = s_p.at[pl.ds(a, 1)]
        chains.append(_gla_rows(qk_p.at[a], v_p.at[a], glr_p.at[a], w2_ref, bg_ref, cum_p, lvl_p, segind_p,
                                state, state, o_p.at[a], PROMPT_ROWS, PROMPT_ROWS))
    chains.append(_gla_rows(qk_s, v_s, glr_s, w2_ref, bg_ref, cum_s, lvl_s, segind_s,
                            s_in, s_out, o_s, SAMPLE_ROWS, seg_len))
    _emit_staggered(chains)


def _mixer_params(n_axes):
    return pltpu.CompilerParams(dimension_semantics=("arbitrary",) * n_axes, vmem_limit_bytes=MIXER_VMEM_LIMIT)


def _state_spec(n, trailing):
    nd = 1 + len(trailing)
    return pl.BlockSpec((n,) + trailing, lambda *ids: (ids[0],) + (0,) * (nd - 1))


def _mlstm_mixer(prompt, sample, bias_i, bias_f, c0, n0, m0, n_prompt_seq, prompt_len):
    qk_p, v_p, gcol_p, grow_p = prompt
    qk_s, v_s, gcol_s, grow_s = sample
    n_sample_seq = c0.shape[0]
    seg_len = qk_s.shape[0] // n_sample_seq
    bias = jnp.concatenate([bias_i, bias_f]).astype(F32)
    bias_row = jnp.zeros((1, LANES), F32).at[0, :2 * N_HEADS].set(bias)
    bias_col = bias.reshape(2 * N_HEADS, 1)

    nb, plen = n_prompt_seq, prompt_len
    steps = plen // PROMPT_ROWS
    spb = SAMPLE_ROWS // seg_len
    assert n_sample_seq == steps * spb, "one block of sample sequences per prompt chunk step"
    ccol_p, crow_p = _mlstm_constants(PROMPT_ROWS, PROMPT_ROWS)
    ccol_s, crow_s = _mlstm_constants(SAMPLE_ROWS, seg_len)
    consts = [bias_row, bias_col, ccol_p, crow_p, ccol_s, crow_s]
    m0b = jnp.broadcast_to(m0[:, :, None], m0.shape + (LANES,))
    grow_p = jnp.transpose(grow_p.reshape(2 * N_HEADS, nb, plen), (1, 0, 2))
    grow_s = jnp.transpose(grow_s.reshape(2 * N_HEADS, steps, SAMPLE_ROWS), (1, 0, 2))
    seq_rows = lambda width: pl.BlockSpec((nb, PROMPT_ROWS, width), lambda i: (0, i, 0))
    blk_rows = lambda width: pl.BlockSpec((SAMPLE_ROWS, width), lambda i: (i, 0))
    whole = lambda shape: pl.BlockSpec(shape, lambda i: (0,) * len(shape))
    state_specs = [_state_spec(spb, (N_HEADS, DQK, DV)), _state_spec(spb, (N_HEADS, DQK)),
                   _state_spec(spb, (N_HEADS, LANES))]
    h_p, c_p, n_p, m_p, h_s, c_s, n_s, m_s = pl.pallas_call(
        functools.partial(_mlstm_body, n_prompt_seq=nb, seg_len=seg_len),
        grid=(steps,),
        in_specs=[seq_rows(2 * QK_WIDTH), seq_rows(V_WIDTH), seq_rows(LANES),
                  pl.BlockSpec((nb, 2 * N_HEADS, PROMPT_ROWS), lambda i: (0, 0, i)),
                  blk_rows(2 * QK_WIDTH), blk_rows(V_WIDTH), blk_rows(LANES),
                  pl.BlockSpec((1, 2 * N_HEADS, SAMPLE_ROWS), lambda i: (i, 0, 0))]
        + [_const_spec(a.shape) for a in consts] + state_specs,
        out_specs=[seq_rows(V_WIDTH), whole((nb, N_HEADS, DQK, DV)), whole((nb, N_HEADS, DQK)),
                   whole((nb, N_HEADS, LANES)), blk_rows(V_WIDTH)] + state_specs,
        out_shape=[
            jax.ShapeDtypeStruct((nb, plen, V_WIDTH), F32),
            jax.ShapeDtypeStruct((nb, N_HEADS, DQK, DV), F32),
            jax.ShapeDtypeStruct((nb, N_HEADS, DQK), F32),
            jax.ShapeDtypeStruct((nb, N_HEADS, LANES), F32),
            jax.ShapeDtypeStruct((n_sample_seq * seg_len, V_WIDTH), F32),
            jax.ShapeDtypeStruct(c0.shape, F32),
            jax.ShapeDtypeStruct(n0.shape, F32),
            jax.ShapeDtypeStruct(m0b.shape, F32),
        ],
        compiler_params=_mixer_params(1),
        name="mlstm_mixer",
    )(qk_p.reshape(nb, plen, -1), v_p.reshape(nb, plen, -1), gcol_p.reshape(nb, plen, -1), grow_p,
      qk_s, v_s, gcol_s, grow_s, *consts, c0, n0, m0b)

    return (h_p.reshape(nb * plen, V_WIDTH), h_s), (c_p, n_p, m_p[:, :, 0]), (c_s, n_s, m_s[:, :, 0])


def _gla_mixer(prompt, sample, w_gate2, b_gate, s0, n_prompt_seq, prompt_len):
    qk_p, v_p, glr_p = prompt
    qk_s, v_s, glr_s = sample
    n_sample_seq = s0.shape[0]
    seg_len = qk_s.shape[0] // n_sample_seq
    nb, plen = n_prompt_seq, prompt_len
    steps = plen // PROMPT_ROWS
    spb = SAMPLE_ROWS // seg_len
    assert n_sample_seq == steps * spb, "one block of sample sequences per prompt chunk step"
    w2 = jnp.zeros((LANES, QK_WIDTH), BF16).at[:GATE_RANK].set(w_gate2.astype(BF16))
    bg = b_gate.reshape(1, QK_WIDTH).astype(F32)
    consts = [w2, bg, *_gla_constants(PROMPT_ROWS, PROMPT_ROWS), *_gla_constants(SAMPLE_ROWS, seg_len)]
    seq_rows = lambda width: pl.BlockSpec((nb, PROMPT_ROWS, width), lambda i: (0, i, 0))
    blk_rows = lambda width: pl.BlockSpec((SAMPLE_ROWS, width), lambda i: (i, 0))
    state_spec = _state_spec(spb, (N_HEADS, DQK, DV))
    o_p, s_p, o_s, s_s = pl.pallas_call(
        functools.partial(_gla_body, n_prompt_seq=nb, seg_len=seg_len),
        grid=(steps,),
        in_specs=[seq_rows(2 * QK_WIDTH), seq_rows(V_WIDTH), seq_rows(LANES),
                  blk_rows(2 * QK_WIDTH), blk_rows(V_WIDTH), blk_rows(LANES)]
        + [_const_spec(a.shape) for a in consts] + [state_spec],
        out_specs=[seq_rows(V_WIDTH), pl.BlockSpec((nb, N_HEADS, DQK, DV), lambda i: (0, 0, 0, 0)),
                   blk_rows(V_WIDTH), state_spec],
        out_shape=[
            jax.ShapeDtypeStruct((nb, plen, V_WIDTH), F32),
            jax.ShapeDtypeStruct((nb, N_HEADS, DQK, DV), F32),
            jax.ShapeDtypeStruct((n_sample_seq * seg_len, V_WIDTH), F32),
            jax.ShapeDtypeStruct(s0.shape, F32),
        ],
        compiler_params=_mixer_params(1),
        name="gla_mixer",
    )(qk_p.reshape(nb, plen, -1), v_p.reshape(nb, plen, -1), glr_p.reshape(nb, plen, -1),
      qk_s, v_s, glr_s, *consts, s0)

    return (o_p.reshape(nb * plen, V_WIDTH), o_s), s_p, s_s


def _pad_gate_columns(w):
    return jnp.zeros((D_MODEL, LANES), BF16).at[:, :w.shape[1]].set(w.astype(BF16))


def kernel(x_prompt, x_sample, state_mlstm_C, state_mlstm_n, state_mlstm_m, state_gla_S, g_pre_mix, g_post_mix, g_pre_ffn, g_post_ffn, w_in_mlstm, b_i_mlstm, b_f_mlstm, g_head_mlstm, w_out_mlstm, w_in_gla, w_gate2_gla, b_gate_gla, g_head_gla, w_out_gla, w_ffn_gate, w_ffn_up, w_ffn_down):
    bp, sp, d = x_prompt.shape
    bs, ss, _ = x_sample.shape
    depth = g_pre_mix.shape[0]
    assert d == D_MODEL and sp % PROMPT_ROWS == 0 and SAMPLE_ROWS % ss == 0 and (bs * ss) % SAMPLE_ROWS == 0
    assert (bp * sp) % TM == 0 and (bs * ss) % TM == 0

    xp = x_prompt.reshape(bp * sp, d)
    xs = x_sample.reshape(bs * ss, d)
    vec = lambda g: g.reshape(1, D_MODEL).astype(F32)

    prompt_states = {"C": [], "n": [], "m": [], "S": []}
    sample_states = {"C": [], "n": [], "m": [], "S": []}
    ffn_gate, ffn_up, ffn_down = (w.astype(BF16) for w in (w_ffn_gate, w_ffn_up, w_ffn_down))
    in_mlstm, in_gla = w_in_mlstm.astype(BF16), w_in_gla.astype(BF16)
    ones = jnp.ones((QK_WIDTH,), F32)
    head_scale = jnp.full((QK_WIDTH,), DQK ** -0.5, F32)
    for layer in range(depth):
        j = layer // 2
        if layer % 2 == 0:
            project = lambda x: _inproj(x, vec(g_pre_mix[layer]), in_mlstm, j,
                                        _pad_gate_columns(w_in_mlstm[j][:, MAIN_WIDTH:]),
                                        jnp.concatenate([ones, head_scale]).reshape(1, -1),
                                        silu_gate=False, gate_rows=True)
            (qk_p, v_p, gate_p, gcol_p, grow_p), (qk_s, v_s, gate_s, gcol_s, grow_s) = project(xp), project(xs)
            h, st_p, st_s = _mlstm_mixer((qk_p, v_p, gcol_p, grow_p), (qk_s, v_s, gcol_s, grow_s),
                                         b_i_mlstm[j], b_f_mlstm[j],
                                         state_mlstm_C[j], state_mlstm_n[j], state_mlstm_m[j], bp, sp)
            for dst, st in ((prompt_states, st_p), (sample_states, st_s)):
                dst["C"].append(st[0]); dst["n"].append(st[1]); dst["m"].append(st[2])
            w_out, g_head = w_out_mlstm[j], g_head_mlstm[j]
        else:
            project = lambda x: _inproj(x, vec(g_pre_mix[layer]), in_gla, j,
                                        _pad_gate_columns(w_in_gla[j][:, MAIN_WIDTH:]),
                                        jnp.concatenate([head_scale, ones]).reshape(1, -1),
                                        silu_gate=True, gate_rows=False)
            (qk_p, v_p, gate_p, glr_p), (qk_s, v_s, gate_s, glr_s) = project(xp), project(xs)
            h, s_p, s_s = _gla_mixer((qk_p, v_p, glr_p), (qk_s, v_s, glr_s), w_gate2_gla[j], b_gate_gla[j],
                                     state_gla_S[j], bp, sp)
            prompt_states["S"].append(s_p)
            sample_states["S"].append(s_s)
            w_out, g_head = w_out_gla[j], g_head_gla[j]
        xp, xs = _block_tail(h[0], h[1], gate_p, gate_s, g_head.reshape(1, V_WIDTH).astype(F32), w_out.astype(BF16),
                             xp, xs,
                             vec(g_post_mix[layer]), vec(g_pre_ffn[layer]), vec(g_post_ffn[layer]),
                             ffn_gate, ffn_up, ffn_down, layer)

    stack = lambda xs_: jnp.stack(xs_)
    return (xp.reshape(bp, sp, d), xs.reshape(bs, ss, d),
            stack(prompt_states["C"]), stack(prompt_states["n"]), stack(prompt_states["m"]), stack(prompt_states["S"]),
            stack(sample_states["C"]), stack(sample_states["n"]), stack(sample_states["m"]), stack(sample_states["S"]))
```

```python
import functools

import numpy as np
import jax
import jax.numpy as jnp
from jax import lax
from jax.experimental import pallas as pl
from jax.experimental.pallas import tpu as pltpu

F32 = jnp.float32
BF16 = jnp.bfloat16

D_MODEL = 1024
N_HEADS = 4
DQK = 128
DV = 256
QK_WIDTH = N_HEADS * DQK
V_WIDTH = N_HEADS * DV
MAIN_WIDTH = 2 * QK_WIDTH + 2 * V_WIDTH
GATE_RANK = 16
GATE_TAU = 16.0
EPS = 1e-6
LOG2_E = 1.4426950408889634
LANES = 128
SUBLANES = 8
MXU_DIM = 256
VMEM_LIMIT = 56 * 1024 * 1024
MIXER_VMEM_LIMIT = 62 * 1024 * 1024

PROMPT_ROWS = 128
SAMPLE_ROWS = 64
LEVELS = (64, 32, 16, 8, 4, 2, 1)
TM = 512
TM_IN = 1024
GROUP_ROWS = 256
FF_SPLITS = 2


def _dot(a, b):
    return jnp.dot(a.astype(BF16), b.astype(BF16), preferred_element_type=F32)


def _dot_nt(a, b):
    return lax.dot_general(a.astype(BF16), b.astype(BF16), (((1,), (1,)), ((), ())),
                           preferred_element_type=F32)


def _dot_tn(a, b):
    return lax.dot_general(a.astype(BF16), b.astype(BF16), (((0,), (0,)), ((), ())),
                           preferred_element_type=F32)


def _split_hi_lo(x):
    hi = x.astype(BF16)
    lo = (x - hi.astype(F32)).astype(BF16)
    return hi, lo


def _log_sigmoid(x):
    return jnp.minimum(x, 0.0) - jnp.log(1.0 + jnp.exp(-jnp.abs(x)))


def _sigmoid(x):
    return 1.0 / (1.0 + jnp.exp(-x))


def _rms(x, gain):
    return x * lax.rsqrt(jnp.mean(x * x, axis=-1, keepdims=True) + EPS) * gain


def _rows_per_segment(values, seg_len):
    parts = [jnp.broadcast_to(v, (seg_len, v.shape[1])) for v in values]
    return parts[0] if len(parts) == 1 else jnp.concatenate(parts, axis=0)


def _emit_staggered(chains, lag=1):
    pending, live, step = list(chains), [], 0
    while pending or live:
        while pending and (lag == 0 or step % lag == 0):
            live.append(pending.pop(0))
            if lag:
                break
        step += 1
        for chain in list(live):
            if next(chain, StopIteration) is StopIteration:
                live.remove(chain)


def _const_spec(shape):
    nd = len(shape)
    return pl.BlockSpec(shape, lambda *_: (0,) * nd)


def _two_group_specs(n_prompt_blocks, width, single_buffer_sample=False):
    prompt = pl.BlockSpec((TM, width), lambda i: (jnp.minimum(i, n_prompt_blocks - 1), 0))
    mode = dict(pipeline_mode=pl.Buffered(1)) if single_buffer_sample else {}
    sample = pl.BlockSpec((TM, width), lambda i: (jnp.maximum(i - n_prompt_blocks, 0), 0), **mode)
    return prompt, sample


def _resident_spec(shape, layer=None):
    nd = len(shape)
    if layer is None:
        return pl.BlockSpec(shape, lambda i: (0,) * nd, pipeline_mode=pl.Buffered(1))
    return pl.BlockSpec((None,) + tuple(shape), lambda i: (layer,) + (0,) * nd, pipeline_mode=pl.Buffered(1))


def _inproj_body(x_ref, gain_ref, w_ref, wg_ref, scale_ref, oqk_ref, ov_ref, ogate_ref, og_ref,
                 *maybe_grow_ref, silu_gate):
    n_groups = x_ref.shape[0] // GROUP_ROWS
    groups = [slice(a * GROUP_ROWS, (a + 1) * GROUP_ROWS) for a in range(n_groups)]
    h = [None] * n_groups
    for a in range(n_groups + 1):
        if a < n_groups:
            h[a] = _rms(x_ref[groups[a], :], gain_ref[...]).astype(BF16)
        if a > 0:
            rows, hb = groups[a - 1], h[a - 1]
            narrow = jnp.dot(hb, wg_ref[...], preferred_element_type=F32)
            og_ref[rows, :] = narrow
            for grow_ref in maybe_grow_ref:
                grow_ref[:, rows] = narrow.T[0:SUBLANES, :]
            qk = jnp.dot(hb, w_ref[:, 0:2 * QK_WIDTH], preferred_element_type=F32)
            oqk_ref[rows, :] = qk * scale_ref[...]
            v = jnp.dot(hb, w_ref[:, 2 * QK_WIDTH:2 * QK_WIDTH + V_WIDTH], preferred_element_type=F32)
            ov_ref[rows, :] = v.astype(BF16)
            gate = jnp.dot(hb, w_ref[:, 2 * QK_WIDTH + V_WIDTH:MAIN_WIDTH], preferred_element_type=F32)
            ogate_ref[rows, :] = gate * _sigmoid(gate) if silu_gate else _sigmoid(gate)


def _inproj(x, gain, w_in, layer, w_gate, qk_scale, silu_gate, gate_rows):
    m = x.shape[0]
    assert m % TM_IN == 0
    rows_spec = lambda width: pl.BlockSpec((TM_IN, width), lambda i: (i, 0))
    out_specs = [rows_spec(2 * QK_WIDTH), rows_spec(V_WIDTH), rows_spec(V_WIDTH), rows_spec(LANES)]
    out_shape = [
        jax.ShapeDtypeStruct((m, 2 * QK_WIDTH), F32),
        jax.ShapeDtypeStruct((m, V_WIDTH), BF16),
        jax.ShapeDtypeStruct((m, V_WIDTH), F32),
        jax.ShapeDtypeStruct((m, LANES), F32),
    ]
    if gate_rows:
        out_specs.append(pl.BlockSpec((SUBLANES, TM_IN), lambda i: (0, i)))
        out_shape.append(jax.ShapeDtypeStruct((SUBLANES, m), F32))
    return pl.pallas_call(
        functools.partial(_inproj_body, silu_gate=silu_gate),
        grid=(m // TM_IN,),
        in_specs=[rows_spec(D_MODEL), _resident_spec((1, D_MODEL)), _resident_spec(w_in.shape[1:], layer),
                  _resident_spec((D_MODEL, LANES)), _resident_spec((1, 2 * QK_WIDTH))],
        out_specs=out_specs,
        out_shape=out_shape,
        compiler_params=pltpu.CompilerParams(dimension_semantics=("arbitrary",), vmem_limit_bytes=VMEM_LIMIT),
        name="inproj",
    )(x, gain, w_in, w_gate, qk_scale)


def _tail_body(hp_ref, hs_ref, gp_ref, gs_ref, ghead_ref, wo_ref, xp_ref, xs_ref, gpm_ref, gpf_ref, gqf_ref,
               wg_ref, wu_ref, wd_ref, op_ref, os_ref, *, n_prompt_blocks):
    is_prompt = pl.program_id(0) < n_prompt_blocks
    d_ff = wg_ref.shape[1]
    n_tiles = d_ff // MXU_DIM
    bounds = [MXU_DIM * ((n_tiles * t + FF_SPLITS - 1) // FF_SPLITS) for t in range(FF_SPLITS + 1)]

    def stages(raw_ref, gate_ref, x_ref, out_ref, rows):
        raw = raw_ref[rows, :]
        gain = ghead_ref[...]
        y = jnp.concatenate(
            [gate_ref[rows, h * DV:(h + 1) * DV] * _rms(raw[:, h * DV:(h + 1) * DV], gain[:, h * DV:(h + 1) * DV])
             for h in range(N_HEADS)], axis=1).astype(BF16)
        yield
        mix = jnp.dot(y, wo_ref[...], preferred_element_type=F32)
        yield
        x1 = x_ref[rows, :] + _rms(mix, gpm_ref[...])
        h = _rms(x1, gpf_ref[...]).astype(BF16)
        yield
        ffn = None
        for t in range(FF_SPLITS):
            cols = slice(bounds[t], bounds[t + 1])
            gate = jnp.dot(h, wg_ref[:, cols], preferred_element_type=F32)
            up = jnp.dot(h, wu_ref[:, cols], preferred_element_type=F32)
            yield
            act = (gate * _sigmoid(gate) * up).astype(BF16)
            yield
            part = jnp.dot(act, wd_ref[cols, :], preferred_element_type=F32)
            ffn = part if ffn is None else ffn + part
            yield
        out_ref[rows, :] = x1 + _rms(ffn, gqf_ref[...])
        yield

    def block(raw_ref, gate_ref, x_ref, out_ref):
        _emit_staggered([stages(raw_ref, gate_ref, x_ref, out_ref, slice(a * GROUP_ROWS, (a + 1) * GROUP_ROWS))
                         for a in range(TM // GROUP_ROWS)])

    pl.when(is_prompt)(lambda: block(hp_ref, gp_ref, xp_ref, op_ref))
    pl.when(jnp.logical_not(is_prompt))(lambda: block(hs_ref, gs_ref, xs_ref, os_ref))


def _block_tail(hp, hs, gate_p, gate_s, ghead, w_out, xp, xs, g_post_mix, g_pre_ffn, g_post_ffn,
                w_gate, w_up, w_down, layer):
    npb = xp.shape[0] // TM
    m = xp.shape[0] + xs.shape[0]
    d_ff = w_gate.shape[2]
    assert d_ff % MXU_DIM == 0
    in_p, in_s = _two_group_specs(npb, D_MODEL, single_buffer_sample=True)
    vec = _resident_spec((1, D_MODEL))
    return pl.pallas_call(
        functools.partial(_tail_body, n_prompt_blocks=npb),
        grid=(m // TM,),
        in_specs=[
            in_p, in_s,
            in_p, in_s,
            _resident_spec((1, V_WIDTH)),
            _resident_spec((V_WIDTH, D_MODEL)),
            in_p, in_s,
            vec, vec, vec,
            _resident_spec((D_MODEL, d_ff), layer), _resident_spec((D_MODEL, d_ff), layer),
            _resident_spec((d_ff, D_MODEL), layer),
        ],
        out_specs=list(_two_group_specs(npb, D_MODEL)),
        out_shape=[jax.ShapeDtypeStruct(xp.shape, F32), jax.ShapeDtypeStruct(xs.shape, F32)],
        compiler_params=pltpu.CompilerParams(dimension_semantics=("arbitrary",), vmem_limit_bytes=VMEM_LIMIT),
        name="block_tail",
    )(hp, hs, gate_p, gate_s, ghead, w_out, xp, xs, g_post_mix, g_pre_ffn, g_post_ffn, w_gate, w_up, w_down)


def _segment_structure(rows, seg_len):
    t = np.arange(rows)
    seg = t // seg_len
    same = seg[:, None] == seg[None, :]
    cum = same & (t[None, :] <= t[:, None])
    return t, seg, same, cum


def _mlstm_constants(rows, seg_len):
    _, _, same, cum = _segment_structure(rows, seg_len)
    col = np.concatenate([cum, same], axis=0).astype(np.float32)
    row = cum.T.astype(np.float32)
    return jnp.asarray(col, BF16), jnp.asarray(row, BF16)


def _mlstm_rows(qk_ref, v_ref, gcol_ref, grow_ref, bias_row_ref, bias_col_ref, ccol_ref, crow_ref,
                c_in, n_in, m_in, c_out, n_out, m_out, h_ref, rows, seg_len):
    nseg = rows // seg_len
    r = lax.broadcasted_iota(jnp.int32, (rows, rows), 0)
    c = lax.broadcasted_iota(jnp.int32, (rows, rows), 1)
    same = None if nseg == 1 else (r // seg_len) == (c // seg_len)
    lower = (c <= r) if nseg == 1 else same & (c <= r)
    row_seg = lax.broadcasted_iota(jnp.int32, (rows, DQK), 0) // seg_len

    pre_col = gcol_ref[...] + bias_row_ref[...]
    grow = grow_ref[0] if len(grow_ref.shape) == 3 else grow_ref[...]
    pre_row = grow + bias_col_ref[...]
    ccol = ccol_ref[...]
    crow = crow_ref[...]
    hi, lo = _split_hi_lo(_log_sigmoid(pre_col))
    sums_col = jnp.dot(ccol, hi, preferred_element_type=F32) + jnp.dot(ccol, lo, preferred_element_type=F32)
    hi, lo = _split_hi_lo(_log_sigmoid(pre_row))
    sums_row = jnp.dot(hi, crow, preferred_element_type=F32) + jnp.dot(lo, crow, preferred_element_type=F32)
    yield

    qk = qk_ref[...]
    c_prev = [[c_in[s, h] for h in range(N_HEADS)] for s in range(nseg)]
    n_state = [[n_in[s, h:h + 1, :] for h in range(N_HEADS)] for s in range(nseg)]
    m_state = [[m_in[s, h:h + 1, 0:1] for h in range(N_HEADS)] for s in range(nseg)]
    writes = []

    heads = range(N_HEADS)
    q = [qk[:, h * DQK:(h + 1) * DQK] for h in heads]
    k = [qk[:, QK_WIDTH + h * DQK:QK_WIDTH + (h + 1) * DQK] for h in heads]
    v_bf = [v_ref[:, h * DV:(h + 1) * DV] for h in heads]
    i_col = [pre_col[:, h:h + 1] for h in heads]
    b_col = [sums_col[0:rows, N_HEADS + h:N_HEADS + h + 1] for h in heads]
    b_last = [sums_col[rows:2 * rows, N_HEADS + h:N_HEADS + h + 1] for h in heads]
    src = [pre_row[h:h + 1, :] - sums_row[N_HEADS + h:N_HEADS + h + 1, :] for h in heads]
    m_prev = [_rows_per_segment([m_state[s][h] for s in range(nseg)], seg_len) for h in heads]
    n_prev = [_rows_per_segment([n_state[s][h] for s in range(nseg)], seg_len) for h in heads]

    dmat = [jnp.where(lower, b_col[h] + src[h], -jnp.inf) for h in heads]
    inter = [b_col[h] + m_prev[h] for h in heads]
    m_t = [jnp.maximum(inter[h], jnp.max(dmat[h], axis=1, keepdims=True)) for h in heads]
    if nseg == 1:
        m_new = [jnp.broadcast_to(m_t[h][rows - 1:rows, :], (rows, 1)) for h in heads]
    else:
        dlast = [jnp.where(same, b_last[h] + src[h], -jnp.inf) for h in heads]
        m_new = [jnp.maximum(b_last[h] + m_prev[h], jnp.max(dlast[h], axis=1, keepdims=True)) for h in heads]
    yield
    scores = [_dot_nt(q[h], k[h]) for h in heads]
    q_bf = [q[h].astype(BF16) for h in heads]
    qc = []
    for h in heads:
        parts = []
        for s in range(nseg):
            full = jnp.dot(q_bf[h], c_prev[s][h].astype(BF16), preferred_element_type=F32)
            parts.append(full[s * seg_len:(s + 1) * seg_len] if nseg > 1 else full)
        qc.append(parts[0] if nseg == 1 else jnp.concatenate(parts, axis=0))
    qn = [jnp.sum(q[h] * n_prev[h], axis=1, keepdims=True) for h in heads]
    yield

    w_inter = [jnp.exp(inter[h] - m_t[h]) for h in heads]
    a = [scores[h] * jnp.exp(dmat[h] - m_t[h]) for h in heads]
    w_s = [jnp.exp(b_last[h] - b_col[h] + i_col[h] - m_new[h]) for h in heads]
    decay = [jnp.exp(b_last[h] + m_prev[h] - m_new[h]) for h in heads]
    kw = [k[h] * w_s[h] for h in heads]
    yield

    num = [jnp.dot(a[h].astype(BF16), v_bf[h], preferred_element_type=F32) + w_inter[h] * qc[h] for h in heads]
    den = [jnp.sum(a[h], axis=1, keepdims=True) + w_inter[h] * qn[h] for h in heads]
    for h in heads:
        for s in range(nseg):
            kw_s = kw[h] if nseg == 1 else jnp.where(row_seg == s, kw[h], 0.0)
            dec = decay[h][s * seg_len:s * seg_len + 1, :]
            writes.append((s, h, dec * c_prev[s][h] + _dot_tn(kw_s, v_bf[h]),
                           dec * n_state[s][h] + jnp.sum(kw_s, axis=0, keepdims=True),
                           jnp.broadcast_to(m_new[h][s * seg_len:s * seg_len + 1, :], (1, LANES))))

    yield
    for h in heads:
        h_ref[:, h * DV:(h + 1) * DV] = num[h] * (1.0 / jnp.maximum(jnp.abs(den[h]), jnp.exp(-m_t[h])))

    for s, h, c_new, n_new, m_new in writes:
        c_out[s, h] = c_new
        n_out[s, h:h + 1, :] = n_new
        m_out[s, h:h + 1, :] = m_new
    yield


def _mlstm_body(qk_p, v_p, gcol_p, grow_p, qk_s, v_s, gcol_s, grow_s, brow_ref, bcol_ref,
                ccol_p, crow_p, ccol_s, crow_s, c_in, n_in, m_in,
                h_p, c_p, n_p, m_p, h_s, c_out, n_out, m_out, *, n_prompt_seq, seg_len):
    @pl.when(pl.program_id(0) == 0)
    def _():
        c_p[...] = jnp.zeros_like(c_p)
        n_p[...] = jnp.zeros_like(n_p)
        m_p[...] = jnp.zeros_like(m_p)

    chains = []
    for a in range(n_prompt_seq):
        one = pl.ds(a, 1)
        c, n, m = c_p.at[one], n_p.at[one], m_p.at[one]
        chains.append(_mlstm_rows(qk_p.at[a], v_p.at[a], gcol_p.at[a], grow_p.at[a], brow_ref, bcol_ref, ccol_p, crow_p,
                                  c, n, m, c, n, m, h_p.at[a], PROMPT_ROWS, PROMPT_ROWS))
    chains.append(_mlstm_rows(qk_s, v_s, gcol_s, grow_s, brow_ref, bcol_ref, ccol_s, crow_s,
                              c_in, n_in, m_in, c_out, n_out, m_out, h_s, SAMPLE_ROWS, seg_len))
    _emit_staggered(chains)


def _gla_constants(rows, seg_len):
    t, seg, _, cum = _segment_structure(rows, seg_len)
    level_id = np.full((rows, rows), -1, np.int32)
    level_id[t, t] = 0
    for idx, hb in enumerate(LEVELS):
        if 2 * hb > seg_len:
            continue
        block = t // (2 * hb)
        upper = (t % (2 * hb)) >= hb
        level_id[(block[:, None] == block[None, :]) & upper[:, None] & ~upper[None, :]] = idx + 1
    segind = (seg[:, None] == np.arange(LANES)[None, :]).astype(np.float32)
    return jnp.asarray(cum.astype(np.float32), BF16), jnp.asarray(level_id), jnp.asarray(segind, BF16)


def _row_bcast(x, row, n):
    return jnp.broadcast_to(x[row:row + 1, :], (n, x.shape[1]))


def _gla_rows(qk_ref, v_ref, glr_ref, w2_ref, bg_ref, cum_ref, lvl_ref, segind_ref,
              s_in, s_out, o_ref, rows, seg_len):
    nseg = rows // seg_len
    ntile = rows // SUBLANES
    row_id = lax.broadcasted_iota(jnp.int32, (rows, DQK), 0)
    row_seg = row_id // seg_len
    row_in_tile = row_id % SUBLANES

    pre = _dot(glr_ref[...], w2_ref[...]) + bg_ref[...]
    g_all = _log_sigmoid(pre) * (LOG2_E / GATE_TAU)
    g_hi, g_lo = _split_hi_lo(g_all)
    cum = cum_ref[...]
    bc_all = jnp.dot(cum, g_hi, preferred_element_type=F32) + jnp.dot(cum, g_lo, preferred_element_type=F32)
    level_tiles = [lvl_ref[t * SUBLANES:(t + 1) * SUBLANES, :] for t in range(ntile)]
    segind = segind_ref[...]
    total_col = (lax.dot_general(g_hi, segind, (((0,), (0,)), ((), ())), preferred_element_type=F32)
                 + lax.dot_general(g_lo, segind, (((0,), (0,)), ((), ())), preferred_element_type=F32))
    yield

    qk = qk_ref[...]
    s_prev = [[s_in[s, h] for h in range(N_HEADS)] for s in range(nseg)]
    writes = []

    heads = range(N_HEADS)
    sl = [slice(h * DQK, (h + 1) * DQK) for h in heads]
    q = [qk[:, sl[h]] for h in heads]
    k = [qk[:, QK_WIDTH + h * DQK:QK_WIDTH + (h + 1) * DQK] for h in heads]
    v_bf = [v_ref[:, h * DV:(h + 1) * DV] for h in heads]
    g = [g_all[:, sl[h]] for h in heads]
    bc = [bc_all[:, sl[h]] for h in heads]
    seg_total = [jnp.concatenate([_row_bcast(bc[h], (s + 1) * seg_len - 1, seg_len) for s in range(nseg)], axis=0)
                 if nseg > 1 else _row_bcast(bc[h], rows - 1, rows) for h in heads]

    qq = [(q[h] * jnp.exp2(bc[h])).astype(BF16) for h in heads]
    kk = [k[h] * jnp.exp2(seg_total[h] - bc[h]) for h in heads]
    o_inter = []
    for h in heads:
        parts = []
        for s in range(nseg):
            st = s_prev[s][h]
            oi = jnp.dot(qq[h], st.astype(BF16), preferred_element_type=F32)
            parts.append(oi[s * seg_len:(s + 1) * seg_len] if nseg > 1 else oi)
            kk_s = kk[h] if nseg == 1 else jnp.where(row_seg == s, kk[h], 0.0)
            writes.append((s, h, st * jnp.exp2(total_col[sl[h], s:s + 1]) + _dot_tn(kk_s, v_bf[h])))
        o_inter.append(parts[0] if nseg == 1 else jnp.concatenate(parts, axis=0))
    yield

    a = [[None] * ntile for h in heads]

    def put(h, t, level, z_rows):
        prev = 0.0 if a[h][t] is None else a[h][t]
        a[h][t] = jnp.where(level_tiles[t] == level, z_rows, prev)

    for h in heads:
        z = _dot_nt(q[h], k[h])
        for t in range(ntile):
            put(h, t, 0, z[t * SUBLANES:(t + 1) * SUBLANES])

    for idx, hb in enumerate(LEVELS):
        if 2 * hb > seg_len:
            continue
        level = idx + 1
        if hb >= SUBLANES:
            starts = range(0, rows, 2 * hb)
            for h in heads:
                e_parts, x_parts = [], []
                for r0 in starts:
                    ref = _row_bcast(bc[h], r0 + hb - 1, hb)
                    e_parts += [ref - bc[h][r0:r0 + hb], bc[h][r0 + hb:r0 + 2 * hb] - ref]
                    x_parts += [k[h][r0:r0 + hb], q[h][r0 + hb:r0 + 2 * hb]]
                u = jnp.concatenate(x_parts, axis=0) * jnp.exp2(jnp.concatenate(e_parts, axis=0))
                u_upper = jnp.concatenate([u[r0 + hb:r0 + 2 * hb] for r0 in starts], axis=0)
                z = _dot_nt(u_upper, u)
                upper_tiles = [t for r0 in starts for t in range((r0 + hb) // SUBLANES, (r0 + 2 * hb) // SUBLANES)]
                for i, t in enumerate(upper_tiles):
                    put(h, t, level, z[i * SUBLANES:(i + 1) * SUBLANES])
        else:
            upper = (row_id % (2 * hb)) >= hb
            for h in heads:
                if hb == 1:
                    e_l = jnp.where(upper, g[h], 0.0)
                else:
                    tiles = range(ntile)
                    if hb == 4:
                        ref = jnp.concatenate([_row_bcast(bc[h], t * SUBLANES + 3, SUBLANES) for t in tiles], axis=0)
                    else:
                        lo = jnp.concatenate([_row_bcast(bc[h], t * SUBLANES + 1, SUBLANES) for t in tiles], axis=0)
                        hi = jnp.concatenate([_row_bcast(bc[h], t * SUBLANES + 5, SUBLANES) for t in tiles], axis=0)
                        ref = jnp.where(row_in_tile < 4, lo, hi)
                    d = bc[h] - ref
                    e_l = jnp.where(upper, d, -d)
                u = jnp.where(upper, q[h], k[h]) * jnp.exp2(e_l)
                z = _dot_nt(u, u)
                for t in range(ntile):
                    put(h, t, level, z[t * SUBLANES:(t + 1) * SUBLANES])
        yield

    for h in heads:
        o_ref[:, h * DV:(h + 1) * DV] = jnp.dot(jnp.concatenate(a[h], axis=0).astype(BF16), v_bf[h],
                                                preferred_element_type=F32) + o_inter[h]

    for s, h, s_new in writes:
        s_out[s, h] = s_new
    yield


def _gla_body(qk_p, v_p, glr_p, qk_s, v_s, glr_s, w2_ref, bg_ref, cum_p, lvl_p, segind_p, cum_s, lvl_s, segind_s,
              s_in, o_p, s_p, o_s, s_out, *, n_prompt_seq, seg_len):
    @pl.when(pl.program_id(0) == 0)
    def _():
        s_p[...] = jnp.zeros_like(s_p)

    chains = []
    for a in range(n_prompt_seq):
        state = s_p.at[pl.ds(a, 1)]
        chains.append(_gla_rows(qk_p.at[a], v_p.at[a], glr_p.at[a], w2_ref, bg_ref, cum_p, lvl_p, segind_p,
                                state, state, o_p.at[a], PROMPT_ROWS, PROMPT_ROWS))
    chains.append(_gla_rows(qk_s, v_s, glr_s, w2_ref, bg_ref, cum_s, lvl_s, segind_s,
                            s_in, s_out, o_s, SAMPLE_ROWS, seg_len))
    _emit_staggered(chains)


def _mixer_params(n_axes):
    return pltpu.CompilerParams(dimension_semantics=("arbitrary",) * n_axes, vmem_limit_bytes=MIXER_VMEM_LIMIT)


def _state_spec(n, trailing):
    nd = 1 + len(trailing)
    return pl.BlockSpec((n,) + trailing, lambda *ids: (ids[0],) + (0,) * (nd - 1))


def _mlstm_mixer(prompt, sample, bias_i, bias_f, c0, n0, m0, n_prompt_seq, prompt_len):
    qk_p, v_p, gcol_p, grow_p = prompt
    qk_s, v_s, gcol_s, grow_s = sample
    n_sample_seq = c0.shape[0]
    seg_len = qk_s.shape[0] // n_sample_seq
    bias = jnp.concatenate([bias_i, bias_f]).astype(F32)
    bias_row = jnp.zeros((1, LANES), F32).at[0, :2 * N_HEADS].set(bias)
    bias_col = bias.reshape(2 * N_HEADS, 1)

    nb, plen = n_prompt_seq, prompt_len
    steps = plen // PROMPT_ROWS
    spb = SAMPLE_ROWS // seg_len
    assert n_sample_seq == steps * spb, "one block of sample sequences per prompt chunk step"
    ccol_p, crow_p = _mlstm_constants(PROMPT_ROWS, PROMPT_ROWS)
    ccol_s, crow_s = _mlstm_constants(SAMPLE_ROWS, seg_len)
    consts = [bias_row, bias_col, ccol_p, crow_p, ccol_s, crow_s]
    m0b = jnp.broadcast_to(m0[:, :, None], m0.shape + (LANES,))
    grow_p = jnp.transpose(grow_p.reshape(2 * N_HEADS, nb, plen), (1, 0, 2))
    grow_s = jnp.transpose(grow_s.reshape(2 * N_HEADS, steps, SAMPLE_ROWS), (1, 0, 2))
    seq_rows = lambda width: pl.BlockSpec((nb, PROMPT_ROWS, width), lambda i: (0, i, 0))
    blk_rows = lambda width: pl.BlockSpec((SAMPLE_ROWS, width), lambda i: (i, 0))
    whole = lambda shape: pl.BlockSpec(shape, lambda i: (0,) * len(shape))
    state_specs = [_state_spec(spb, (N_HEADS, DQK, DV)), _state_spec(spb, (N_HEADS, DQK)),
                   _state_spec(spb, (N_HEADS, LANES))]
    h_p, c_p, n_p, m_p, h_s, c_s, n_s, m_s = pl.pallas_call(
        functools.partial(_mlstm_body, n_prompt_seq=nb, seg_len=seg_len),
        grid=(steps,),
        in_specs=[seq_rows(2 * QK_WIDTH), seq_rows(V_WIDTH), seq_rows(LANES),
                  pl.BlockSpec((nb, 2 * N_HEADS, PROMPT_ROWS), lambda i: (0, 0, i)),
                  blk_rows(2 * QK_WIDTH), blk_rows(V_WIDTH), blk_rows(LANES),
                  pl.BlockSpec((1, 2 * N_HEADS, SAMPLE_ROWS), lambda i: (i, 0, 0))]
        + [_const_spec(a.shape) for a in consts] + state_specs,
        out_specs=[seq_rows(V_WIDTH), whole((nb, N_HEADS, DQK, DV)), whole((nb, N_HEADS, DQK)),
                   whole((nb, N_HEADS, LANES)), blk_rows(V_WIDTH)] + state_specs,
        out_shape=[
            jax.ShapeDtypeStruct((nb, plen, V_WIDTH), F32),
            jax.ShapeDtypeStruct((nb, N_HEADS, DQK, DV), F32),
            jax.ShapeDtypeStruct((nb, N_HEADS, DQK), F32),
            jax.ShapeDtypeStruct((nb, N_HEADS, LANES), F32),
            jax.ShapeDtypeStruct((n_sample_seq * seg_len, V_WIDTH), F32),
            jax.ShapeDtypeStruct(c0.shape, F32),
            jax.ShapeDtypeStruct(n0.shape, F32),
            jax.ShapeDtypeStruct(m0b.shape, F32),
        ],
        compiler_params=_mixer_params(1),
        name="mlstm_mixer",
    )(qk_p.reshape(nb, plen, -1), v_p.reshape(nb, plen, -1), gcol_p.reshape(nb, plen, -1), grow_p,
      qk_s, v_s, gcol_s, grow_s, *consts, c0, n0, m0b)

    return (h_p.reshape(nb * plen, V_WIDTH), h_s), (c_p, n_p, m_p[:, :, 0]), (c_s, n_s, m_s[:, :, 0])


def _gla_mixer(prompt, sample, w_gate2, b_gate, s0, n_prompt_seq, prompt_len):
    qk_p, v_p, glr_p = prompt
    qk_s, v_s, glr_s = sample
    n_sample_seq = s0.shape[0]
    seg_len = qk_s.shape[0] // n_sample_seq
    nb, plen = n_prompt_seq, prompt_len
    steps = plen // PROMPT_ROWS
    spb = SAMPLE_ROWS // seg_len
    assert n_sample_seq == steps * spb, "one block of sample sequences per prompt chunk step"
    w2 = jnp.zeros((LANES, QK_WIDTH), BF16).at[:GATE_RANK].set(w_gate2.astype(BF16))
    bg = b_gate.reshape(1, QK_WIDTH).astype(F32)
    consts = [w2, bg, *_gla_constants(PROMPT_ROWS, PROMPT_ROWS), *_gla_constants(SAMPLE_ROWS, seg_len)]
    seq_rows = lambda width: pl.BlockSpec((nb, PROMPT_ROWS, width), lambda i: (0, i, 0))
    blk_rows = lambda width: pl.BlockSpec((SAMPLE_ROWS, width), lambda i: (i, 0))
    state_spec = _state_spec(spb, (N_HEADS, DQK, DV))
    o_p, s_p, o_s, s_s = pl.pallas_call(
        functools.partial(_gla_body, n_prompt_seq=nb, seg_len=seg_len),
        grid=(steps,),
        in_specs=[seq_rows(2 * QK_WIDTH), seq_rows(V_WIDTH), seq_rows(LANES),
                  blk_rows(2 * QK_WIDTH), blk_rows(V_WIDTH), blk_rows(LANES)]
        + [_const_spec(a.shape) for a in consts] + [state_spec],
        out_specs=[seq_rows(V_WIDTH), pl.BlockSpec((nb, N_HEADS, DQK, DV), lambda i: (0, 0, 0, 0)),
                   blk_rows(V_WIDTH), state_spec],
        out_shape=[
            jax.ShapeDtypeStruct((nb, plen, V_WIDTH), F32),
            jax.ShapeDtypeStruct((nb, N_HEADS, DQK, DV), F32),
            jax.ShapeDtypeStruct((n_sample_seq * seg_len, V_WIDTH), F32),
            jax.ShapeDtypeStruct(s0.shape, F32),
        ],
        compiler_params=_mixer_params(1),
        name="gla_mixer",
    )(qk_p.reshape(nb, plen, -1), v_p.reshape(nb, plen, -1), glr_p.reshape(nb, plen, -1),
      qk_s, v_s, glr_s, *consts, s0)

    return (o_p.reshape(nb * plen, V_WIDTH), o_s), s_p, s_s


def _pad_gate_columns(w):
    return jnp.zeros((D_MODEL, LANES), BF16).at[:, :w.shape[1]].set(w.astype(BF16))


def kernel(x_prompt, x_sample, state_mlstm_C, state_mlstm_n, state_mlstm_m, state_gla_S, g_pre_mix, g_post_mix, g_pre_ffn, g_post_ffn, w_in_mlstm, b_i_mlstm, b_f_mlstm, g_head_mlstm, w_out_mlstm, w_in_gla, w_gate2_gla, b_gate_gla, g_head_gla, w_out_gla, w_ffn_gate, w_ffn_up, w_ffn_down):
    bp, sp, d = x_prompt.shape
    bs, ss, _ = x_sample.shape
    depth = g_pre_mix.shape[0]
    assert d == D_MODEL and sp % PROMPT_ROWS == 0 and SAMPLE_ROWS % ss == 0 and (bs * ss) % SAMPLE_ROWS == 0
    assert (bp * sp) % TM == 0 and (bs * ss) % TM == 0

    xp = x_prompt.reshape(bp * sp, d)
    xs = x_sample.reshape(bs * ss, d)
    vec = lambda g: g.reshape(1, D_MODEL).astype(F32)

    prompt_states = {"C": [], "n": [], "m": [], "S": []}
    sample_states = {"C": [], "n": [], "m": [], "S": []}
    ffn_gate, ffn_up, ffn_down = (w.astype(BF16) for w in (w_ffn_gate, w_ffn_up, w_ffn_down))
    in_mlstm, in_gla = w_in_mlstm.astype(BF16), w_in_gla.astype(BF16)
    ones = jnp.ones((QK_WIDTH,), F32)
    head_scale = jnp.full((QK_WIDTH,), DQK ** -0.5, F32)
    for layer in range(depth):
        j = layer // 2
        if layer % 2 == 0:
            project = lambda x: _inproj(x, vec(g_pre_mix[layer]), in_mlstm, j,
                                        _pad_gate_columns(w_in_mlstm[j][:, MAIN_WIDTH:]),
                                        jnp.concatenate([ones, head_scale]).reshape(1, -1),
                                        silu_gate=False, gate_rows=True)
            (qk_p, v_p, gate_p, gcol_p, grow_p), (qk_s, v_s, gate_s, gcol_s, grow_s) = project(xp), project(xs)
            h, st_p, st_s = _mlstm_mixer((qk_p, v_p, gcol_p, grow_p), (qk_s, v_s, gcol_s, grow_s),
                                         b_i_mlstm[j], b_f_mlstm[j],
                                         state_mlstm_C[j], state_mlstm_n[j], state_mlstm_m[j], bp, sp)
            for dst, st in ((prompt_states, st_p), (sample_states, st_s)):
                dst["C"].append(st[0]); dst["n"].append(st[1]); dst["m"].append(st[2])
            w_out, g_head = w_out_mlstm[j], g_head_mlstm[j]
        else:
            project = lambda x: _inproj(x, vec(g_pre_mix[layer]), in_gla, j,
                                        _pad_gate_columns(w_in_gla[j][:, MAIN_WIDTH:]),
                                        jnp.concatenate([head_scale, ones]).reshape(1, -1),
                                        silu_gate=True, gate_rows=False)
            (qk_p, v_p, gate_p, glr_p), (qk_s, v_s, gate_s, glr_s) = project(xp), project(xs)
            h, s_p, s_s = _gla_mixer((qk_p, v_p, glr_p), (qk_s, v_s, glr_s), w_gate2_gla[j], b_gate_gla[j],
                                     state_gla_S[j], bp, sp)
            prompt_states["S"].append(s_p)
            sample_states["S"].append(s_s)
            w_out, g_head = w_out_gla[j], g_head_gla[j]
        xp, xs = _block_tail(h[0], h[1], gate_p, gate_s, g_head.reshape(1, V_WIDTH).astype(F32), w_out.astype(BF16),
                             xp, xs,
                             vec(g_post_mix[layer]), vec(g_pre_ffn[layer]), vec(g_post_ffn[layer]),
                             ffn_gate, ffn_up, ffn_down, layer)

    stack = lambda xs_: jnp.stack(xs_)
    return (xp.reshape(bp, sp, d), xs.reshape(bs, ss, d),
            stack(prompt_states["C"]), stack(prompt_states["n"]), stack(prompt_states["m"]), stack(prompt_states["S"]),
            stack(sample_states["C"]), stack(sample_states["n"]), stack(sample_states["m"]), stack(sample_states["S"]))
```

```python
import functools

import numpy as np
import jax
import jax.numpy as jnp
from jax import lax
from jax.experimental import pallas as pl
from jax.experimental.pallas import tpu as pltpu

F32 = jnp.float32
BF16 = jnp.bfloat16

D_MODEL = 1024
N_HEADS = 4
DQK = 128
DV = 256
QK_WIDTH = N_HEADS * DQK
V_WIDTH = N_HEADS * DV
MAIN_WIDTH = 2 * QK_WIDTH + 2 * V_WIDTH
GATE_RANK = 16
GATE_TAU = 16.0
EPS = 1e-6
LOG2_E = 1.4426950408889634
LANES = 128
SUBLANES = 8
MXU_DIM = 256
VMEM_LIMIT = 56 * 1024 * 1024
MIXER_VMEM_LIMIT = 62 * 1024 * 1024

PROMPT_ROWS = 128
SAMPLE_ROWS = 64
LEVELS = (64, 32, 16, 8, 4, 2, 1)
TM = 512
TM_IN = 1024
GROUP_ROWS = 256
FF_SPLITS = 2


def _dot(a, b):
    return jnp.dot(a.astype(BF16), b.astype(BF16), preferred_element_type=F32)


def _dot_nt(a, b):
    return lax.dot_general(a.astype(BF16), b.astype(BF16), (((1,), (1,)), ((), ())),
                           preferred_element_type=F32)


def _dot_tn(a, b):
    return lax.dot_general(a.astype(BF16), b.astype(BF16), (((0,), (0,)), ((), ())),
                           preferred_element_type=F32)


def _split_hi_lo(x):
    hi = x.astype(BF16)
    lo = (x - hi.astype(F32)).astype(BF16)
    return hi, lo


def _log_sigmoid(x):
    return jnp.minimum(x, 0.0) - jnp.log(1.0 + jnp.exp(-jnp.abs(x)))


def _sigmoid(x):
    return 1.0 / (1.0 + jnp.exp(-x))


def _rms(x, gain):
    return x * lax.rsqrt(jnp.mean(x * x, axis=-1, keepdims=True) + EPS) * gain


def _rows_per_segment(values, seg_len):
    parts = [jnp.broadcast_to(v, (seg_len, v.shape[1])) for v in values]
    return parts[0] if len(parts) == 1 else jnp.concatenate(parts, axis=0)


def _emit_staggered(chains):
    pending, live = list(chains), []
    while pending or live:
        if pending:
            live.append(pending.pop(0))
        for chain in list(live):
            if next(chain, StopIteration) is StopIteration:
                live.remove(chain)


def _const_spec(shape):
    nd = len(shape)
    return pl.BlockSpec(shape, lambda *_: (0,) * nd)


def _two_group_specs(n_prompt_blocks, width, single_buffer_sample=False):
    prompt = pl.BlockSpec((TM, width), lambda i: (jnp.minimum(i, n_prompt_blocks - 1), 0))
    mode = dict(pipeline_mode=pl.Buffered(1)) if single_buffer_sample else {}
    sample = pl.BlockSpec((TM, width), lambda i: (jnp.maximum(i - n_prompt_blocks, 0), 0), **mode)
    return prompt, sample


def _resident_spec(shape, layer=None):
    nd = len(shape)
    if layer is None:
        return pl.BlockSpec(shape, lambda i: (0,) * nd, pipeline_mode=pl.Buffered(1))
    return pl.BlockSpec((None,) + tuple(shape), lambda i: (layer,) + (0,) * nd, pipeline_mode=pl.Buffered(1))


def _inproj_body(x_ref, gain_ref, w_ref, wg_ref, scale_ref, oqk_ref, ov_ref, ogate_ref, og_ref,
                 *maybe_grow_ref, silu_gate):
    n_groups = x_ref.shape[0] // GROUP_ROWS
    groups = [slice(a * GROUP_ROWS, (a + 1) * GROUP_ROWS) for a in range(n_groups)]
    h = [None] * n_groups
    for a in range(n_groups + 1):
        if a < n_groups:
            h[a] = _rms(x_ref[groups[a], :], gain_ref[...]).astype(BF16)
        if a > 0:
            rows, hb = groups[a - 1], h[a - 1]
            narrow = jnp.dot(hb, wg_ref[...], preferred_element_type=F32)
            og_ref[rows, :] = narrow
            for grow_ref in maybe_grow_ref:
                grow_ref[:, rows] = narrow.T[0:SUBLANES, :]
            qk = jnp.dot(hb, w_ref[:, 0:2 * QK_WIDTH], preferred_element_type=F32)
            oqk_ref[rows, :] = qk * scale_ref[...]
            v = jnp.dot(hb, w_ref[:, 2 * QK_WIDTH:2 * QK_WIDTH + V_WIDTH], preferred_element_type=F32)
            ov_ref[rows, :] = v.astype(BF16)
            gate = jnp.dot(hb, w_ref[:, 2 * QK_WIDTH + V_WIDTH:MAIN_WIDTH], preferred_element_type=F32)
            ogate_ref[rows, :] = gate * _sigmoid(gate) if silu_gate else _sigmoid(gate)


def _inproj(x, gain, w_in, layer, w_gate, qk_scale, silu_gate, gate_rows):
    m = x.shape[0]
    tm = TM_IN if m >= 4 * TM_IN else GROUP_ROWS
    assert m % tm == 0
    rows_spec = lambda width: pl.BlockSpec((tm, width), lambda i: (i, 0))
    out_specs = [rows_spec(2 * QK_WIDTH), rows_spec(V_WIDTH), rows_spec(V_WIDTH), rows_spec(LANES)]
    out_shape = [
        jax.ShapeDtypeStruct((m, 2 * QK_WIDTH), F32),
        jax.ShapeDtypeStruct((m, V_WIDTH), BF16),
        jax.ShapeDtypeStruct((m, V_WIDTH), F32),
        jax.ShapeDtypeStruct((m, LANES), F32),
    ]
    if gate_rows:
        out_specs.append(pl.BlockSpec((SUBLANES, tm), lambda i: (0, i)))
        out_shape.append(jax.ShapeDtypeStruct((SUBLANES, m), F32))
    return pl.pallas_call(
        functools.partial(_inproj_body, silu_gate=silu_gate),
        grid=(m // tm,),
        in_specs=[rows_spec(D_MODEL), _resident_spec((1, D_MODEL)), _resident_spec(w_in.shape[1:], layer),
                  _resident_spec((D_MODEL, LANES)), _resident_spec((1, 2 * QK_WIDTH))],
        out_specs=out_specs,
        out_shape=out_shape,
        compiler_params=pltpu.CompilerParams(dimension_semantics=("arbitrary",), vmem_limit_bytes=VMEM_LIMIT),
        name="inproj",
    )(x, gain, w_in, w_gate, qk_scale)


def _tail_body(hp_ref, hs_ref, gp_ref, gs_ref, ghead_ref, wo_ref, xp_ref, xs_ref, gpm_ref, gpf_ref, gqf_ref,
               wg_ref, wu_ref, wd_ref, op_ref, os_ref, *, n_prompt_blocks):
    is_prompt = pl.program_id(0) < n_prompt_blocks
    d_ff = wg_ref.shape[1]
    n_tiles = d_ff // MXU_DIM
    bounds = [MXU_DIM * ((n_tiles * t + FF_SPLITS - 1) // FF_SPLITS) for t in range(FF_SPLITS + 1)]

    def stages(raw_ref, gate_ref, x_ref, out_ref, rows):
        raw = raw_ref[rows, :]
        gain = ghead_ref[...]
        y = jnp.concatenate(
            [gate_ref[rows, h * DV:(h + 1) * DV] * _rms(raw[:, h * DV:(h + 1) * DV], gain[:, h * DV:(h + 1) * DV])
             for h in range(N_HEADS)], axis=1).astype(BF16)
        yield
        mix = jnp.dot(y, wo_ref[...], preferred_element_type=F32)
        yield
        x1 = x_ref[rows, :] + _rms(mix, gpm_ref[...])
        h = _rms(x1, gpf_ref[...]).astype(BF16)
        yield
        ffn = None
        for t in range(FF_SPLITS):
            cols = slice(bounds[t], bounds[t + 1])
            gate = jnp.dot(h, wg_ref[:, cols], preferred_element_type=F32)
            up = jnp.dot(h, wu_ref[:, cols], preferred_element_type=F32)
            yield
            act = (gate * _sigmoid(gate) * up).astype(BF16)
            yield
            part = jnp.dot(act, wd_ref[cols, :], preferred_element_type=F32)
            ffn = part if ffn is None else ffn + part
            yield
        out_ref[rows, :] = x1 + _rms(ffn, gqf_ref[...])
        yield

    def block(raw_ref, gate_ref, x_ref, out_ref):
        _emit_staggered([stages(raw_ref, gate_ref, x_ref, out_ref, slice(a * GROUP_ROWS, (a + 1) * GROUP_ROWS))
                         for a in range(TM // GROUP_ROWS)])

    pl.when(is_prompt)(lambda: block(hp_ref, gp_ref, xp_ref, op_ref))
    pl.when(jnp.logical_not(is_prompt))(lambda: block(hs_ref, gs_ref, xs_ref, os_ref))


def _block_tail(hp, hs, gate_p, gate_s, ghead, w_out, xp, xs, g_post_mix, g_pre_ffn, g_post_ffn,
                w_gate, w_up, w_down, layer):
    npb = xp.shape[0] // TM
    m = xp.shape[0] + xs.shape[0]
    d_ff = w_gate.shape[2]
    assert d_ff % MXU_DIM == 0
    in_p, in_s = _two_group_specs(npb, D_MODEL, single_buffer_sample=True)
    vec = _resident_spec((1, D_MODEL))
    return pl.pallas_call(
        functools.partial(_tail_body, n_prompt_blocks=npb),
        grid=(m // TM,),
        in_specs=[
            in_p, in_s,
            in_p, in_s,
            _resident_spec((1, V_WIDTH)),
            _resident_spec((V_WIDTH, D_MODEL)),
            in_p, in_s,
            vec, vec, vec,
            _resident_spec((D_MODEL, d_ff), layer), _resident_spec((D_MODEL, d_ff), layer),
            _resident_spec((d_ff, D_MODEL), layer),
        ],
        out_specs=list(_two_group_specs(npb, D_MODEL)),
        out_shape=[jax.ShapeDtypeStruct(xp.shape, F32), jax.ShapeDtypeStruct(xs.shape, F32)],
        compiler_params=pltpu.CompilerParams(dimension_semantics=("arbitrary",), vmem_limit_bytes=VMEM_LIMIT),
        name="block_tail",
    )(hp, hs, gate_p, gate_s, ghead, w_out, xp, xs, g_post_mix, g_pre_ffn, g_post_ffn, w_gate, w_up, w_down)


def _segment_structure(rows, seg_len):
    t = np.arange(rows)
    seg = t // seg_len
    same = seg[:, None] == seg[None, :]
    cum = same & (t[None, :] <= t[:, None])
    return t, seg, same, cum


def _mlstm_constants(rows, seg_len):
    _, _, same, cum = _segment_structure(rows, seg_len)
    col = np.concatenate([cum, same], axis=0).astype(np.float32)
    row = cum.T.astype(np.float32)
    return jnp.asarray(col, BF16), jnp.asarray(row, BF16)


def _mlstm_rows(qk_ref, v_ref, gcol_ref, grow_ref, bias_row_ref, bias_col_ref, ccol_ref, crow_ref,
                c_in, n_in, m_in, c_out, n_out, m_out, h_ref, rows, seg_len):
    nseg = rows // seg_len
    r = lax.broadcasted_iota(jnp.int32, (rows, rows), 0)
    c = lax.broadcasted_iota(jnp.int32, (rows, rows), 1)
    same = None if nseg == 1 else (r // seg_len) == (c // seg_len)
    lower = (c <= r) if nseg == 1 else same & (c <= r)
    row_seg = lax.broadcasted_iota(jnp.int32, (rows, DQK), 0) // seg_len

    pre_col = gcol_ref[...] + bias_row_ref[...]
    grow = grow_ref[0] if len(grow_ref.shape) == 3 else grow_ref[...]
    pre_row = grow + bias_col_ref[...]
    ccol = ccol_ref[...]
    crow = crow_ref[...]
    hi, lo = _split_hi_lo(_log_sigmoid(pre_col))
    sums_col = jnp.dot(ccol, hi, preferred_element_type=F32) + jnp.dot(ccol, lo, preferred_element_type=F32)
    hi, lo = _split_hi_lo(_log_sigmoid(pre_row))
    sums_row = jnp.dot(hi, crow, preferred_element_type=F32) + jnp.dot(lo, crow, preferred_element_type=F32)
    yield

    qk = qk_ref[...]
    c_prev = [[c_in[s, h] for h in range(N_HEADS)] for s in range(nseg)]
    n_state = [[n_in[s, h:h + 1, :] for h in range(N_HEADS)] for s in range(nseg)]
    m_state = [[m_in[s, h:h + 1, 0:1] for h in range(N_HEADS)] for s in range(nseg)]
    writes = []

    heads = range(N_HEADS)
    q = [qk[:, h * DQK:(h + 1) * DQK] for h in heads]
    k = [qk[:, QK_WIDTH + h * DQK:QK_WIDTH + (h + 1) * DQK] for h in heads]
    v_bf = [v_ref[:, h * DV:(h + 1) * DV] for h in heads]
    i_col = [pre_col[:, h:h + 1] for h in heads]
    b_col = [sums_col[0:rows, N_HEADS + h:N_HEADS + h + 1] for h in heads]
    b_last = [sums_col[rows:2 * rows, N_HEADS + h:N_HEADS + h + 1] for h in heads]
    src = [pre_row[h:h + 1, :] - sums_row[N_HEADS + h:N_HEADS + h + 1, :] for h in heads]
    m_prev = [_rows_per_segment([m_state[s][h] for s in range(nseg)], seg_len) for h in heads]
    n_prev = [_rows_per_segment([n_state[s][h] for s in range(nseg)], seg_len) for h in heads]

    dmat = [jnp.where(lower, b_col[h] + src[h], -jnp.inf) for h in heads]
    inter = [b_col[h] + m_prev[h] for h in heads]
    m_t = [jnp.maximum(inter[h], jnp.max(dmat[h], axis=1, keepdims=True)) for h in heads]
    if nseg == 1:
        m_new = [jnp.broadcast_to(m_t[h][rows - 1:rows, :], (rows, 1)) for h in heads]
    else:
        dlast = [jnp.where(same, b_last[h] + src[h], -jnp.inf) for h in heads]
        m_new = [jnp.maximum(b_last[h] + m_prev[h], jnp.max(dlast[h], axis=1, keepdims=True)) for h in heads]
    yield
    scores = [_dot_nt(q[h], k[h]) for h in heads]
    q_bf = [q[h].astype(BF16) for h in heads]
    qc = []
    for h in heads:
        parts = []
        for s in range(nseg):
            full = jnp.dot(q_bf[h], c_prev[s][h].astype(BF16), preferred_element_type=F32)
            parts.append(full[s * seg_len:(s + 1) * seg_len] if nseg > 1 else full)
        qc.append(parts[0] if nseg == 1 else jnp.concatenate(parts, axis=0))
    qn = [jnp.sum(q[h] * n_prev[h], axis=1, keepdims=True) for h in heads]
    yield

    w_inter = [jnp.exp(inter[h] - m_t[h]) for h in heads]
    a = [scores[h] * jnp.exp(dmat[h] - m_t[h]) for h in heads]
    w_s = [jnp.exp(b_last[h] - b_col[h] + i_col[h] - m_new[h]) for h in heads]
    decay = [jnp.exp(b_last[h] + m_prev[h] - m_new[h]) for h in heads]
    kw = [k[h] * w_s[h] for h in heads]
    yield

    num = [jnp.dot(a[h].astype(BF16), v_bf[h], preferred_element_type=F32) + w_inter[h] * qc[h] for h in heads]
    den = [jnp.sum(a[h], axis=1, keepdims=True) + w_inter[h] * qn[h] for h in heads]
    for h in heads:
        for s in range(nseg):
            kw_s = kw[h] if nseg == 1 else jnp.where(row_seg == s, kw[h], 0.0)
            dec = decay[h][s * seg_len:s * seg_len + 1, :]
            writes.append((s, h, dec * c_prev[s][h] + _dot_tn(kw_s, v_bf[h]),
                           dec * n_state[s][h] + jnp.sum(kw_s, axis=0, keepdims=True),
                           jnp.broadcast_to(m_new[h][s * seg_len:s * seg_len + 1, :], (1, LANES))))

    yield
    for h in heads:
        h_ref[:, h * DV:(h + 1) * DV] = num[h] * (1.0 / jnp.maximum(jnp.abs(den[h]), jnp.exp(-m_t[h])))

    for s, h, c_new, n_new, m_new in writes:
        c_out[s, h] = c_new
        n_out[s, h:h + 1, :] = n_new
        m_out[s, h:h + 1, :] = m_new
    yield


def _mlstm_body(qk_p, v_p, gcol_p, grow_p, qk_s, v_s, gcol_s, grow_s, brow_ref, bcol_ref,
                ccol_p, crow_p, ccol_s, crow_s, c_in, n_in, m_in,
                h_p, c_p, n_p, m_p, h_s, c_out, n_out, m_out, *, n_prompt_seq, seg_len):
    @pl.when(pl.program_id(0) == 0)
    def _():
        c_p[...] = jnp.zeros_like(c_p)
        n_p[...] = jnp.zeros_like(n_p)
        m_p[...] = jnp.zeros_like(m_p)

    chains = []
    for a in range(n_prompt_seq):
        one = pl.ds(a, 1)
        c, n, m = c_p.at[one], n_p.at[one], m_p.at[one]
        chains.append(_mlstm_rows(qk_p.at[a], v_p.at[a], gcol_p.at[a], grow_p.at[a], brow_ref, bcol_ref, ccol_p, crow_p,
                                  c, n, m, c, n, m, h_p.at[a], PROMPT_ROWS, PROMPT_ROWS))
    chains.append(_mlstm_rows(qk_s, v_s, gcol_s, grow_s, brow_ref, bcol_ref, ccol_s, crow_s,
                              c_in, n_in, m_in, c_out, n_out, m_out, h_s, SAMPLE_ROWS, seg_len))
    _emit_staggered(chains)


def _gla_constants(rows, seg_len):
    t, seg, _, cum = _segment_structure(rows, seg_len)
    level_id = np.full((rows, rows), -1, np.int32)
    level_id[t, t] = 0
    for idx, hb in enumerate(LEVELS):
        if 2 * hb > seg_len:
            continue
        block = t // (2 * hb)
        upper = (t % (2 * hb)) >= hb
        level_id[(block[:, None] == block[None, :]) & upper[:, None] & ~upper[None, :]] = idx + 1
    segind = (seg[:, None] == np.arange(LANES)[None, :]).astype(np.float32)
    return jnp.asarray(cum.astype(np.float32), BF16), jnp.asarray(level_id), jnp.asarray(segind, BF16)


def _row_bcast(x, row, n):
    return jnp.broadcast_to(x[row:row + 1, :], (n, x.shape[1]))


def _gla_rows(qk_ref, v_ref, glr_ref, w2_ref, bg_ref, cum_ref, lvl_ref, segind_ref,
              s_in, s_out, o_ref, rows, seg_len):
    nseg = rows // seg_len
    ntile = rows // SUBLANES
    row_id = lax.broadcasted_iota(jnp.int32, (rows, DQK), 0)
    row_seg = row_id // seg_len
    row_in_tile = row_id % SUBLANES

    pre = _dot(glr_ref[...], w2_ref[...]) + bg_ref[...]
    g_all = _log_sigmoid(pre) * (LOG2_E / GATE_TAU)
    g_hi, g_lo = _split_hi_lo(g_all)
    cum = cum_ref[...]
    bc_all = jnp.dot(cum, g_hi, preferred_element_type=F32) + jnp.dot(cum, g_lo, preferred_element_type=F32)
    level_tiles = [lvl_ref[t * SUBLANES:(t + 1) * SUBLANES, :] for t in range(ntile)]
    segind = segind_ref[...]
    total_col = (lax.dot_general(g_hi, segind, (((0,), (0,)), ((), ())), preferred_element_type=F32)
                 + lax.dot_general(g_lo, segind, (((0,), (0,)), ((), ())), preferred_element_type=F32))
    yield

    qk = qk_ref[...]
    s_prev = [[s_in[s, h] for h in range(N_HEADS)] for s in range(nseg)]
    writes = []

    heads = range(N_HEADS)
    sl = [slice(h * DQK, (h + 1) * DQK) for h in heads]
    q = [qk[:, sl[h]] for h in heads]
    k = [qk[:, QK_WIDTH + h * DQK:QK_WIDTH + (h + 1) * DQK] for h in heads]
    v_bf = [v_ref[:, h * DV:(h + 1) * DV] for h in heads]
    g = [g_all[:, sl[h]] for h in heads]
    bc = [bc_all[:, sl[h]] for h in heads]
    seg_total = [jnp.concatenate([_row_bcast(bc[h], (s + 1) * seg_len - 1, seg_len) for s in range(nseg)], axis=0)
                 if nseg > 1 else _row_bcast(bc[h], rows - 1, rows) for h in heads]

    qq = [(q[h] * jnp.exp2(bc[h])).astype(BF16) for h in heads]
    kk = [k[h] * jnp.exp2(seg_total[h] - bc[h]) for h in heads]
    o_inter = []
    for h in heads:
        parts = []
        for s in range(nseg):
            st = s_prev[s][h]
            oi = jnp.dot(qq[h], st.astype(BF16), preferred_element_type=F32)
            parts.append(oi[s * seg_len:(s + 1) * seg_len] if nseg > 1 else oi)
            kk_s = kk[h] if nseg == 1 else jnp.where(row_seg == s, kk[h], 0.0)
            writes.append((s, h, st * jnp.exp2(total_col[sl[h], s:s + 1]) + _dot_tn(kk_s, v_bf[h])))
        o_inter.append(parts[0] if nseg == 1 else jnp.concatenate(parts, axis=0))
    yield

    a = [[None] * ntile for h in heads]

    def put(h, t, level, z_rows):
        prev = 0.0 if a[h][t] is None else a[h][t]
        a[h][t] = jnp.where(level_tiles[t] == level, z_rows, prev)

    for h in heads:
        z = _dot_nt(q[h], k[h])
        for t in range(ntile):
            put(h, t, 0, z[t * SUBLANES:(t + 1) * SUBLANES])

    for idx, hb in enumerate(LEVELS):
        if 2 * hb > seg_len:
            continue
        level = idx + 1
        if hb >= SUBLANES:
            starts = range(0, rows, 2 * hb)
            for h in heads:
                e_parts, x_parts = [], []
                for r0 in starts:
                    ref = _row_bcast(bc[h], r0 + hb - 1, hb)
                    e_parts += [ref - bc[h][r0:r0 + hb], bc[h][r0 + hb:r0 + 2 * hb] - ref]
                    x_parts += [k[h][r0:r0 + hb], q[h][r0 + hb:r0 + 2 * hb]]
                u = jnp.concatenate(x_parts, axis=0) * jnp.exp2(jnp.concatenate(e_parts, axis=0))
                u_upper = jnp.concatenate([u[r0 + hb:r0 + 2 * hb] for r0 in starts], axis=0)
                z = _dot_nt(u_upper, u)
                upper_tiles = [t for r0 in starts for t in range((r0 + hb) // SUBLANES, (r0 + 2 * hb) // SUBLANES)]
                for i, t in enumerate(upper_tiles):
                    put(h, t, level, z[i * SUBLANES:(i + 1) * SUBLANES])
        else:
            upper = (row_id % (2 * hb)) >= hb
            for h in heads:
                if hb == 1:
                    e_l = jnp.where(upper, g[h], 0.0)
                else:
                    tiles = range(ntile)
                    if hb == 4:
                        ref = jnp.concatenate([_row_bcast(bc[h], t * SUBLANES + 3, SUBLANES) for t in tiles], axis=0)
                    else:
                        lo = jnp.concatenate([_row_bcast(bc[h], t * SUBLANES + 1, SUBLANES) for t in tiles], axis=0)
                        hi = jnp.concatenate([_row_bcast(bc[h], t * SUBLANES + 5, SUBLANES) for t in tiles], axis=0)
                        ref = jnp.where(row_in_tile < 4, lo, hi)
                    d = bc[h] - ref
                    e_l = jnp.where(upper, d, -d)
                u = jnp.where(upper, q[h], k[h]) * jnp.exp2(e_l)
                z = _dot_nt(u, u)
                for t in range(ntile):
                    put(h, t, level, z[t * SUBLANES:(t + 1) * SUBLANES])
        yield

    for h in heads:
        o_ref[:, h * DV:(h + 1) * DV] = jnp.dot(jnp.concatenate(a[h], axis=0).astype(BF16), v_bf[h],
                                                preferred_element_type=F32) + o_inter[h]

    for s, h, s_new in writes:
        s_out[s, h] = s_new
    yield


def _gla_body(qk_p, v_p, glr_p, qk_s, v_s, glr_s, w2_ref, bg_ref, cum_p, lvl_p, segind_p, cum_s, lvl_s, segind_s,
              s_in, o_p, s_p, o_s, s_out, *, n_prompt_seq, seg_len):
    @pl.when(pl.program_id(0) == 0)
    def _():
        s_p[...] = jnp.zeros_like(s_p)

    chains = []
    for a in range(n_prompt_seq):
        state = s_p.at[pl.ds(a, 1)]
        chains.append(_gla_rows(qk_p.at[a], v_p.at[a], glr_p.at[a], w2_ref, bg_ref, cum_p, lvl_p, segind_p,
                                state, state, o_p.at[a], PROMPT_ROWS, PROMPT_ROWS))
    chains.append(_gla_rows(qk_s, v_s, glr_s, w2_ref, bg_ref, cum_s, lvl_s, segind_s,
                            s_in, s_out, o_s, SAMPLE_ROWS, seg_len))
    _emit_staggered(chains)


def _mixer_params(n_axes):
    return pltpu.CompilerParams(dimension_semantics=("arbitrary",) * n_axes, vmem_limit_bytes=MIXER_VMEM_LIMIT)


def _state_spec(n, trailing):
    nd = 1 + len(trailing)
    return pl.BlockSpec((n,) + trailing, lambda *ids: (ids[0],) + (0,) * (nd - 1))


def _mlstm_mixer(prompt, sample, bias_i, bias_f, c0, n0, m0, n_prompt_seq, prompt_len):
    qk_p, v_p, gcol_p, grow_p = prompt
    qk_s, v_s, gcol_s, grow_s = sample
    n_sample_seq = c0.shape[0]
    seg_len = qk_s.shape[0] // n_sample_seq
    bias = jnp.concatenate([bias_i, bias_f]).astype(F32)
    bias_row = jnp.zeros((1, LANES), F32).at[0, :2 * N_HEADS].set(bias)
    bias_col = bias.reshape(2 * N_HEADS, 1)

    nb, plen = n_prompt_seq, prompt_len
    steps = plen // PROMPT_ROWS
    spb = SAMPLE_ROWS // seg_len
    assert n_sample_seq == steps * spb, "one block of sample sequences per prompt chunk step"
    ccol_p, crow_p = _mlstm_constants(PROMPT_ROWS, PROMPT_ROWS)
    ccol_s, crow_s = _mlstm_constants(SAMPLE_ROWS, seg_len)
    consts = [bias_row, bias_col, ccol_p, crow_p, ccol_s, crow_s]
    m0b = jnp.broadcast_to(m0[:, :, None], m0.shape + (LANES,))
    grow_p = jnp.transpose(grow_p.reshape(2 * N_HEADS, nb, plen), (1, 0, 2))
    grow_s = jnp.transpose(grow_s.reshape(2 * N_HEADS, steps, SAMPLE_ROWS), (1, 0, 2))
    seq_rows = lambda width: pl.BlockSpec((nb, PROMPT_ROWS, width), lambda i: (0, i, 0))
    blk_rows = lambda width: pl.BlockSpec((SAMPLE_ROWS, width), lambda i: (i, 0))
    whole = lambda shape: pl.BlockSpec(shape, lambda i: (0,) * len(shape))
    state_specs = [_state_spec(spb, (N_HEADS, DQK, DV)), _state_spec(spb, (N_HEADS, DQK)),
                   _state_spec(spb, (N_HEADS, LANES))]
    h_p, c_p, n_p, m_p, h_s, c_s, n_s, m_s = pl.pallas_call(
        functools.partial(_mlstm_body, n_prompt_seq=nb, seg_len=seg_len),
        grid=(steps,),
        in_specs=[seq_rows(2 * QK_WIDTH), seq_rows(V_WIDTH), seq_rows(LANES),
                  pl.BlockSpec((nb, 2 * N_HEADS, PROMPT_ROWS), lambda i: (0, 0, i)),
                  blk_rows(2 * QK_WIDTH), blk_rows(V_WIDTH), blk_rows(LANES),
                  pl.BlockSpec((1, 2 * N_HEADS, SAMPLE_ROWS), lambda i: (i, 0, 0))]
        + [_const_spec(a.shape) for a in consts] + state_specs,
        out_specs=[seq_rows(V_WIDTH), whole((nb, N_HEADS, DQK, DV)), whole((nb, N_HEADS, DQK)),
                   whole((nb, N_HEADS, LANES)), blk_rows(V_WIDTH)] + state_specs,
        out_shape=[
            jax.ShapeDtypeStruct((nb, plen, V_WIDTH), F32),
            jax.ShapeDtypeStruct((nb, N_HEADS, DQK, DV), F32),
            jax.ShapeDtypeStruct((nb, N_HEADS, DQK), F32),
            jax.ShapeDtypeStruct((nb, N_HEADS, LANES), F32),
            jax.ShapeDtypeStruct((n_sample_seq * seg_len, V_WIDTH), F32),
            jax.ShapeDtypeStruct(c0.shape, F32),
            jax.ShapeDtypeStruct(n0.shape, F32),
            jax.ShapeDtypeStruct(m0b.shape, F32),
        ],
        compiler_params=_mixer_params(1),
        name="mlstm_mixer",
    )(qk_p.reshape(nb, plen, -1), v_p.reshape(nb, plen, -1), gcol_p.reshape(nb, plen, -1), grow_p,
      qk_s, v_s, gcol_s, grow_s, *consts, c0, n0, m0b)

    return (h_p.reshape(nb * plen, V_WIDTH), h_s), (c_p, n_p, m_p[:, :, 0]), (c_s, n_s, m_s[:, :, 0])


def _gla_mixer(prompt, sample, w_gate2, b_gate, s0, n_prompt_seq, prompt_len):
    qk_p, v_p, glr_p = prompt
    qk_s, v_s, glr_s = sample
    n_sample_seq = s0.shape[0]
    seg_len = qk_s.shape[0] // n_sample_seq
    nb, plen = n_prompt_seq, prompt_len
    steps = plen // PROMPT_ROWS
    spb = SAMPLE_ROWS // seg_len
    assert n_sample_seq == steps * spb, "one block of sample sequences per prompt chunk step"
    w2 = jnp.zeros((LANES, QK_WIDTH), BF16).at[:GATE_RANK].set(w_gate2.astype(BF16))
    bg = b_gate.reshape(1, QK_WIDTH).astype(F32)
    consts = [w2, bg, *_gla_constants(PROMPT_ROWS, PROMPT_ROWS), *_gla_constants(SAMPLE_ROWS, seg_len)]
    seq_rows = lambda width: pl.BlockSpec((nb, PROMPT_ROWS, width), lambda i: (0, i, 0))
    blk_rows = lambda width: pl.BlockSpec((SAMPLE_ROWS, width), lambda i: (i, 0))
    state_spec = _state_spec(spb, (N_HEADS, DQK, DV))
    o_p, s_p, o_s, s_s = pl.pallas_call(
        functools.partial(_gla_body, n_prompt_seq=nb, seg_len=seg_len),
        grid=(steps,),
        in_specs=[seq_rows(2 * QK_WIDTH), seq_rows(V_WIDTH), seq_rows(LANES),
                  blk_rows(2 * QK_WIDTH), blk_rows(V_WIDTH), blk_rows(LANES)]
        + [_const_spec(a.shape) for a in consts] + [state_spec],
        out_specs=[seq_rows(V_WIDTH), pl.BlockSpec((nb, N_HEADS, DQK, DV), lambda i: (0, 0, 0, 0)),
                   blk_rows(V_WIDTH), state_spec],
        out_shape=[
            jax.ShapeDtypeStruct((nb, plen, V_WIDTH), F32),
            jax.ShapeDtypeStruct((nb, N_HEADS, DQK, DV), F32),
            jax.ShapeDtypeStruct((n_sample_seq * seg_len, V_WIDTH), F32),
            jax.ShapeDtypeStruct(s0.shape, F32),
        ],
        compiler_params=_mixer_params(1),
        name="gla_mixer",
    )(qk_p.reshape(nb, plen, -1), v_p.reshape(nb, plen, -1), glr_p.reshape(nb, plen, -1),
      qk_s, v_s, glr_s, *consts, s0)

    return (o_p.reshape(nb * plen, V_WIDTH), o_s), s_p, s_s


def _pad_gate_columns(w):
    return jnp.zeros((D_MODEL, LANES), BF16).at[:, :w.shape[1]].set(w.astype(BF16))


def kernel(x_prompt, x_sample, state_mlstm_C, state_mlstm_n, state_mlstm_m, state_gla_S, g_pre_mix, g_post_mix, g_pre_ffn, g_post_ffn, w_in_mlstm, b_i_mlstm, b_f_mlstm, g_head_mlstm, w_out_mlstm, w_in_gla, w_gate2_gla, b_gate_gla, g_head_gla, w_out_gla, w_ffn_gate, w_ffn_up, w_ffn_down):
    bp, sp, d = x_prompt.shape
    bs, ss, _ = x_sample.shape
    depth = g_pre_mix.shape[0]
    assert d == D_MODEL and sp % PROMPT_ROWS == 0 and SAMPLE_ROWS % ss == 0 and (bs * ss) % SAMPLE_ROWS == 0
    assert (bp * sp) % TM == 0 and (bs * ss) % TM == 0

    xp = x_prompt.reshape(bp * sp, d)
    xs = x_sample.reshape(bs * ss, d)
    vec = lambda g: g.reshape(1, D_MODEL).astype(F32)

    prompt_states = {"C": [], "n": [], "m": [], "S": []}
    sample_states = {"C": [], "n": [], "m": [], "S": []}
    ffn_gate, ffn_up, ffn_down = (w.astype(BF16) for w in (w_ffn_gate, w_ffn_up, w_ffn_down))
    in_mlstm, in_gla = w_in_mlstm.astype(BF16), w_in_gla.astype(BF16)
    ones = jnp.ones((QK_WIDTH,), F32)
    head_scale = jnp.full((QK_WIDTH,), DQK ** -0.5, F32)
    for layer in range(depth):
        j = layer // 2
        if layer % 2 == 0:
            project = lambda x: _inproj(x, vec(g_pre_mix[layer]), in_mlstm, j,
                                        _pad_gate_columns(w_in_mlstm[j][:, MAIN_WIDTH:]),
                                        jnp.concatenate([ones, head_scale]).reshape(1, -1),
                                        silu_gate=False, gate_rows=True)
            (qk_p, v_p, gate_p, gcol_p, grow_p), (qk_s, v_s, gate_s, gcol_s, grow_s) = project(xp), project(xs)
            h, st_p, st_s = _mlstm_mixer((qk_p, v_p, gcol_p, grow_p), (qk_s, v_s, gcol_s, grow_s),
                                         b_i_mlstm[j], b_f_mlstm[j],
                                         state_mlstm_C[j], state_mlstm_n[j], state_mlstm_m[j], bp, sp)
            for dst, st in ((prompt_states, st_p), (sample_states, st_s)):
                dst["C"].append(st[0]); dst["n"].append(st[1]); dst["m"].append(st[2])
            w_out, g_head = w_out_mlstm[j], g_head_mlstm[j]
        else:
            project = lambda x: _inproj(x, vec(g_pre_mix[layer]), in_gla, j,
                                        _pad_gate_columns(w_in_gla[j][:, MAIN_WIDTH:]),
                                        jnp.concatenate([head_scale, ones]).reshape(1, -1),
                                        silu_gate=True, gate_rows=False)
            (qk_p, v_p, gate_p, glr_p), (qk_s, v_s, gate_s, glr_s) = project(xp), project(xs)
            h, s_p, s_s = _gla_mixer((qk_p, v_p, glr_p), (qk_s, v_s, glr_s), w_gate2_gla[j], b_gate_gla[j],
                                     state_gla_S[j], bp, sp)
            prompt_states["S"].append(s_p)
            sample_states["S"].append(s_s)
            w_out, g_head = w_out_gla[j], g_head_gla[j]
        xp, xs = _block_tail(h[0], h[1], gate_p, gate_s, g_head.reshape(1, V_WIDTH).astype(F32), w_out.astype(BF16),
                             xp, xs,
                             vec(g_post_mix[layer]), vec(g_pre_ffn[layer]), vec(g_post_ffn[layer]),
                             ffn_gate, ffn_up, ffn_down, layer)

    stack = lambda xs_: jnp.stack(xs_)
    return (xp.reshape(bp, sp, d), xs.reshape(bs, ss, d),
            stack(prompt_states["C"]), stack(prompt_states["n"]), stack(prompt_states["m"]), stack(prompt_states["S"]),
            stack(sample_states["C"]), stack(sample_states["n"]), stack(sample_states["m"]), stack(sample_states["S"]))
```

```python
import functools

import numpy as np
import jax
import jax.numpy as jnp
from jax import lax
from jax.experimental import pallas as pl
from jax.experimental.pallas import tpu as pltpu

F32 = jnp.float32
BF16 = jnp.bfloat16

D_MODEL = 1024
N_HEADS = 4
DQK = 128
DV = 256
QK_WIDTH = N_HEADS * DQK
V_WIDTH = N_HEADS * DV
MAIN_WIDTH = 2 * QK_WIDTH + 2 * V_WIDTH
GATE_RANK = 16
GATE_TAU = 16.0
EPS = 1e-6
LOG2_E = 1.4426950408889634
LANES = 128
SUBLANES = 8
MXU_DIM = 256
VMEM_LIMIT = 56 * 1024 * 1024
MIXER_VMEM_LIMIT = 62 * 1024 * 1024

PROMPT_ROWS = 128
SAMPLE_ROWS = 64
LEVELS = (64, 32, 16, 8, 4, 2, 1)
TM = 512
TM_IN = 1024
GROUP_ROWS = 256
FF_SPLITS = 2


def _dot(a, b):
    return jnp.dot(a.astype(BF16), b.astype(BF16), preferred_element_type=F32)


def _dot_nt(a, b):
    return lax.dot_general(a.astype(BF16), b.astype(BF16), (((1,), (1,)), ((), ())),
                           preferred_element_type=F32)


def _dot_tn(a, b):
    return lax.dot_general(a.astype(BF16), b.astype(BF16), (((0,), (0,)), ((), ())),
                           preferred_element_type=F32)


def _split_hi_lo(x):
    hi = x.astype(BF16)
    lo = (x - hi.astype(F32)).astype(BF16)
    return hi, lo


def _log_sigmoid(x):
    return jnp.minimum(x, 0.0) - jnp.log(1.0 + jnp.exp(-jnp.abs(x)))


def _sigmoid(x):
    return 1.0 / (1.0 + jnp.exp(-x))


def _rms(x, gain):
    return x * lax.rsqrt(jnp.mean(x * x, axis=-1, keepdims=True) + EPS) * gain


def _rows_per_segment(values, seg_len):
    parts = [jnp.broadcast_to(v, (seg_len, v.shape[1])) for v in values]
    return parts[0] if len(parts) == 1 else jnp.concatenate(parts, axis=0)


def _emit_staggered(chains):
    pending, live = list(chains), []
    while pending or live:
        if pending:
            live.append(pending.pop(0))
        for chain in list(live):
            if next(chain, StopIteration) is StopIteration:
                live.remove(chain)


def _const_spec(shape):
    nd = len(shape)
    return pl.BlockSpec(shape, lambda *_: (0,) * nd)


def _two_group_specs(n_prompt_blocks, width, single_buffer_sample=False):
    prompt = pl.BlockSpec((TM, width), lambda i: (jnp.minimum(i, n_prompt_blocks - 1), 0))
    mode = dict(pipeline_mode=pl.Buffered(1)) if single_buffer_sample else {}
    sample = pl.BlockSpec((TM, width), lambda i: (jnp.maximum(i - n_prompt_blocks, 0), 0), **mode)
    return prompt, sample


def _resident_spec(shape, layer=None):
    nd = len(shape)
    if layer is None:
        return pl.BlockSpec(shape, lambda i: (0,) * nd, pipeline_mode=pl.Buffered(1))
    return pl.BlockSpec((None,) + tuple(shape), lambda i: (layer,) + (0,) * nd, pipeline_mode=pl.Buffered(1))


def _inproj_body(x_ref, gain_ref, w_ref, wg_ref, scale_ref, oqk_ref, ov_ref, ogate_ref, og_ref,
                 *maybe_grow_ref, silu_gate):
    n_groups = x_ref.shape[0] // GROUP_ROWS
    groups = [slice(a * GROUP_ROWS, (a + 1) * GROUP_ROWS) for a in range(n_groups)]
    h = [None] * n_groups
    for a in range(n_groups + 1):
        if a < n_groups:
            h[a] = _rms(x_ref[groups[a], :], gain_ref[...]).astype(BF16)
        if a > 0:
            rows, hb = groups[a - 1], h[a - 1]
            narrow = jnp.dot(hb, wg_ref[...], preferred_element_type=F32)
            og_ref[rows, :] = narrow
            for grow_ref in maybe_grow_ref:
                grow_ref[:, rows] = narrow.T[0:SUBLANES, :]
            qk = jnp.dot(hb, w_ref[:, 0:2 * QK_WIDTH], preferred_element_type=F32)
            oqk_ref[rows, :] = qk * scale_ref[...]
            v = jnp.dot(hb, w_ref[:, 2 * QK_WIDTH:2 * QK_WIDTH + V_WIDTH], preferred_element_type=F32)
            ov_ref[rows, :] = v.astype(BF16)
            gate = jnp.dot(hb, w_ref[:, 2 * QK_WIDTH + V_WIDTH:MAIN_WIDTH], preferred_element_type=F32)
            ogate_ref[rows, :] = gate * _sigmoid(gate) if silu_gate else _sigmoid(gate)


def _inproj(x, gain, w_in, layer, w_gate, qk_scale, silu_gate, gate_rows):
    m = x.shape[0]
    tm = TM_IN if m >= 4 * TM_IN else GROUP_ROWS
    assert m % tm == 0
    rows_spec = lambda width: pl.BlockSpec((tm, width), lambda i: (i, 0))
    out_specs = [rows_spec(2 * QK_WIDTH), rows_spec(V_WIDTH), rows_spec(V_WIDTH), rows_spec(LANES)]
    out_shape = [
        jax.ShapeDtypeStruct((m, 2 * QK_WIDTH), F32),
        jax.ShapeDtypeStruct((m, V_WIDTH), BF16),
        jax.ShapeDtypeStruct((m, V_WIDTH), F32),
        jax.ShapeDtypeStruct((m, LANES), F32),
    ]
    if gate_rows:
        out_specs.append(pl.BlockSpec((SUBLANES, tm), lambda i: (0, i)))
        out_shape.append(jax.ShapeDtypeStruct((SUBLANES, m), F32))
    return pl.pallas_call(
        functools.partial(_inproj_body, silu_gate=silu_gate),
        grid=(m // tm,),
        in_specs=[rows_spec(D_MODEL), _resident_spec((1, D_MODEL)), _resident_spec(w_in.shape[1:], layer),
                  _resident_spec((D_MODEL, LANES)), _resident_spec((1, 2 * QK_WIDTH))],
        out_specs=out_specs,
        out_shape=out_shape,
        compiler_params=pltpu.CompilerParams(dimension_semantics=("arbitrary",), vmem_limit_bytes=VMEM_LIMIT),
        name="inproj",
    )(x, gain, w_in, w_gate, qk_scale)


def _tail_body(hp_ref, hs_ref, gp_ref, gs_ref, ghead_ref, wo_ref, xp_ref, xs_ref, gpm_ref, gpf_ref, gqf_ref,
               wg_ref, wu_ref, wd_ref, op_ref, os_ref, *, n_prompt_blocks):
    is_prompt = pl.program_id(0) < n_prompt_blocks
    d_ff = wg_ref.shape[1]
    n_tiles = d_ff // MXU_DIM
    bounds = [MXU_DIM * ((n_tiles * t + FF_SPLITS - 1) // FF_SPLITS) for t in range(FF_SPLITS + 1)]

    def stages(raw_ref, gate_ref, x_ref, out_ref, rows):
        raw = raw_ref[rows, :]
        gain = ghead_ref[...]
        y = jnp.concatenate(
            [gate_ref[rows, h * DV:(h + 1) * DV] * _rms(raw[:, h * DV:(h + 1) * DV], gain[:, h * DV:(h + 1) * DV])
             for h in range(N_HEADS)], axis=1).astype(BF16)
        yield
        mix = jnp.dot(y, wo_ref[...], preferred_element_type=F32)
        yield
        x1 = x_ref[rows, :] + _rms(mix, gpm_ref[...])
        h = _rms(x1, gpf_ref[...]).astype(BF16)
        yield
        ffn = None
        for t in range(FF_SPLITS):
            cols = slice(bounds[t], bounds[t + 1])
            gate = jnp.dot(h, wg_ref[:, cols], preferred_element_type=F32)
            up = jnp.dot(h, wu_ref[:, cols], preferred_element_type=F32)
            yield
            act = (gate * _sigmoid(gate) * up).astype(BF16)
            yield
            part = jnp.dot(act, wd_ref[cols, :], preferred_element_type=F32)
            ffn = part if ffn is None else ffn + part
            yield
        out_ref[rows, :] = x1 + _rms(ffn, gqf_ref[...])
        yield

    def block(raw_ref, gate_ref, x_ref, out_ref):
        _emit_staggered([stages(raw_ref, gate_ref, x_ref, out_ref, slice(a * GROUP_ROWS, (a + 1) * GROUP_ROWS))
                         for a in range(TM // GROUP_ROWS)])

    pl.when(is_prompt)(lambda: block(hp_ref, gp_ref, xp_ref, op_ref))
    pl.when(jnp.logical_not(is_prompt))(lambda: block(hs_ref, gs_ref, xs_ref, os_ref))


def _block_tail(hp, hs, gate_p, gate_s, ghead, w_out, xp, xs, g_post_mix, g_pre_ffn, g_post_ffn,
                w_gate, w_up, w_down, layer):
    npb = xp.shape[0] // TM
    m = xp.shape[0] + xs.shape[0]
    d_ff = w_gate.shape[2]
    assert d_ff % MXU_DIM == 0
    in_p, in_s = _two_group_specs(npb, D_MODEL, single_buffer_sample=True)
    vec = _resident_spec((1, D_MODEL))
    return pl.pallas_call(
        functools.partial(_tail_body, n_prompt_blocks=npb),
        grid=(m // TM,),
        in_specs=[
            in_p, in_s,
            in_p, in_s,
            _resident_spec((1, V_WIDTH)),
            _resident_spec((V_WIDTH, D_MODEL)),
            in_p, in_s,
            vec, vec, vec,
            _resident_spec((D_MODEL, d_ff), layer), _resident_spec((D_MODEL, d_ff), layer),
            _resident_spec((d_ff, D_MODEL), layer),
        ],
        out_specs=list(_two_group_specs(npb, D_MODEL)),
        out_shape=[jax.ShapeDtypeStruct(xp.shape, F32), jax.ShapeDtypeStruct(xs.shape, F32)],
        compiler_params=pltpu.CompilerParams(dimension_semantics=("arbitrary",), vmem_limit_bytes=VMEM_LIMIT),
        name="block_tail",
    )(hp, hs, gate_p, gate_s, ghead, w_out, xp, xs, g_post_mix, g_pre_ffn, g_post_ffn, w_gate, w_up, w_down)


def _segment_structure(rows, seg_len):
    t = np.arange(rows)
    seg = t // seg_len
    same = seg[:, None] == seg[None, :]
    cum = same & (t[None, :] <= t[:, None])
    return t, seg, same, cum


def _mlstm_constants(rows, seg_len):
    _, _, same, cum = _segment_structure(rows, seg_len)
    col = np.concatenate([cum, same], axis=0).astype(np.float32)
    row = cum.T.astype(np.float32)
    return jnp.asarray(col, BF16), jnp.asarray(row, BF16)


def _mlstm_rows(qk_ref, v_ref, gcol_ref, grow_ref, bias_row_ref, bias_col_ref, ccol_ref, crow_ref,
                c_in, n_in, m_in, c_out, n_out, m_out, h_ref, rows, seg_len):
    nseg = rows // seg_len
    r = lax.broadcasted_iota(jnp.int32, (rows, rows), 0)
    c = lax.broadcasted_iota(jnp.int32, (rows, rows), 1)
    same = None if nseg == 1 else (r // seg_len) == (c // seg_len)
    lower = (c <= r) if nseg == 1 else same & (c <= r)
    row_seg = lax.broadcasted_iota(jnp.int32, (rows, DQK), 0) // seg_len

    pre_col = gcol_ref[...] + bias_row_ref[...]
    grow = grow_ref[0] if len(grow_ref.shape) == 3 else grow_ref[...]
    pre_row = grow + bias_col_ref[...]
    ccol = ccol_ref[...]
    crow = crow_ref[...]
    hi, lo = _split_hi_lo(_log_sigmoid(pre_col))
    sums_col = jnp.dot(ccol, hi, preferred_element_type=F32) + jnp.dot(ccol, lo, preferred_element_type=F32)
    hi, lo = _split_hi_lo(_log_sigmoid(pre_row))
    sums_row = jnp.dot(hi, crow, preferred_element_type=F32) + jnp.dot(lo, crow, preferred_element_type=F32)
    yield

    qk = qk_ref[...]
    c_prev = [[c_in[s, h] for h in range(N_HEADS)] for s in range(nseg)]
    n_state = [[n_in[s, h:h + 1, :] for h in range(N_HEADS)] for s in range(nseg)]
    m_state = [[m_in[s, h:h + 1, 0:1] for h in range(N_HEADS)] for s in range(nseg)]
    writes = []

    heads = range(N_HEADS)
    q = [qk[:, h * DQK:(h + 1) * DQK] for h in heads]
    k = [qk[:, QK_WIDTH + h * DQK:QK_WIDTH + (h + 1) * DQK] for h in heads]
    v_bf = [v_ref[:, h * DV:(h + 1) * DV] for h in heads]
    i_col = [pre_col[:, h:h + 1] for h in heads]
    b_col = [sums_col[0:rows, N_HEADS + h:N_HEADS + h + 1] for h in heads]
    b_last = [sums_col[rows:2 * rows, N_HEADS + h:N_HEADS + h + 1] for h in heads]
    src = [pre_row[h:h + 1, :] - sums_row[N_HEADS + h:N_HEADS + h + 1, :] for h in heads]
    m_prev = [_rows_per_segment([m_state[s][h] for s in range(nseg)], seg_len) for h in heads]
    n_prev = [_rows_per_segment([n_state[s][h] for s in range(nseg)], seg_len) for h in heads]

    dmat = [jnp.where(lower, b_col[h] + src[h], -jnp.inf) for h in heads]
    inter = [b_col[h] + m_prev[h] for h in heads]
    m_t = [jnp.maximum(inter[h], jnp.max(dmat[h], axis=1, keepdims=True)) for h in heads]
    if nseg == 1:
        m_new = [jnp.broadcast_to(m_t[h][rows - 1:rows, :], (rows, 1)) for h in heads]
    else:
        dlast = [jnp.where(same, b_last[h] + src[h], -jnp.inf) for h in heads]
        m_new = [jnp.maximum(b_last[h] + m_prev[h], jnp.max(dlast[h], axis=1, keepdims=True)) for h in heads]
    yield
    scores = [_dot_nt(q[h], k[h]) for h in heads]
    q_bf = [q[h].astype(BF16) for h in heads]
    qc = []
    for h in heads:
        parts = []
        for s in range(nseg):
            full = jnp.dot(q_bf[h], c_prev[s][h].astype(BF16), preferred_element_type=F32)
            parts.append(full[s * seg_len:(s + 1) * seg_len] if nseg > 1 else full)
        qc.append(parts[0] if nseg == 1 else jnp.concatenate(parts, axis=0))
    qn_terms = [q[h] * n_prev[h] for h in heads]
    yield

    w_inter = [jnp.exp(inter[h] - m_t[h]) for h in heads]
    a = [scores[h] * jnp.exp(dmat[h] - m_t[h]) for h in heads]
    w_s = [jnp.exp(b_last[h] - b_col[h] + i_col[h] - m_new[h]) for h in heads]
    decay = [jnp.exp(b_last[h] + m_prev[h] - m_new[h]) for h in heads]
    kw = [k[h] * w_s[h] for h in heads]
    yield

    num = [jnp.dot(a[h].astype(BF16), v_bf[h], preferred_element_type=F32) + w_inter[h] * qc[h] for h in heads]
    if rows == DQK:
        den = [jnp.sum(a[h] + w_inter[h] * qn_terms[h], axis=1, keepdims=True) for h in heads]
    else:
        den = [jnp.sum(a[h], axis=1, keepdims=True) + w_inter[h] * jnp.sum(qn_terms[h], axis=1, keepdims=True)
               for h in heads]
    for h in heads:
        for s in range(nseg):
            kw_s = kw[h] if nseg == 1 else jnp.where(row_seg == s, kw[h], 0.0)
            dec = decay[h][s * seg_len:s * seg_len + 1, :]
            writes.append((s, h, dec * c_prev[s][h] + _dot_tn(kw_s, v_bf[h]),
                           dec * n_state[s][h] + jnp.sum(kw_s, axis=0, keepdims=True),
                           jnp.broadcast_to(m_new[h][s * seg_len:s * seg_len + 1, :], (1, LANES))))

    yield
    for h in heads:
        h_ref[:, h * DV:(h + 1) * DV] = num[h] * (1.0 / jnp.maximum(jnp.abs(den[h]), jnp.exp(-m_t[h])))

    for s, h, c_new, n_new, m_new in writes:
        c_out[s, h] = c_new
        n_out[s, h:h + 1, :] = n_new
        m_out[s, h:h + 1, :] = m_new
    yield


def _mlstm_body(qk_p, v_p, gcol_p, grow_p, qk_s, v_s, gcol_s, grow_s, brow_ref, bcol_ref,
                ccol_p, crow_p, ccol_s, crow_s, c_in, n_in, m_in,
                h_p, c_p, n_p, m_p, h_s, c_out, n_out, m_out, *, n_prompt_seq, seg_len):
    @pl.when(pl.program_id(0) == 0)
    def _():
        c_p[...] = jnp.zeros_like(c_p)
        n_p[...] = jnp.zeros_like(n_p)
        m_p[...] = jnp.zeros_like(m_p)

    chains = []
    for a in range(n_prompt_seq):
        one = pl.ds(a, 1)
        c, n, m = c_p.at[one], n_p.at[one], m_p.at[one]
        chains.append(_mlstm_rows(qk_p.at[a], v_p.at[a], gcol_p.at[a], grow_p.at[a], brow_ref, bcol_ref, ccol_p, crow_p,
                                  c, n, m, c, n, m, h_p.at[a], PROMPT_ROWS, PROMPT_ROWS))
    chains.append(_mlstm_rows(qk_s, v_s, gcol_s, grow_s, brow_ref, bcol_ref, ccol_s, crow_s,
                              c_in, n_in, m_in, c_out, n_out, m_out, h_s, SAMPLE_ROWS, seg_len))
    _emit_staggered(chains)


def _gla_constants(rows, seg_len):
    t, seg, _, cum = _segment_structure(rows, seg_len)
    level_id = np.full((rows, rows), -1, np.int32)
    level_id[t, t] = 0
    for idx, hb in enumerate(LEVELS):
        if 2 * hb > seg_len:
            continue
        block = t // (2 * hb)
        upper = (t % (2 * hb)) >= hb
        level_id[(block[:, None] == block[None, :]) & upper[:, None] & ~upper[None, :]] = idx + 1
    segind = (seg[:, None] == np.arange(LANES)[None, :]).astype(np.float32)
    return jnp.asarray(cum.astype(np.float32), BF16), jnp.asarray(level_id), jnp.asarray(segind, BF16)


def _row_bcast(x, row, n):
    return jnp.broadcast_to(x[row:row + 1, :], (n, x.shape[1]))


def _gla_rows(qk_ref, v_ref, glr_ref, w2_ref, bg_ref, cum_ref, lvl_ref, segind_ref,
              s_in, s_out, o_ref, rows, seg_len):
    nseg = rows // seg_len
    ntile = rows // SUBLANES
    row_id = lax.broadcasted_iota(jnp.int32, (rows, DQK), 0)
    row_seg = row_id // seg_len
    row_in_tile = row_id % SUBLANES

    pre = _dot(glr_ref[...], w2_ref[...]) + bg_ref[...]
    g_all = _log_sigmoid(pre) * (LOG2_E / GATE_TAU)
    g_hi, g_lo = _split_hi_lo(g_all)
    cum = cum_ref[...]
    bc_all = jnp.dot(cum, g_hi, preferred_element_type=F32) + jnp.dot(cum, g_lo, preferred_element_type=F32)
    level_tiles = [lvl_ref[t * SUBLANES:(t + 1) * SUBLANES, :] for t in range(ntile)]
    segind = segind_ref[...]
    total_col = (lax.dot_general(g_hi, segind, (((0,), (0,)), ((), ())), preferred_element_type=F32)
                 + lax.dot_general(g_lo, segind, (((0,), (0,)), ((), ())), preferred_element_type=F32))
    yield

    qk = qk_ref[...]
    s_prev = [[s_in[s, h] for h in range(N_HEADS)] for s in range(nseg)]
    writes = []

    heads = range(N_HEADS)
    sl = [slice(h * DQK, (h + 1) * DQK) for h in heads]
    q = [qk[:, sl[h]] for h in heads]
    k = [qk[:, QK_WIDTH + h * DQK:QK_WIDTH + (h + 1) * DQK] for h in heads]
    v_bf = [v_ref[:, h * DV:(h + 1) * DV] for h in heads]
    g = [g_all[:, sl[h]] for h in heads]
    bc = [bc_all[:, sl[h]] for h in heads]
    seg_total = [jnp.concatenate([_row_bcast(bc[h], (s + 1) * seg_len - 1, seg_len) for s in range(nseg)], axis=0)
                 if nseg > 1 else _row_bcast(bc[h], rows - 1, rows) for h in heads]

    qq = [(q[h] * jnp.exp2(bc[h])).astype(BF16) for h in heads]
    kk = [k[h] * jnp.exp2(seg_total[h] - bc[h]) for h in heads]
    o_inter = []
    for h in heads:
        parts = []
        for s in range(nseg):
            st = s_prev[s][h]
            oi = jnp.dot(qq[h], st.astype(BF16), preferred_element_type=F32)
            parts.append(oi[s * seg_len:(s + 1) * seg_len] if nseg > 1 else oi)
            kk_s = kk[h] if nseg == 1 else jnp.where(row_seg == s, kk[h], 0.0)
            writes.append((s, h, st * jnp.exp2(total_col[sl[h], s:s + 1]) + _dot_tn(kk_s, v_bf[h])))
        o_inter.append(parts[0] if nseg == 1 else jnp.concatenate(parts, axis=0))
    yield

    a = [[None] * ntile for h in heads]

    def put(h, t, level, z_rows):
        prev = 0.0 if a[h][t] is None else a[h][t]
        a[h][t] = jnp.where(level_tiles[t] == level, z_rows, prev)

    for h in heads:
        z = _dot_nt(q[h], k[h])
        for t in range(ntile):
            put(h, t, 0, z[t * SUBLANES:(t + 1) * SUBLANES])

    for idx, hb in enumerate(LEVELS):
        if 2 * hb > seg_len:
            continue
        level = idx + 1
        if hb >= SUBLANES:
            starts = range(0, rows, 2 * hb)
            for h in heads:
                e_parts, x_parts = [], []
                for r0 in starts:
                    ref = _row_bcast(bc[h], r0 + hb - 1, hb)
                    e_parts += [ref - bc[h][r0:r0 + hb], bc[h][r0 + hb:r0 + 2 * hb] - ref]
                    x_parts += [k[h][r0:r0 + hb], q[h][r0 + hb:r0 + 2 * hb]]
                u = jnp.concatenate(x_parts, axis=0) * jnp.exp2(jnp.concatenate(e_parts, axis=0))
                u_upper = jnp.concatenate([u[r0 + hb:r0 + 2 * hb] for r0 in starts], axis=0)
                z = _dot_nt(u_upper, u)
                upper_tiles = [t for r0 in starts for t in range((r0 + hb) // SUBLANES, (r0 + 2 * hb) // SUBLANES)]
                for i, t in enumerate(upper_tiles):
                    put(h, t, level, z[i * SUBLANES:(i + 1) * SUBLANES])
        else:
            upper = (row_id % (2 * hb)) >= hb
            for h in heads:
                if hb == 1:
                    e_l = jnp.where(upper, g[h], 0.0)
                else:
                    tiles = range(ntile)
                    if hb == 4:
                        ref = jnp.concatenate([_row_bcast(bc[h], t * SUBLANES + 3, SUBLANES) for t in tiles], axis=0)
                    else:
                        lo = jnp.concatenate([_row_bcast(bc[h], t * SUBLANES + 1, SUBLANES) for t in tiles], axis=0)
                        hi = jnp.concatenate([_row_bcast(bc[h], t * SUBLANES + 5, SUBLANES) for t in tiles], axis=0)
                        ref = jnp.where(row_in_tile < 4, lo, hi)
                    d = bc[h] - ref
                    e_l = jnp.where(upper, d, -d)
                u = jnp.where(upper, q[h], k[h]) * jnp.exp2(e_l)
                z = _dot_nt(u, u)
                for t in range(ntile):
                    put(h, t, level, z[t * SUBLANES:(t + 1) * SUBLANES])
        yield

    for h in heads:
        o_ref[:, h * DV:(h + 1) * DV] = jnp.dot(jnp.concatenate(a[h], axis=0).astype(BF16), v_bf[h],
                                                preferred_element_type=F32) + o_inter[h]

    for s, h, s_new in writes:
        s_out[s, h] = s_new
    yield


def _gla_body(qk_p, v_p, glr_p, qk_s, v_s, glr_s, w2_ref, bg_ref, cum_p, lvl_p, segind_p, cum_s, lvl_s, segind_s,
              s_in, o_p, s_p, o_s, s_out, *, n_prompt_seq, seg_len):
    @pl.when(pl.program_id(0) == 0)
    def _():
        s_p[...] = jnp.zeros_like(s_p)

    chains = []
    for a in range(n_prompt_seq):
        state = s_p.at[pl.ds(a, 1)]
        chains.append(_gla_rows(qk_p.at[a], v_p.at[a], glr_p.at[a], w2_ref, bg_ref, cum_p, lvl_p, segind_p,
                                state, state, o_p.at[a], PROMPT_ROWS, PROMPT_ROWS))
    chains.append(_gla_rows(qk_s, v_s, glr_s, w2_ref, bg_ref, cum_s, lvl_s, segind_s,
                            s_in, s_out, o_s, SAMPLE_ROWS, seg_len))
    _emit_staggered(chains)


def _mixer_params(n_axes):
    return pltpu.CompilerParams(dimension_semantics=("arbitrary",) * n_axes, vmem_limit_bytes=MIXER_VMEM_LIMIT)


def _state_spec(n, trailing):
    nd = 1 + len(trailing)
    return pl.BlockSpec((n,) + trailing, lambda *ids: (ids[0],) + (0,) * (nd - 1))


def _mlstm_mixer(prompt, sample, bias_i, bias_f, c0, n0, m0, n_prompt_seq, prompt_len):
    qk_p, v_p, gcol_p, grow_p = prompt
    qk_s, v_s, gcol_s, grow_s = sample
    n_sample_seq = c0.shape[0]
    seg_len = qk_s.shape[0] // n_sample_seq
    bias = jnp.concatenate([bias_i, bias_f]).astype(F32)
    bias_row = jnp.zeros((1, LANES), F32).at[0, :2 * N_HEADS].set(bias)
    bias_col = bias.reshape(2 * N_HEADS, 1)

    nb, plen = n_prompt_seq, prompt_len
    steps = plen // PROMPT_ROWS
    spb = SAMPLE_ROWS // seg_len
    assert n_sample_seq == steps * spb, "one block of sample sequences per prompt chunk step"
    ccol_p, crow_p = _mlstm_constants(PROMPT_ROWS, PROMPT_ROWS)
    ccol_s, crow_s = _mlstm_constants(SAMPLE_ROWS, seg_len)
    consts = [bias_row, bias_col, ccol_p, crow_p, ccol_s, crow_s]
    m0b = jnp.broadcast_to(m0[:, :, None], m0.shape + (LANES,))
    grow_p = jnp.transpose(grow_p.reshape(2 * N_HEADS, nb, plen), (1, 0, 2))
    grow_s = jnp.transpose(grow_s.reshape(2 * N_HEADS, steps, SAMPLE_ROWS), (1, 0, 2))
    seq_rows = lambda width: pl.BlockSpec((nb, PROMPT_ROWS, width), lambda i: (0, i, 0))
    blk_rows = lambda width: pl.BlockSpec((SAMPLE_ROWS, width), lambda i: (i, 0))
    whole = lambda shape: pl.BlockSpec(shape, lambda i: (0,) * len(shape))
    state_specs = [_state_spec(spb, (N_HEADS, DQK, DV)), _state_spec(spb, (N_HEADS, DQK)),
                   _state_spec(spb, (N_HEADS, LANES))]
    h_p, c_p, n_p, m_p, h_s, c_s, n_s, m_s = pl.pallas_call(
        functools.partial(_mlstm_body, n_prompt_seq=nb, seg_len=seg_len),
        grid=(steps,),
        in_specs=[seq_rows(2 * QK_WIDTH), seq_rows(V_WIDTH), seq_rows(LANES),
                  pl.BlockSpec((nb, 2 * N_HEADS, PROMPT_ROWS), lambda i: (0, 0, i)),
                  blk_rows(2 * QK_WIDTH), blk_rows(V_WIDTH), blk_rows(LANES),
                  pl.BlockSpec((1, 2 * N_HEADS, SAMPLE_ROWS), lambda i: (i, 0, 0))]
        + [_const_spec(a.shape) for a in consts] + state_specs,
        out_specs=[seq_rows(V_WIDTH), whole((nb, N_HEADS, DQK, DV)), whole((nb, N_HEADS, DQK)),
                   whole((nb, N_HEADS, LANES)), blk_rows(V_WIDTH)] + state_specs,
        out_shape=[
            jax.ShapeDtypeStruct((nb, plen, V_WIDTH), F32),
            jax.ShapeDtypeStruct((nb, N_HEADS, DQK, DV), F32),
            jax.ShapeDtypeStruct((nb, N_HEADS, DQK), F32),
            jax.ShapeDtypeStruct((nb, N_HEADS, LANES), F32),
            jax.ShapeDtypeStruct((n_sample_seq * seg_len, V_WIDTH), F32),
            jax.ShapeDtypeStruct(c0.shape, F32),
            jax.ShapeDtypeStruct(n0.shape, F32),
            jax.ShapeDtypeStruct(m0b.shape, F32),
        ],
        compiler_params=_mixer_params(1),
        name="mlstm_mixer",
    )(qk_p.reshape(nb, plen, -1), v_p.reshape(nb, plen, -1), gcol_p.reshape(nb, plen, -1), grow_p,
      qk_s, v_s, gcol_s, grow_s, *consts, c0, n0, m0b)

    return (h_p.reshape(nb * plen, V_WIDTH), h_s), (c_p, n_p, m_p[:, :, 0]), (c_s, n_s, m_s[:, :, 0])


def _gla_mixer(prompt, sample, w_gate2, b_gate, s0, n_prompt_seq, prompt_len):
    qk_p, v_p, glr_p = prompt
    qk_s, v_s, glr_s = sample
    n_sample_seq = s0.shape[0]
    seg_len = qk_s.shape[0] // n_sample_seq
    nb, plen = n_prompt_seq, prompt_len
    steps = plen // PROMPT_ROWS
    spb = SAMPLE_ROWS // seg_len
    assert n_sample_seq == steps * spb, "one block of sample sequences per prompt chunk step"
    w2 = jnp.zeros((LANES, QK_WIDTH), BF16).at[:GATE_RANK].set(w_gate2.astype(BF16))
    bg = b_gate.reshape(1, QK_WIDTH).astype(F32)
    consts = [w2, bg, *_gla_constants(PROMPT_ROWS, PROMPT_ROWS), *_gla_constants(SAMPLE_ROWS, seg_len)]
    seq_rows = lambda width: pl.BlockSpec((nb, PROMPT_ROWS, width), lambda i: (0, i, 0))
    blk_rows = lambda width: pl.BlockSpec((SAMPLE_ROWS, width), lambda i: (i, 0))
    state_spec = _state_spec(spb, (N_HEADS, DQK, DV))
    o_p, s_p, o_s, s_s = pl.pallas_call(
        functools.partial(_gla_body, n_prompt_seq=nb, seg_len=seg_len),
        grid=(steps,),
        in_specs=[seq_rows(2 * QK_WIDTH), seq_rows(V_WIDTH), seq_rows(LANES),
                  blk_rows(2 * QK_WIDTH), blk_rows(V_WIDTH), blk_rows(LANES)]
        + [_const_spec(a.shape) for a in consts] + [state_spec],
        out_specs=[seq_rows(V_WIDTH), pl.BlockSpec((nb, N_HEADS, DQK, DV), lambda i: (0, 0, 0, 0)),
                   blk_rows(V_WIDTH), state_spec],
        out_shape=[
            jax.ShapeDtypeStruct((nb, plen, V_WIDTH), F32),
            jax.ShapeDtypeStruct((nb, N_HEADS, DQK, DV), F32),
            jax.ShapeDtypeStruct((n_sample_seq * seg_len, V_WIDTH), F32),
            jax.ShapeDtypeStruct(s0.shape, F32),
        ],
        compiler_params=_mixer_params(1),
        name="gla_mixer",
    )(qk_p.reshape(nb, plen, -1), v_p.reshape(nb, plen, -1), glr_p.reshape(nb, plen, -1),
      qk_s, v_s, glr_s, *consts, s0)

    return (o_p.reshape(nb * plen, V_WIDTH), o_s), s_p, s_s


def _pad_gate_columns(w):
    return jnp.zeros((D_MODEL, LANES), BF16).at[:, :w.shape[1]].set(w.astype(BF16))


def kernel(x_prompt, x_sample, state_mlstm_C, state_mlstm_n, state_mlstm_m, state_gla_S, g_pre_mix, g_post_mix, g_pre_ffn, g_post_ffn, w_in_mlstm, b_i_mlstm, b_f_mlstm, g_head_mlstm, w_out_mlstm, w_in_gla, w_gate2_gla, b_gate_gla, g_head_gla, w_out_gla, w_ffn_gate, w_ffn_up, w_ffn_down):
    bp, sp, d = x_prompt.shape
    bs, ss, _ = x_sample.shape
    depth = g_pre_mix.shape[0]
    assert d == D_MODEL and sp % PROMPT_ROWS == 0 and SAMPLE_ROWS % ss == 0 and (bs * ss) % SAMPLE_ROWS == 0
    assert (bp * sp) % TM == 0 and (bs * ss) % TM == 0

    xp = x_prompt.reshape(bp * sp, d)
    xs = x_sample.reshape(bs * ss, d)
    vec = lambda g: g.reshape(1, D_MODEL).astype(F32)

    prompt_states = {"C": [], "n": [], "m": [], "S": []}
    sample_states = {"C": [], "n": [], "m": [], "S": []}
    ffn_gate, ffn_up, ffn_down = (w.astype(BF16) for w in (w_ffn_gate, w_ffn_up, w_ffn_down))
    in_mlstm, in_gla = w_in_mlstm.astype(BF16), w_in_gla.astype(BF16)
    ones = jnp.ones((QK_WIDTH,), F32)
    head_scale = jnp.full((QK_WIDTH,), DQK ** -0.5, F32)
    for layer in range(depth):
        j = layer // 2
        if layer % 2 == 0:
            project = lambda x: _inproj(x, vec(g_pre_mix[layer]), in_mlstm, j,
                                        _pad_gate_columns(w_in_mlstm[j][:, MAIN_WIDTH:]),
                                        jnp.concatenate([ones, head_scale]).reshape(1, -1),
                                        silu_gate=False, gate_rows=True)
            (qk_p, v_p, gate_p, gcol_p, grow_p), (qk_s, v_s, gate_s, gcol_s, grow_s) = project(xp), project(xs)
            h, st_p, st_s = _mlstm_mixer((qk_p, v_p, gcol_p, grow_p), (qk_s, v_s, gcol_s, grow_s),
                                         b_i_mlstm[j], b_f_mlstm[j],
                                         state_mlstm_C[j], state_mlstm_n[j], state_mlstm_m[j], bp, sp)
            for dst, st in ((prompt_states, st_p), (sample_states, st_s)):
                dst["C"].append(st[0]); dst["n"].append(st[1]); dst["m"].append(st[2])
            w_out, g_head = w_out_mlstm[j], g_head_mlstm[j]
        else:
            project = lambda x: _inproj(x, vec(g_pre_mix[layer]), in_gla, j,
                                        _pad_gate_columns(w_in_gla[j][:, MAIN_WIDTH:]),
                                        jnp.concatenate([head_scale, ones]).reshape(1, -1),
                                        silu_gate=True, gate_rows=False)
            (qk_p, v_p, gate_p, glr_p), (qk_s, v_s, gate_s, glr_s) = project(xp), project(xs)
            h, s_p, s_s = _gla_mixer((qk_p, v_p, glr_p), (qk_s, v_s, glr_s), w_gate2_gla[j], b_gate_gla[j],
                                     state_gla_S[j], bp, sp)
            prompt_states["S"].append(s_p)
            sample_states["S"].append(s_s)
            w_out, g_head = w_out_gla[j], g_head_gla[j]
        xp, xs = _block_tail(h[0], h[1], gate_p, gate_s, g_head.reshape(1, V_WIDTH).astype(F32), w_out.astype(BF16),
                             xp, xs,
                             vec(g_post_mix[layer]), vec(g_pre_ffn[layer]), vec(g_post_ffn[layer]),
                             ffn_gate, ffn_up, ffn_down, layer)

    stack = lambda xs_: jnp.stack(xs_)
    return (xp.reshape(bp, sp, d), xs.reshape(bs, ss, d),
            stack(prompt_states["C"]), stack(prompt_states["n"]), stack(prompt_states["m"]), stack(prompt_states["S"]),
            stack(sample_states["C"]), stack(sample_states["n"]), stack(sample_states["m"]), stack(sample_states["S"]))
```

```python
import functools

import numpy as np
import jax
import jax.numpy as jnp
from jax import lax
from jax.experimental import pallas as pl
from jax.experimental.pallas import tpu as pltpu

F32 = jnp.float32
BF16 = jnp.bfloat16

D_MODEL = 1024
N_HEADS = 4
DQK = 128
DV = 256
QK_WIDTH = N_HEADS * DQK
V_WIDTH = N_HEADS * DV
MAIN_WIDTH = 2 * QK_WIDTH + 2 * V_WIDTH
GATE_RANK = 16
GATE_TAU = 16.0
EPS = 1e-6
LOG2_E = 1.4426950408889634
LANES = 128
SUBLANES = 8
MXU_DIM = 256
VMEM_LIMIT = 56 * 1024 * 1024
MIXER_VMEM_LIMIT = 62 * 1024 * 1024

PROMPT_ROWS = 128
SAMPLE_ROWS = 64
LEVELS = (64, 32, 16, 8, 4, 2, 1)
TM = 512
TM_IN = 1024
GROUP_ROWS = 256
FF_SPLITS = 4


def _dot(a, b):
    return jnp.dot(a.astype(BF16), b.astype(BF16), preferred_element_type=F32)


def _dot_nt(a, b):
    return lax.dot_general(a.astype(BF16), b.astype(BF16), (((1,), (1,)), ((), ())),
                           preferred_element_type=F32)


def _dot_tn(a, b):
    return lax.dot_general(a.astype(BF16), b.astype(BF16), (((0,), (0,)), ((), ())),
                           preferred_element_type=F32)


def _split_hi_lo(x):
    hi = x.astype(BF16)
    lo = (x - hi.astype(F32)).astype(BF16)
    return hi, lo


def _log_sigmoid(x):
    return jnp.minimum(x, 0.0) - jnp.log(1.0 + jnp.exp(-jnp.abs(x)))


def _sigmoid(x):
    return 1.0 / (1.0 + jnp.exp(-x))


def _rms(x, gain):
    return x * lax.rsqrt(jnp.mean(x * x, axis=-1, keepdims=True) + EPS) * gain


def _rows_per_segment(values, seg_len):
    parts = [jnp.broadcast_to(v, (seg_len, v.shape[1])) for v in values]
    return parts[0] if len(parts) == 1 else jnp.concatenate(parts, axis=0)


def _emit_staggered(chains):
    pending, live = list(chains), []
    while pending or live:
        if pending:
            live.append(pending.pop(0))
        for chain in list(live):
            if next(chain, StopIteration) is StopIteration:
                live.remove(chain)


def _const_spec(shape):
    nd = len(shape)
    return pl.BlockSpec(shape, lambda *_: (0,) * nd)


def _two_group_specs(n_prompt_blocks, width, single_buffer_sample=False):
    prompt = pl.BlockSpec((TM, width), lambda i: (jnp.minimum(i, n_prompt_blocks - 1), 0))
    mode = dict(pipeline_mode=pl.Buffered(1)) if single_buffer_sample else {}
    sample = pl.BlockSpec((TM, width), lambda i: (jnp.maximum(i - n_prompt_blocks, 0), 0), **mode)
    return prompt, sample


def _resident_spec(shape, layer=None):
    nd = len(shape)
    if layer is None:
        return pl.BlockSpec(shape, lambda i: (0,) * nd, pipeline_mode=pl.Buffered(1))
    return pl.BlockSpec((None,) + tuple(shape), lambda i: (layer,) + (0,) * nd, pipeline_mode=pl.Buffered(1))


def _inproj_body(x_ref, gain_ref, w_ref, wg_ref, scale_ref, oqk_ref, ov_ref, ogate_ref, og_ref,
                 *maybe_grow_ref, silu_gate):
    n_groups = x_ref.shape[0] // GROUP_ROWS
    groups = [slice(a * GROUP_ROWS, (a + 1) * GROUP_ROWS) for a in range(n_groups)]
    h = [None] * n_groups
    for a in range(n_groups + 1):
        if a < n_groups:
            h[a] = _rms(x_ref[groups[a], :], gain_ref[...]).astype(BF16)
        if a > 0:
            rows, hb = groups[a - 1], h[a - 1]
            narrow = jnp.dot(hb, wg_ref[...], preferred_element_type=F32)
            og_ref[rows, :] = narrow
            for grow_ref in maybe_grow_ref:
                grow_ref[:, rows] = narrow.T[0:SUBLANES, :]
            qk = jnp.dot(hb, w_ref[:, 0:2 * QK_WIDTH], preferred_element_type=F32)
            oqk_ref[rows, :] = qk * scale_ref[...]
            v = jnp.dot(hb, w_ref[:, 2 * QK_WIDTH:2 * QK_WIDTH + V_WIDTH], preferred_element_type=F32)
            ov_ref[rows, :] = v.astype(BF16)
            gate = jnp.dot(hb, w_ref[:, 2 * QK_WIDTH + V_WIDTH:MAIN_WIDTH], preferred_element_type=F32)
            ogate_ref[rows, :] = gate * _sigmoid(gate) if silu_gate else _sigmoid(gate)


def _inproj(x, gain, w_in, layer, w_gate, qk_scale, silu_gate, gate_rows):
    m = x.shape[0]
    tm = TM_IN if m >= 4 * TM_IN else GROUP_ROWS
    assert m % tm == 0
    rows_spec = lambda width: pl.BlockSpec((tm, width), lambda i: (i, 0))
    out_specs = [rows_spec(2 * QK_WIDTH), rows_spec(V_WIDTH), rows_spec(V_WIDTH), rows_spec(LANES)]
    out_shape = [
        jax.ShapeDtypeStruct((m, 2 * QK_WIDTH), F32),
        jax.ShapeDtypeStruct((m, V_WIDTH), BF16),
        jax.ShapeDtypeStruct((m, V_WIDTH), F32),
        jax.ShapeDtypeStruct((m, LANES), F32),
    ]
    if gate_rows:
        out_specs.append(pl.BlockSpec((SUBLANES, tm), lambda i: (0, i)))
        out_shape.append(jax.ShapeDtypeStruct((SUBLANES, m), F32))
    return pl.pallas_call(
        functools.partial(_inproj_body, silu_gate=silu_gate),
        grid=(m // tm,),
        in_specs=[rows_spec(D_MODEL), _resident_spec((1, D_MODEL)), _resident_spec(w_in.shape[1:], layer),
                  _resident_spec((D_MODEL, LANES)), _resident_spec((1, 2 * QK_WIDTH))],
        out_specs=out_specs,
        out_shape=out_shape,
        compiler_params=pltpu.CompilerParams(dimension_semantics=("arbitrary",), vmem_limit_bytes=VMEM_LIMIT),
        name="inproj",
    )(x, gain, w_in, w_gate, qk_scale)


def _tail_body(hp_ref, hs_ref, gp_ref, gs_ref, ghead_ref, wo_ref, xp_ref, xs_ref, gpm_ref, gpf_ref, gqf_ref,
               wg_ref, wu_ref, wd_ref, op_ref, os_ref, *, n_prompt_blocks):
    is_prompt = pl.program_id(0) < n_prompt_blocks
    d_ff = wg_ref.shape[1]
    n_tiles = d_ff // MXU_DIM
    bounds = [MXU_DIM * ((n_tiles * t + FF_SPLITS - 1) // FF_SPLITS) for t in range(FF_SPLITS + 1)]

    def stages(raw_ref, gate_ref, x_ref, out_ref, rows):
        raw = raw_ref[rows, :]
        gain = ghead_ref[...]
        y = jnp.concatenate(
            [gate_ref[rows, h * DV:(h + 1) * DV] * _rms(raw[:, h * DV:(h + 1) * DV], gain[:, h * DV:(h + 1) * DV])
             for h in range(N_HEADS)], axis=1).astype(BF16)
        yield
        mix = jnp.dot(y, wo_ref[...], preferred_element_type=F32)
        yield
        x1 = x_ref[rows, :] + _rms(mix, gpm_ref[...])
        h = _rms(x1, gpf_ref[...]).astype(BF16)
        yield
        ffn = None
        for t in range(FF_SPLITS):
            cols = slice(bounds[t], bounds[t + 1])
            gate = jnp.dot(h, wg_ref[:, cols], preferred_element_type=F32)
            up = jnp.dot(h, wu_ref[:, cols], preferred_element_type=F32)
            yield
            act = (gate * _sigmoid(gate) * up).astype(BF16)
            yield
            part = jnp.dot(act, wd_ref[cols, :], preferred_element_type=F32)
            ffn = part if ffn is None else ffn + part
            yield
        out_ref[rows, :] = x1 + _rms(ffn, gqf_ref[...])
        yield

    def block(raw_ref, gate_ref, x_ref, out_ref):
        _emit_staggered([stages(raw_ref, gate_ref, x_ref, out_ref, slice(a * GROUP_ROWS, (a + 1) * GROUP_ROWS))
                         for a in range(TM // GROUP_ROWS)])

    pl.when(is_prompt)(lambda: block(hp_ref, gp_ref, xp_ref, op_ref))
    pl.when(jnp.logical_not(is_prompt))(lambda: block(hs_ref, gs_ref, xs_ref, os_ref))


def _block_tail(hp, hs, gate_p, gate_s, ghead, w_out, xp, xs, g_post_mix, g_pre_ffn, g_post_ffn,
                w_gate, w_up, w_down, layer):
    npb = xp.shape[0] // TM
    m = xp.shape[0] + xs.shape[0]
    d_ff = w_gate.shape[2]
    assert d_ff % MXU_DIM == 0
    in_p, in_s = _two_group_specs(npb, D_MODEL, single_buffer_sample=True)
    vec = _resident_spec((1, D_MODEL))
    return pl.pallas_call(
        functools.partial(_tail_body, n_prompt_blocks=npb),
        grid=(m // TM,),
        in_specs=[
            in_p, in_s,
            in_p, in_s,
            _resident_spec((1, V_WIDTH)),
            _resident_spec((V_WIDTH, D_MODEL)),
            in_p, in_s,
            vec, vec, vec,
            _resident_spec((D_MODEL, d_ff), layer), _resident_spec((D_MODEL, d_ff), layer),
            _resident_spec((d_ff, D_MODEL), layer),
        ],
        out_specs=list(_two_group_specs(npb, D_MODEL)),
        out_shape=[jax.ShapeDtypeStruct(xp.shape, F32), jax.ShapeDtypeStruct(xs.shape, F32)],
        compiler_params=pltpu.CompilerParams(dimension_semantics=("arbitrary",), vmem_limit_bytes=VMEM_LIMIT),
        name="block_tail",
    )(hp, hs, gate_p, gate_s, ghead, w_out, xp, xs, g_post_mix, g_pre_ffn, g_post_ffn, w_gate, w_up, w_down)


def _segment_structure(rows, seg_len):
    t = np.arange(rows)
    seg = t // seg_len
    same = seg[:, None] == seg[None, :]
    cum = same & (t[None, :] <= t[:, None])
    return t, seg, same, cum


def _mlstm_constants(rows, seg_len):
    _, _, same, cum = _segment_structure(rows, seg_len)
    col = np.concatenate([cum, same], axis=0).astype(np.float32)
    row = cum.T.astype(np.float32)
    return jnp.asarray(col, BF16), jnp.asarray(row, BF16)


def _mlstm_rows(qk_ref, v_ref, gcol_ref, grow_ref, bias_row_ref, bias_col_ref, ccol_ref, crow_ref,
                c_in, n_in, m_in, c_out, n_out, m_out, h_ref, rows, seg_len):
    nseg = rows // seg_len
    r = lax.broadcasted_iota(jnp.int32, (rows, rows), 0)
    c = lax.broadcasted_iota(jnp.int32, (rows, rows), 1)
    same = None if nseg == 1 else (r // seg_len) == (c // seg_len)
    lower = (c <= r) if nseg == 1 else same & (c <= r)
    row_seg = lax.broadcasted_iota(jnp.int32, (rows, DQK), 0) // seg_len

    pre_col = gcol_ref[...] + bias_row_ref[...]
    grow = grow_ref[0] if len(grow_ref.shape) == 3 else grow_ref[...]
    pre_row = grow + bias_col_ref[...]
    ccol = ccol_ref[...]
    crow = crow_ref[...]
    hi, lo = _split_hi_lo(_log_sigmoid(pre_col))
    sums_col = jnp.dot(ccol, hi, preferred_element_type=F32) + jnp.dot(ccol, lo, preferred_element_type=F32)
    hi, lo = _split_hi_lo(_log_sigmoid(pre_row))
    sums_row = jnp.dot(hi, crow, preferred_element_type=F32) + jnp.dot(lo, crow, preferred_element_type=F32)
    yield

    qk = qk_ref[...]
    c_prev = [[c_in[s, h] for h in range(N_HEADS)] for s in range(nseg)]
    n_state = [[n_in[s, h:h + 1, :] for h in range(N_HEADS)] for s in range(nseg)]
    m_state = [[m_in[s, h:h + 1, 0:1] for h in range(N_HEADS)] for s in range(nseg)]
    writes = []

    heads = range(N_HEADS)
    q = [qk[:, h * DQK:(h + 1) * DQK] for h in heads]
    k = [qk[:, QK_WIDTH + h * DQK:QK_WIDTH + (h + 1) * DQK] for h in heads]
    v_bf = [v_ref[:, h * DV:(h + 1) * DV] for h in heads]
    i_col = [pre_col[:, h:h + 1] for h in heads]
    b_col = [sums_col[0:rows, N_HEADS + h:N_HEADS + h + 1] for h in heads]
    b_last = [sums_col[rows:2 * rows, N_HEADS + h:N_HEADS + h + 1] for h in heads]
    src = [pre_row[h:h + 1, :] - sums_row[N_HEADS + h:N_HEADS + h + 1, :] for h in heads]
    m_prev = [_rows_per_segment([m_state[s][h] for s in range(nseg)], seg_len) for h in heads]
    n_prev = [_rows_per_segment([n_state[s][h] for s in range(nseg)], seg_len) for h in heads]

    dmat = [jnp.where(lower, b_col[h] + src[h], -jnp.inf) for h in heads]
    inter = [b_col[h] + m_prev[h] for h in heads]
    m_t = [jnp.maximum(inter[h], jnp.max(dmat[h], axis=1, keepdims=True)) for h in heads]
    if nseg == 1:
        m_new = [jnp.broadcast_to(m_t[h][rows - 1:rows, :], (rows, 1)) for h in heads]
    else:
        dlast = [jnp.where(same, b_last[h] + src[h], -jnp.inf) for h in heads]
        m_new = [jnp.maximum(b_last[h] + m_prev[h], jnp.max(dlast[h], axis=1, keepdims=True)) for h in heads]
    yield
    scores = [_dot_nt(q[h], k[h]) for h in heads]
    q_bf = [q[h].astype(BF16) for h in heads]
    qc = []
    for h in heads:
        parts = []
        for s in range(nseg):
            full = jnp.dot(q_bf[h], c_prev[s][h].astype(BF16), preferred_element_type=F32)
            parts.append(full[s * seg_len:(s + 1) * seg_len] if nseg > 1 else full)
        qc.append(parts[0] if nseg == 1 else jnp.concatenate(parts, axis=0))
    qn_terms = [q[h] * n_prev[h] for h in heads]
    yield

    w_inter = [jnp.exp(inter[h] - m_t[h]) for h in heads]
    a = [scores[h] * jnp.exp(dmat[h] - m_t[h]) for h in heads]
    w_s = [jnp.exp(b_last[h] - b_col[h] + i_col[h] - m_new[h]) for h in heads]
    decay = [jnp.exp(b_last[h] + m_prev[h] - m_new[h]) for h in heads]
    kw = [k[h] * w_s[h] for h in heads]
    yield

    num = [jnp.dot(a[h].astype(BF16), v_bf[h], preferred_element_type=F32) + w_inter[h] * qc[h] for h in heads]
    if rows == DQK:
        den = [jnp.sum(a[h] + w_inter[h] * qn_terms[h], axis=1, keepdims=True) for h in heads]
    else:
        den = [jnp.sum(a[h], axis=1, keepdims=True) + w_inter[h] * jnp.sum(qn_terms[h], axis=1, keepdims=True)
               for h in heads]
    for h in heads:
        for s in range(nseg):
            kw_s = kw[h] if nseg == 1 else jnp.where(row_seg == s, kw[h], 0.0)
            dec = decay[h][s * seg_len:s * seg_len + 1, :]
            writes.append((s, h, dec * c_prev[s][h] + _dot_tn(kw_s, v_bf[h]),
                           dec * n_state[s][h] + jnp.sum(kw_s, axis=0, keepdims=True),
                           jnp.broadcast_to(m_new[h][s * seg_len:s * seg_len + 1, :], (1, LANES))))

    yield
    for h in heads:
        h_ref[:, h * DV:(h + 1) * DV] = num[h] * (1.0 / jnp.maximum(jnp.abs(den[h]), jnp.exp(-m_t[h])))

    for s, h, c_new, n_new, m_new in writes:
        c_out[s, h] = c_new
        n_out[s, h:h + 1, :] = n_new
        m_out[s, h:h + 1, :] = m_new
    yield


def _mlstm_body(qk_p, v_p, gcol_p, grow_p, qk_s, v_s, gcol_s, grow_s, brow_ref, bcol_ref,
                ccol_p, crow_p, ccol_s, crow_s, c_in, n_in, m_in,
                h_p, c_p, n_p, m_p, h_s, c_out, n_out, m_out, *, n_prompt_seq, seg_len):
    @pl.when(pl.program_id(0) == 0)
    def _():
        c_p[...] = jnp.zeros_like(c_p)
        n_p[...] = jnp.zeros_like(n_p)
        m_p[...] = jnp.zeros_like(m_p)

    chains = []
    for a in range(n_prompt_seq):
        one = pl.ds(a, 1)
        c, n, m = c_p.at[one], n_p.at[one], m_p.at[one]
        chains.append(_mlstm_rows(qk_p.at[a], v_p.at[a], gcol_p.at[a], grow_p.at[a], brow_ref, bcol_ref, ccol_p, crow_p,
                                  c, n, m, c, n, m, h_p.at[a], PROMPT_ROWS, PROMPT_ROWS))
    chains.append(_mlstm_rows(qk_s, v_s, gcol_s, grow_s, brow_ref, bcol_ref, ccol_s, crow_s,
                              c_in, n_in, m_in, c_out, n_out, m_out, h_s, SAMPLE_ROWS, seg_len))
    _emit_staggered(chains)


def _gla_constants(rows, seg_len):
    t, seg, _, cum = _segment_structure(rows, seg_len)
    level_id = np.full((rows, rows), -1, np.int32)
    level_id[t, t] = 0
    for idx, hb in enumerate(LEVELS):
        if 2 * hb > seg_len:
            continue
        block = t // (2 * hb)
        upper = (t % (2 * hb)) >= hb
        level_id[(block[:, None] == block[None, :]) & upper[:, None] & ~upper[None, :]] = idx + 1
    segind = (seg[:, None] == np.arange(LANES)[None, :]).astype(np.float32)
    return jnp.asarray(cum.astype(np.float32), BF16), jnp.asarray(level_id), jnp.asarray(segind, BF16)


def _row_bcast(x, row, n):
    return jnp.broadcast_to(x[row:row + 1, :], (n, x.shape[1]))


def _gla_rows(qk_ref, v_ref, glr_ref, w2_ref, bg_ref, cum_ref, lvl_ref, segind_ref,
              s_in, s_out, o_ref, rows, seg_len):
    nseg = rows // seg_len
    ntile = rows // SUBLANES
    row_id = lax.broadcasted_iota(jnp.int32, (rows, DQK), 0)
    row_seg = row_id // seg_len
    row_in_tile = row_id % SUBLANES

    pre = _dot(glr_ref[...], w2_ref[...]) + bg_ref[...]
    g_all = _log_sigmoid(pre) * (LOG2_E / GATE_TAU)
    g_hi, g_lo = _split_hi_lo(g_all)
    cum = cum_ref[...]
    bc_all = jnp.dot(cum, g_hi, preferred_element_type=F32) + jnp.dot(cum, g_lo, preferred_element_type=F32)
    level_tiles = [lvl_ref[t * SUBLANES:(t + 1) * SUBLANES, :] for t in range(ntile)]
    segind = segind_ref[...]
    total_col = (lax.dot_general(g_hi, segind, (((0,), (0,)), ((), ())), preferred_element_type=F32)
                 + lax.dot_general(g_lo, segind, (((0,), (0,)), ((), ())), preferred_element_type=F32))
    yield

    qk = qk_ref[...]
    s_prev = [[s_in[s, h] for h in range(N_HEADS)] for s in range(nseg)]
    writes = []

    heads = range(N_HEADS)
    sl = [slice(h * DQK, (h + 1) * DQK) for h in heads]
    q = [qk[:, sl[h]] for h in heads]
    k = [qk[:, QK_WIDTH + h * DQK:QK_WIDTH + (h + 1) * DQK] for h in heads]
    v_bf = [v_ref[:, h * DV:(h + 1) * DV] for h in heads]
    g = [g_all[:, sl[h]] for h in heads]
    bc = [bc_all[:, sl[h]] for h in heads]
    seg_total = [jnp.concatenate([_row_bcast(bc[h], (s + 1) * seg_len - 1, seg_len) for s in range(nseg)], axis=0)
                 if nseg > 1 else _row_bcast(bc[h], rows - 1, rows) for h in heads]

    qq = [(q[h] * jnp.exp2(bc[h])).astype(BF16) for h in heads]
    kk = [k[h] * jnp.exp2(seg_total[h] - bc[h]) for h in heads]
    o_inter = []
    for h in heads:
        parts = []
        for s in range(nseg):
            st = s_prev[s][h]
            oi = jnp.dot(qq[h], st.astype(BF16), preferred_element_type=F32)
            parts.append(oi[s * seg_len:(s + 1) * seg_len] if nseg > 1 else oi)
            kk_s = kk[h] if nseg == 1 else jnp.where(row_seg == s, kk[h], 0.0)
            writes.append((s, h, st * jnp.exp2(total_col[sl[h], s:s + 1]) + _dot_tn(kk_s, v_bf[h])))
        o_inter.append(parts[0] if nseg == 1 else jnp.concatenate(parts, axis=0))
    yield

    a = [[None] * ntile for h in heads]

    def put(h, t, level, z_rows):
        prev = 0.0 if a[h][t] is None else a[h][t]
        a[h][t] = jnp.where(level_tiles[t] == level, z_rows, prev)

    for h in heads:
        z = _dot_nt(q[h], k[h])
        for t in range(ntile):
            put(h, t, 0, z[t * SUBLANES:(t + 1) * SUBLANES])

    for idx, hb in enumerate(LEVELS):
        if 2 * hb > seg_len:
            continue
        level = idx + 1
        if hb >= SUBLANES:
            starts = range(0, rows, 2 * hb)
            for h in heads:
                e_parts, x_parts = [], []
                for r0 in starts:
                    ref = _row_bcast(bc[h], r0 + hb - 1, hb)
                    e_parts += [ref - bc[h][r0:r0 + hb], bc[h][r0 + hb:r0 + 2 * hb] - ref]
                    x_parts += [k[h][r0:r0 + hb], q[h][r0 + hb:r0 + 2 * hb]]
                u = jnp.concatenate(x_parts, axis=0) * jnp.exp2(jnp.concatenate(e_parts, axis=0))
                u_upper = jnp.concatenate([u[r0 + hb:r0 + 2 * hb] for r0 in starts], axis=0)
                z = _dot_nt(u_upper, u)
                upper_tiles = [t for r0 in starts for t in range((r0 + hb) // SUBLANES, (r0 + 2 * hb) // SUBLANES)]
                for i, t in enumerate(upper_tiles):
                    put(h, t, level, z[i * SUBLANES:(i + 1) * SUBLANES])
        else:
            upper = (row_id % (2 * hb)) >= hb
            for h in heads:
                if hb == 1:
                    e_l = jnp.where(upper, g[h], 0.0)
                else:
                    tiles = range(ntile)
                    if hb == 4:
                        ref = jnp.concatenate([_row_bcast(bc[h], t * SUBLANES + 3, SUBLANES) for t in tiles], axis=0)
                    else:
                        lo = jnp.concatenate([_row_bcast(bc[h], t * SUBLANES + 1, SUBLANES) for t in tiles], axis=0)
                        hi = jnp.concatenate([_row_bcast(bc[h], t * SUBLANES + 5, SUBLANES) for t in tiles], axis=0)
                        ref = jnp.where(row_in_tile < 4, lo, hi)
                    d = bc[h] - ref
                    e_l = jnp.where(upper, d, -d)
                u = jnp.where(upper, q[h], k[h]) * jnp.exp2(e_l)
                z = _dot_nt(u, u)
                for t in range(ntile):
                    put(h, t, level, z[t * SUBLANES:(t + 1) * SUBLANES])
        yield

    for h in heads:
        o_ref[:, h * DV:(h + 1) * DV] = jnp.dot(jnp.concatenate(a[h], axis=0).astype(BF16), v_bf[h],
                                                preferred_element_type=F32) + o_inter[h]

    for s, h, s_new in writes:
        s_out[s, h] = s_new
    yield


def _gla_body(qk_p, v_p, glr_p, qk_s, v_s, glr_s, w2_ref, bg_ref, cum_p, lvl_p, segind_p, cum_s, lvl_s, segind_s,
              s_in, o_p, s_p, o_s, s_out, *, n_prompt_seq, seg_len):
    @pl.when(pl.program_id(0) == 0)
    def _():
        s_p[...] = jnp.zeros_like(s_p)

    chains = []
    for a in range(n_prompt_seq):
        state = s_p.at[pl.ds(a, 1)]
        chains.append(_gla_rows(qk_p.at[a], v_p.at[a], glr_p.at[a], w2_ref, bg_ref, cum_p, lvl_p, segind_p,
                                state, state, o_p.at[a], PROMPT_ROWS, PROMPT_ROWS))
    chains.append(_gla_rows(qk_s, v_s, glr_s, w2_ref, bg_ref, cum_s, lvl_s, segind_s,
                            s_in, s_out, o_s, SAMPLE_ROWS, seg_len))
    _emit_staggered(chains)


def _mixer_params(n_axes):
    return pltpu.CompilerParams(dimension_semantics=("arbitrary",) * n_axes, vmem_limit_bytes=MIXER_VMEM_LIMIT)


def _state_spec(n, trailing):
    nd = 1 + len(trailing)
    return pl.BlockSpec((n,) + trailing, lambda *ids: (ids[0],) + (0,) * (nd - 1))


def _mlstm_mixer(prompt, sample, bias_i, bias_f, c0, n0, m0, n_prompt_seq, prompt_len):
    qk_p, v_p, gcol_p, grow_p = prompt
    qk_s, v_s, gcol_s, grow_s = sample
    n_sample_seq = c0.shape[0]
    seg_len = qk_s.shape[0] // n_sample_seq
    bias = jnp.concatenate([bias_i, bias_f]).astype(F32)
    bias_row = jnp.zeros((1, LANES), F32).at[0, :2 * N_HEADS].set(bias)
    bias_col = bias.reshape(2 * N_HEADS, 1)

    nb, plen = n_prompt_seq, prompt_len
    steps = plen // PROMPT_ROWS
    spb = SAMPLE_ROWS // seg_len
    assert n_sample_seq == steps * spb, "one block of sample sequences per prompt chunk step"
    ccol_p, crow_p = _mlstm_constants(PROMPT_ROWS, PROMPT_ROWS)
    ccol_s, crow_s = _mlstm_constants(SAMPLE_ROWS, seg_len)
    consts = [bias_row, bias_col, ccol_p, crow_p, ccol_s, crow_s]
    m0b = jnp.broadcast_to(m0[:, :, None], m0.shape + (LANES,))
    grow_p = jnp.transpose(grow_p.reshape(2 * N_HEADS, nb, plen), (1, 0, 2))
    grow_s = jnp.transpose(grow_s.reshape(2 * N_HEADS, steps, SAMPLE_ROWS), (1, 0, 2))
    seq_rows = lambda width: pl.BlockSpec((nb, PROMPT_ROWS, width), lambda i: (0, i, 0))
    blk_rows = lambda width: pl.BlockSpec((SAMPLE_ROWS, width), lambda i: (i, 0))
    whole = lambda shape: pl.BlockSpec(shape, lambda i: (0,) * len(shape))
    state_specs = [_state_spec(spb, (N_HEADS, DQK, DV)), _state_spec(spb, (N_HEADS, DQK)),
                   _state_spec(spb, (N_HEADS, LANES))]
    h_p, c_p, n_p, m_p, h_s, c_s, n_s, m_s = pl.pallas_call(
        functools.partial(_mlstm_body, n_prompt_seq=nb, seg_len=seg_len),
        grid=(steps,),
        in_specs=[seq_rows(2 * QK_WIDTH), seq_rows(V_WIDTH), seq_rows(LANES),
                  pl.BlockSpec((nb, 2 * N_HEADS, PROMPT_ROWS), lambda i: (0, 0, i)),
                  blk_rows(2 * QK_WIDTH), blk_rows(V_WIDTH), blk_rows(LANES),
                  pl.BlockSpec((1, 2 * N_HEADS, SAMPLE_ROWS), lambda i: (i, 0, 0))]
        + [_const_spec(a.shape) for a in consts] + state_specs,
        out_specs=[seq_rows(V_WIDTH), whole((nb, N_HEADS, DQK, DV)), whole((nb, N_HEADS, DQK)),
                   whole((nb, N_HEADS, LANES)), blk_rows(V_WIDTH)] + state_specs,
        out_shape=[
            jax.ShapeDtypeStruct((nb, plen, V_WIDTH), F32),
            jax.ShapeDtypeStruct((nb, N_HEADS, DQK, DV), F32),
            jax.ShapeDtypeStruct((nb, N_HEADS, DQK), F32),
            jax.ShapeDtypeStruct((nb, N_HEADS, LANES), F32),
            jax.ShapeDtypeStruct((n_sample_seq * seg_len, V_WIDTH), F32),
            jax.ShapeDtypeStruct(c0.shape, F32),
            jax.ShapeDtypeStruct(n0.shape, F32),
            jax.ShapeDtypeStruct(m0b.shape, F32),
        ],
        compiler_params=_mixer_params(1),
        name="mlstm_mixer",
    )(qk_p.reshape(nb, plen, -1), v_p.reshape(nb, plen, -1), gcol_p.reshape(nb, plen, -1), grow_p,
      qk_s, v_s, gcol_s, grow_s, *consts, c0, n0, m0b)

    return (h_p.reshape(nb * plen, V_WIDTH), h_s), (c_p, n_p, m_p[:, :, 0]), (c_s, n_s, m_s[:, :, 0])


def _gla_mixer(prompt, sample, w_gate2, b_gate, s0, n_prompt_seq, prompt_len):
    qk_p, v_p, glr_p = prompt
    qk_s, v_s, glr_s = sample
    n_sample_seq = s0.shape[0]
    seg_len = qk_s.shape[0] // n_sample_seq
    nb, plen = n_prompt_seq, prompt_len
    steps = plen // PROMPT_ROWS
    spb = SAMPLE_ROWS // seg_len
    assert n_sample_seq == steps * spb, "one block of sample sequences per prompt chunk step"
    w2 = jnp.zeros((LANES, QK_WIDTH), BF16).at[:GATE_RANK].set(w_gate2.astype(BF16))
    bg = b_gate.reshape(1, QK_WIDTH).astype(F32)
    consts = [w2, bg, *_gla_constants(PROMPT_ROWS, PROMPT_ROWS), *_gla_constants(SAMPLE_ROWS, seg_len)]
    seq_rows = lambda width: pl.BlockSpec((nb, PROMPT_ROWS, width), lambda i: (0, i, 0))
    blk_rows = lambda width: pl.BlockSpec((SAMPLE_ROWS, width), lambda i: (i, 0))
    state_spec = _state_spec(spb, (N_HEADS, DQK, DV))
    o_p, s_p, o_s, s_s = pl.pallas_call(
        functools.partial(_gla_body, n_prompt_seq=nb, seg_len=seg_len),
        grid=(steps,),
        in_specs=[seq_rows(2 * QK_WIDTH), seq_rows(V_WIDTH), seq_rows(LANES),
                  blk_rows(2 * QK_WIDTH), blk_rows(V_WIDTH), blk_rows(LANES)]
        + [_const_spec(a.shape) for a in consts] + [state_spec],
        out_specs=[seq_rows(V_WIDTH), pl.BlockSpec((nb, N_HEADS, DQK, DV), lambda i: (0, 0, 0, 0)),
                   blk_rows(V_WIDTH), state_spec],
        out_shape=[
            jax.ShapeDtypeStruct((nb, plen, V_WIDTH), F32),
            jax.ShapeDtypeStruct((nb, N_HEADS, DQK, DV), F32),
            jax.ShapeDtypeStruct((n_sample_seq * seg_len, V_WIDTH), F32),
            jax.ShapeDtypeStruct(s0.shape, F32),
        ],
        compiler_params=_mixer_params(1),
        name="gla_mixer",
    )(qk_p.reshape(nb, plen, -1), v_p.reshape(nb, plen, -1), glr_p.reshape(nb, plen, -1),
      qk_s, v_s, glr_s, *consts, s0)

    return (o_p.reshape(nb * plen, V_WIDTH), o_s), s_p, s_s


def _pad_gate_columns(w):
    return jnp.zeros((D_MODEL, LANES), BF16).at[:, :w.shape[1]].set(w.astype(BF16))


def kernel(x_prompt, x_sample, state_mlstm_C, state_mlstm_n, state_mlstm_m, state_gla_S, g_pre_mix, g_post_mix, g_pre_ffn, g_post_ffn, w_in_mlstm, b_i_mlstm, b_f_mlstm, g_head_mlstm, w_out_mlstm, w_in_gla, w_gate2_gla, b_gate_gla, g_head_gla, w_out_gla, w_ffn_gate, w_ffn_up, w_ffn_down):
    bp, sp, d = x_prompt.shape
    bs, ss, _ = x_sample.shape
    depth = g_pre_mix.shape[0]
    assert d == D_MODEL and sp % PROMPT_ROWS == 0 and SAMPLE_ROWS % ss == 0 and (bs * ss) % SAMPLE_ROWS == 0
    assert (bp * sp) % TM == 0 and (bs * ss) % TM == 0

    xp = x_prompt.reshape(bp * sp, d)
    xs = x_sample.reshape(bs * ss, d)
    vec = lambda g: g.reshape(1, D_MODEL).astype(F32)

    prompt_states = {"C": [], "n": [], "m": [], "S": []}
    sample_states = {"C": [], "n": [], "m": [], "S": []}
    ffn_gate, ffn_up, ffn_down = (w.astype(BF16) for w in (w_ffn_gate, w_ffn_up, w_ffn_down))
    in_mlstm, in_gla = w_in_mlstm.astype(BF16), w_in_gla.astype(BF16)
    ones = jnp.ones((QK_WIDTH,), F32)
    head_scale = jnp.full((QK_WIDTH,), DQK ** -0.5, F32)
    for layer in range(depth):
        j = layer // 2
        if layer % 2 == 0:
            project = lambda x: _inproj(x, vec(g_pre_mix[layer]), in_mlstm, j,
                                        _pad_gate_columns(w_in_mlstm[j][:, MAIN_WIDTH:]),
                                        jnp.concatenate([ones, head_scale]).reshape(1, -1),
                                        silu_gate=False, gate_rows=True)
            (qk_p, v_p, gate_p, gcol_p, grow_p), (qk_s, v_s, gate_s, gcol_s, grow_s) = project(xp), project(xs)
            h, st_p, st_s = _mlstm_mixer((qk_p, v_p, gcol_p, grow_p), (qk_s, v_s, gcol_s, grow_s),
                                         b_i_mlstm[j], b_f_mlstm[j],
                                         state_mlstm_C[j], state_mlstm_n[j], state_mlstm_m[j], bp, sp)
            for dst, st in ((prompt_states, st_p), (sample_states, st_s)):
                dst["C"].append(st[0]); dst["n"].append(st[1]); dst["m"].append(st[2])
            w_out, g_head = w_out_mlstm[j], g_head_mlstm[j]
        else:
            project = lambda x: _inproj(x, vec(g_pre_mix[layer]), in_gla, j,
                                        _pad_gate_columns(w_in_gla[j][:, MAIN_WIDTH:]),
                                        jnp.concatenate([head_scale, ones]).reshape(1, -1),
                                        silu_gate=True, gate_rows=False)
            (qk_p, v_p, gate_p, glr_p), (qk_s, v_s, gate_s, glr_s) = project(xp), project(xs)
            h, s_p, s_s = _gla_mixer((qk_p, v_p, glr_p), (qk_s, v_s, glr_s), w_gate2_gla[j], b_gate_gla[j],
                                     state_gla_S[j], bp, sp)
            prompt_states["S"].append(s_p)
            sample_states["S"].append(s_s)
            w_out, g_head = w_out_gla[j], g_head_gla[j]
        xp, xs = _block_tail(h[0], h[1], gate_p, gate_s, g_head.reshape(1, V_WIDTH).astype(F32), w_out.astype(BF16),
                             xp, xs,
                             vec(g_post_mix[layer]), vec(g_pre_ffn[layer]), vec(g_post_ffn[layer]),
                             ffn_gate, ffn_up, ffn_down, layer)

    stack = lambda xs_: jnp.stack(xs_)
    return (xp.reshape(bp, sp, d), xs.reshape(bs, ss, d),
            stack(prompt_states["C"]), stack(prompt_states["n"]), stack(prompt_states["m"]), stack(prompt_states["S"]),
            stack(sample_states["C"]), stack(sample_states["n"]), stack(sample_states["m"]), stack(sample_states["S"]))
```
